```python
import jax, jax.numpy as jnp
from jax import lax
import numpy as np

D_MODEL = 1024
BATCH = 8
SEQ = 8192
DEPTH = 4

CHUNK = 64
N_HEADS = 16
HEAD_DIM = D_MODEL // N_HEADS
D_FF = 4 * D_MODEL
N_MIXERS = 2
LEFT_CHUNKS = 8
BAND = (LEFT_CHUNKS + 1) * CHUNK
MAX_REL = 256
N_REL = 2 * MAX_REL + 1
Q_BLOCK = 128
RMS_EPS = 1e-6
N_SB_LAYERS = (DEPTH + N_MIXERS - 1) // N_MIXERS
N_CA_LAYERS = DEPTH // N_MIXERS

kernel_name = "hybrid_stickbreak_chunkrel_trunk"


def rms_norm(x, gain):
    x32 = x.astype(jnp.float32)
    y = x32 * lax.rsqrt(jnp.mean(x32 * x32, axis=-1, keepdims=True) + RMS_EPS)
    return (y * gain.astype(jnp.float32)).astype(x.dtype)


def split_heads(t):
    b, s, _ = t.shape
    return jnp.transpose(t.reshape(b, s, N_HEADS, HEAD_DIM), (0, 2, 1, 3))


def merge_heads(t):
    b, h, s, d = t.shape
    return jnp.transpose(t, (0, 2, 1, 3)).reshape(b, s, h * d)


def stick_breaking_attention(q, k, v):
    seq = q.shape[2]
    scale = HEAD_DIM ** -0.5
    outs = []
    for qb in range(seq // Q_BLOCK):
        q0 = qb * Q_BLOCK
        kl = q0 + Q_BLOCK
        qblk = q[:, :, q0:kl].astype(jnp.float32)
        kblk = k[:, :, :kl].astype(jnp.float32)
        vblk = v[:, :, :kl].astype(jnp.float32)
        z = jnp.einsum('bhqd,bhkd->bhqk', qblk, kblk) * scale
        t_pos = q0 + jnp.arange(Q_BLOCK)[:, None]
        s_pos = jnp.arange(kl)[None, :]
        valid = s_pos < t_pos
        log_1mb = jnp.where(valid, -jax.nn.softplus(z), 0.0)
        between = lax.cumsum(log_1mb, axis=3, reverse=True) - log_1mb
        log_a = jax.nn.log_sigmoid(z) + between
        a = jnp.where(valid, jnp.exp(log_a), 0.0)
        outs.append(jnp.einsum('bhqk,bhkd->bhqd', a, vblk))
    return jnp.concatenate(outs, axis=2).astype(q.dtype)


def head_rms_norm(t, gain):
    return rms_norm(t, gain)


def chunked_relpos_attention(q, k, v, rel_bias):
    b, h, seq, dh = q.shape
    n_chunks = seq // CHUNK
    pad = LEFT_CHUNKS * CHUNK
    kp = jnp.pad(k, ((0, 0), (0, 0), (pad, 0), (0, 0)))
    vp = jnp.pad(v, ((0, 0), (0, 0), (pad, 0), (0, 0)))
    qi = jnp.arange(CHUNK)[:, None]
    kj = jnp.arange(BAND)[None, :]
    rel_idx = jnp.clip(qi + pad - kj, -MAX_REL, MAX_REL) + MAX_REL
    bias = rel_bias[:, rel_idx].astype(jnp.float32)
    scale = dh ** -0.5

    def one_chunk(c):
        start = c * CHUNK
        qc = lax.dynamic_slice_in_dim(q, start, CHUNK, axis=2).astype(jnp.float32)
        kc = lax.dynamic_slice_in_dim(kp, start, BAND, axis=2).astype(jnp.float32)
        vc = lax.dynamic_slice_in_dim(vp, start, BAND, axis=2).astype(jnp.float32)
        logits = jnp.einsum('bhqd,bhkd->bhqk', qc, kc) * scale + bias[None]
        key_pos = start - pad + jnp.arange(BAND)
        logits = jnp.where((key_pos >= 0)[None, None, None, :], logits, -jnp.inf)
        p = jax.nn.softmax(logits, axis=-1)
        return jnp.einsum('bhqk,bhkd->bhqd', p, vc)

    out = lax.map(one_chunk, jnp.arange(n_chunks))
    out = jnp.transpose(out, (1, 2, 0, 3, 4)).reshape(b, h, seq, dh)
    return out.astype(q.dtype)


def _fwd_setup_inputs(seed: int = 0) -> dict:
    key = jax.random.key(seed)
    ks = jax.random.split(key, 12)
    f32 = jnp.float32
    x = jax.random.normal(ks[0], (BATCH, SEQ, D_MODEL), f32)
    mix_norm = 1.0 + 0.02 * jax.random.normal(ks[1], (DEPTH, D_MODEL), f32)
    w_qkv = jax.random.normal(ks[2], (DEPTH, D_MODEL, 3 * D_MODEL), f32) * D_MODEL ** -0.5
    w_o = jax.random.normal(ks[3], (DEPTH, D_MODEL, D_MODEL), f32) * D_MODEL ** -0.5
    q_norm = 1.0 + 0.02 * jax.random.normal(ks[4], (N_CA_LAYERS, HEAD_DIM), f32)
    k_norm = 1.0 + 0.02 * jax.random.normal(ks[5], (N_CA_LAYERS, HEAD_DIM), f32)
    rel_bias = 0.1 * jax.random.normal(ks[6], (N_CA_LAYERS, N_HEADS, N_REL), f32)
    ffn_norm = 1.0 + 0.02 * jax.random.normal(ks[7], (DEPTH, D_MODEL), f32)
    w_up = jax.random.normal(ks[8], (DEPTH, D_MODEL, D_FF), f32) * D_MODEL ** -0.5
    w_down = jax.random.normal(ks[9], (DEPTH, D_FF, D_MODEL), f32) * D_FF ** -0.5
    return {"x": x, "mix_norm": mix_norm, "w_qkv": w_qkv, "w_o": w_o,
            "q_norm": q_norm, "k_norm": k_norm, "rel_bias": rel_bias,
            "ffn_norm": ffn_norm, "w_up": w_up, "w_down": w_down}


def _fwd_reference(x, mix_norm, w_qkv, w_o, q_norm, k_norm, rel_bias, ffn_norm, w_up, w_down):
    for layer in range(DEPTH):
        h = rms_norm(x, mix_norm[layer])
        qkv = jnp.einsum('bsd,de->bse', h, w_qkv[layer])
        q, k, v = jnp.split(qkv, 3, axis=-1)
        q, k, v = split_heads(q), split_heads(k), split_heads(v)
        if layer % N_MIXERS == 0:
            o = stick_breaking_attention(q, k, v)
        else:
            idx = layer // N_MIXERS
            q = head_rms_norm(q, q_norm[idx])
            k = head_rms_norm(k, k_norm[idx])
            o = chunked_relpos_attention(q, k, v, rel_bias[idx])
        x = x + jnp.einsum('bsd,de->bse', merge_heads(o), w_o[layer])
        h = rms_norm(x, ffn_norm[layer])
        u = jnp.square(jax.nn.relu(jnp.einsum('bsd,df->bsf', h, w_up[layer])))
        x = x + jnp.einsum('bsf,fd->bsd', u, w_down[layer])
    return x


import jax as _jax
import jax.numpy as _jnp

TWIN_FORMAT = 'train_step'
FWD_PARAMS = ['x', 'mix_norm', 'w_qkv', 'w_o', 'q_norm', 'k_norm', 'rel_bias', 'ffn_norm', 'w_up', 'w_down']
TWIN_WEIGHTS = ['mix_norm', 'w_qkv', 'w_o', 'q_norm', 'k_norm', 'rel_bias', 'ffn_norm', 'w_up', 'w_down']
TWIN_DIFF_INPUT = 'x'
TWIN_INPUTS = ['x', 'mix_norm', 'w_qkv', 'w_o', 'q_norm', 'k_norm', 'rel_bias', 'ffn_norm', 'w_up', 'w_down', 'loss_target', 'm_mix_norm', 'm_w_qkv', 'm_w_o', 'm_q_norm', 'm_k_norm', 'm_rel_bias', 'm_ffn_norm', 'm_w_up', 'm_w_down', 'v_mix_norm', 'v_w_qkv', 'v_w_o', 'v_q_norm', 'v_k_norm', 'v_rel_bias', 'v_ffn_norm', 'v_w_up', 'v_w_down']
TWIN_OUTPUTS = ['loss', 'grad_x', 'grad_mix_norm', 'grad_w_qkv', 'grad_w_o', 'grad_q_norm', 'grad_k_norm', 'grad_rel_bias', 'grad_ffn_norm', 'grad_w_up', 'grad_w_down', 'delta_mix_norm', 'delta_w_qkv', 'delta_w_o', 'delta_q_norm', 'delta_k_norm', 'delta_rel_bias', 'delta_ffn_norm', 'delta_w_up', 'delta_w_down', 'new_m_mix_norm', 'new_m_w_qkv', 'new_m_w_o', 'new_m_q_norm', 'new_m_k_norm', 'new_m_rel_bias', 'new_m_ffn_norm', 'new_m_w_up', 'new_m_w_down', 'new_v_mix_norm', 'new_v_w_qkv', 'new_v_w_o', 'new_v_q_norm', 'new_v_k_norm', 'new_v_rel_bias', 'new_v_ffn_norm', 'new_v_w_up', 'new_v_w_down']
TWIN_LEAF_KINDS = {'loss': 'loss', 'grad_x': 'grad_x', 'grad_mix_norm': 'grad_w', 'grad_w_qkv': 'grad_w', 'grad_w_o': 'grad_w', 'grad_q_norm': 'grad_w', 'grad_k_norm': 'grad_w', 'grad_rel_bias': 'grad_w', 'grad_ffn_norm': 'grad_w', 'grad_w_up': 'grad_w', 'grad_w_down': 'grad_w', 'delta_mix_norm': 'delta_w', 'delta_w_qkv': 'delta_w', 'delta_w_o': 'delta_w', 'delta_q_norm': 'delta_w', 'delta_k_norm': 'delta_w', 'delta_rel_bias': 'delta_w', 'delta_ffn_norm': 'delta_w', 'delta_w_up': 'delta_w', 'delta_w_down': 'delta_w', 'new_m_mix_norm': 'new_m', 'new_m_w_qkv': 'new_m', 'new_m_w_o': 'new_m', 'new_m_q_norm': 'new_m', 'new_m_k_norm': 'new_m', 'new_m_rel_bias': 'new_m', 'new_m_ffn_norm': 'new_m', 'new_m_w_up': 'new_m', 'new_m_w_down': 'new_m', 'new_v_mix_norm': 'new_v', 'new_v_w_qkv': 'new_v', 'new_v_w_o': 'new_v', 'new_v_q_norm': 'new_v', 'new_v_k_norm': 'new_v', 'new_v_rel_bias': 'new_v', 'new_v_ffn_norm': 'new_v', 'new_v_w_up': 'new_v', 'new_v_w_down': 'new_v'}


def _forward(args):
    return _fwd_reference(*[args[k] for k in FWD_PARAMS])


def _output_shape():
    def fwd():
        inp = _fwd_setup_inputs(0)
        return _fwd_reference(*[inp[k] for k in FWD_PARAMS])
    out = _jax.eval_shape(fwd)
    return out.shape, out.dtype

N_MICROBATCH = 1
ADAM_LR = 0.001
ADAM_B1 = 0.9
ADAM_B2 = 0.999
ADAM_EPS = 1e-08
ADAM_WD = 0.01
ADAM_STEP = 10
PER_EXAMPLE_BATCH_AXIS = {'x': 0, 'loss_target': 0}
SHARED_INPUTS = []
_WEIGHT_DTYPES = {'mix_norm': _jnp.float32, 'w_qkv': _jnp.float32, 'w_o': _jnp.float32, 'q_norm': _jnp.float32, 'k_norm': _jnp.float32, 'rel_bias': _jnp.float32, 'ffn_norm': _jnp.float32, 'w_up': _jnp.float32, 'w_down': _jnp.float32}
MOMENT_SCALE = {'mix_norm': 4.695413e+01, 'w_qkv': 2.284647e+01, 'w_o': 3.908546e+01, 'q_norm': 3.986651e+00, 'k_norm': 3.977476e+00, 'rel_bias': 6.494662e-02, 'ffn_norm': 2.048395e+02, 'w_up': 2.073549e+01, 'w_down': 7.557059e+01}


def _to_microbatches(a, axis):
    t = _jnp.moveaxis(a, axis, 0)
    t = t.reshape((N_MICROBATCH, t.shape[0] // N_MICROBATCH) + t.shape[1:])
    return _jnp.moveaxis(t, 1, axis + 1)


def setup_inputs(seed: int = 0) -> dict:
    inp = _fwd_setup_inputs(seed)
    key = _jax.random.fold_in(_jax.random.key(seed), 7919)
    shape, _ = _output_shape()
    out = dict(inp)
    out["loss_target"] = _jax.random.normal(_jax.random.fold_in(key, 0), shape, _jnp.float32)
    for i, name in enumerate(TWIN_WEIGHTS):
        w = inp[name].astype(_jnp.float32)
        if MOMENT_SCALE is None:
            s = _jnp.sqrt(_jnp.mean(_jnp.square(w)) + 1e-30)
        else:
            s = MOMENT_SCALE[name]
        km, kv = _jax.random.split(_jax.random.fold_in(key, i + 1))
        out[name] = w
        out["m_" + name] = s * _jax.random.normal(km, w.shape, _jnp.float32)
        out["v_" + name] = (s * s) * _jax.random.uniform(kv, w.shape, _jnp.float32, 0.5, 1.5)
    if N_MICROBATCH > 1:
        for name, axis in PER_EXAMPLE_BATCH_AXIS.items():
            out[name] = _to_microbatches(out[name], axis)
    return {'x': out['x'], 'mix_norm': out['mix_norm'], 'w_qkv': out['w_qkv'], 'w_o': out['w_o'], 'q_norm': out['q_norm'], 'k_norm': out['k_norm'], 'rel_bias': out['rel_bias'], 'ffn_norm': out['ffn_norm'], 'w_up': out['w_up'], 'w_down': out['w_down'], 'loss_target': out['loss_target'], 'm_mix_norm': out['m_mix_norm'], 'm_w_qkv': out['m_w_qkv'], 'm_w_o': out['m_w_o'], 'm_q_norm': out['m_q_norm'], 'm_k_norm': out['m_k_norm'], 'm_rel_bias': out['m_rel_bias'], 'm_ffn_norm': out['m_ffn_norm'], 'm_w_up': out['m_w_up'], 'm_w_down': out['m_w_down'], 'v_mix_norm': out['v_mix_norm'], 'v_w_qkv': out['v_w_qkv'], 'v_w_o': out['v_w_o'], 'v_q_norm': out['v_q_norm'], 'v_k_norm': out['v_k_norm'], 'v_rel_bias': out['v_rel_bias'], 'v_ffn_norm': out['v_ffn_norm'], 'v_w_up': out['v_w_up'], 'v_w_down': out['v_w_down']}


def _loss(weights, diff, rest, loss_target):
    with _jax.named_scope("forward"):
        args = {**rest, TWIN_DIFF_INPUT: diff, **{k: w.astype(_WEIGHT_DTYPES[k]) for k, w in weights.items()}}
        y = _forward(args)
    with _jax.named_scope("loss_head"):
        err = _jnp.square(y.astype(_jnp.float32) - loss_target)
        return 0.5 * _jnp.sum(_jnp.mean(err, axis=-1)) if err.ndim else 0.5 * err


def _adamw(w, g, m, v):
    m = ADAM_B1 * m + (1.0 - ADAM_B1) * g
    v = ADAM_B2 * v + (1.0 - ADAM_B2) * _jnp.square(g)
    m_hat = m / (1.0 - ADAM_B1 ** ADAM_STEP)
    v_hat = v / (1.0 - ADAM_B2 ** ADAM_STEP)
    delta = -ADAM_LR * (m_hat / (_jnp.sqrt(v_hat) + ADAM_EPS) + ADAM_WD * w)
    return delta, m, v


def reference(x, mix_norm, w_qkv, w_o, q_norm, k_norm, rel_bias, ffn_norm, w_up, w_down, loss_target, m_mix_norm, m_w_qkv, m_w_o, m_q_norm, m_k_norm, m_rel_bias, m_ffn_norm, m_w_up, m_w_down, v_mix_norm, v_w_qkv, v_w_o, v_q_norm, v_k_norm, v_rel_bias, v_ffn_norm, v_w_up, v_w_down):
    given = dict(x=x, mix_norm=mix_norm, w_qkv=w_qkv, w_o=w_o, q_norm=q_norm, k_norm=k_norm, rel_bias=rel_bias, ffn_norm=ffn_norm, w_up=w_up, w_down=w_down, loss_target=loss_target, m_mix_norm=m_mix_norm, m_w_qkv=m_w_qkv, m_w_o=m_w_o, m_q_norm=m_q_norm, m_k_norm=m_k_norm, m_rel_bias=m_rel_bias, m_ffn_norm=m_ffn_norm, m_w_up=m_w_up, m_w_down=m_w_down, v_mix_norm=v_mix_norm, v_w_qkv=v_w_qkv, v_w_o=v_w_o, v_q_norm=v_q_norm, v_k_norm=v_k_norm, v_rel_bias=v_rel_bias, v_ffn_norm=v_ffn_norm, v_w_up=v_w_up, v_w_down=v_w_down)
    weights = {n: given[n] for n in TWIN_WEIGHTS}
    shared = {n: given[n] for n in SHARED_INPUTS}
    per_example = {n: given[n] for n in ['x']}
    grad_fn = _jax.value_and_grad(_loss, argnums=(0, 1))

    def one_microbatch(ex, loss_target):
        ex = dict(ex)
        diff = ex.pop(TWIN_DIFF_INPUT)
        return grad_fn(weights, diff, {**shared, **ex}, loss_target)

    if N_MICROBATCH == 1:
        loss, (grad_w, grad_x) = one_microbatch(per_example, given["loss_target"])
    else:
        def body(carry, xs):
            loss_sum, grad_sum = carry
            l_k, (gw_k, gx_k) = one_microbatch(xs[0], xs[1])
            with _jax.named_scope("update"):
                return (loss_sum + l_k, _jax.tree.map(_jnp.add, grad_sum, gw_k)), gx_k

        init = (_jnp.zeros((), _jnp.float32), _jax.tree.map(_jnp.zeros_like, weights))
        (loss, grad_w), grad_x = _jax.lax.scan(body, init, (per_example, given["loss_target"]))
    with _jax.named_scope("update"):
        delta_w, new_m, new_v = {}, {}, {}
        for n in TWIN_WEIGHTS:
            delta_w[n], new_m[n], new_v[n] = _adamw(weights[n], grad_w[n], given["m_" + n], given["v_" + n])
    return (loss, grad_x, *[grad_w[n] for n in TWIN_WEIGHTS], *[delta_w[n] for n in TWIN_WEIGHTS],
            *[new_m[n] for n in TWIN_WEIGHTS], *[new_v[n] for n in TWIN_WEIGHTS])
```

```python
import functools

import numpy as np
import jax
import jax.numpy as jnp
from jax import lax
from jax.experimental import pallas as pl
from jax.experimental.pallas import tpu as pltpu

F32, BF16 = jnp.float32, jnp.bfloat16
MESH = pl.DeviceIdType.MESH

HEAD_DIM = 64
PAIR = 128
CHUNK = 64
LEFT_CHUNKS = 8
MAX_REL = 256
N_REL = 2 * MAX_REL + 1
CA_TQ = 128
CA_PAD = LEFT_CHUNKS * CHUNK
CA_WIN = CA_PAD + CA_TQ
CA_EXT = CA_WIN + CA_TQ
SB_T = 128
RMS_EPS = 1e-6
SB_EXIT = -104.0
NEG_BIG = -1e30
ADAM_LR, ADAM_B1, ADAM_B2, ADAM_EPS, ADAM_WD, ADAM_STEP = 0.001, 0.9, 0.999, 1e-08, 0.01, 10
VMEM_LIMIT = 56 << 20
N_CHIPS = 4
N_DEV = 8
ANY = pl.BlockSpec(memory_space=pl.ANY)


def _cp(sem=None, **kw):
    return pltpu.CompilerParams(dimension_semantics=sem, vmem_limit_bytes=VMEM_LIMIT, **kw)


def _tile(n, want):
    t = min(n, want)
    assert n % t == 0, (n, t)
    return t


def _split_bf16(v):
    hi = v.astype(BF16)
    lo = (v - hi.astype(F32)).astype(BF16)
    return hi, lo


def _dot(a, b):
    return jnp.dot(a, b, preferred_element_type=F32)


def _dot_nt(a, b):
    return lax.dot_general(a, b, (((1,), (1,)), ((), ())), preferred_element_type=F32)


def _dot_tn(a, b):
    return lax.dot_general(a, b, (((0,), (0,)), ((), ())), preferred_element_type=F32)


def _matmul(name, a, b, a_spec, b_spec, grid, nt, acc_shape, extra, extra_specs, out_shape, out_specs, epilogue,
            sem=("parallel", "parallel", "arbitrary"), aliases=None):
    nk = grid[2]
    n_extra, n_out = len(extra), len(out_shape)

    def body(a_ref, b_ref, *rest):
        ins, outs = rest[:n_extra], rest[n_extra:n_extra + n_out]
        av = a_ref[...].astype(BF16)
        bv = b_ref[...].astype(BF16)
        part = _dot_nt(av, bv) if nt else _dot(av, bv)
        if nk == 1:
            epilogue(part, ins, outs)
        else:
            acc_ref = rest[-1]
            k = pl.program_id(2)

            @pl.when(k == 0)
            def _():
                acc_ref[...] = part

            @pl.when(k > 0)
            def _():
                acc_ref[...] += part

            @pl.when(k == nk - 1)
            def _():
                epilogue(acc_ref[...], ins, outs)

    return pl.pallas_call(
        body, name=name, grid=grid, in_specs=[a_spec, b_spec, *extra_specs], out_specs=out_specs, out_shape=out_shape,
        scratch_shapes=[] if nk == 1 else [pltpu.VMEM(acc_shape, F32)],
        input_output_aliases=aliases or {}, compiler_params=_cp(sem),
    )(a, b, *extra)


def _rms_rows(x, gain):
    r = lax.rsqrt(jnp.mean(x * x, axis=-1, keepdims=True) + RMS_EPS)
    return x * r * gain


def _rms_bwd_rows(dh, x, gain, dres):
    r = lax.rsqrt(jnp.mean(x * x, axis=-1, keepdims=True) + RMS_EPS)
    xhat = x * r
    dxn = dh * gain
    dx = r * (dxn - xhat * jnp.mean(dxn * xhat, axis=-1, keepdims=True))
    return dx + dres, jnp.sum(dh * xhat, axis=0, keepdims=True)


def _rms_first(x, gain):
    t, d = x.shape
    tm = _tile(t, 512)

    def body(x_ref, g_ref, h_ref):
        h_ref[...] = _rms_rows(x_ref[...], g_ref[...]).astype(BF16)

    return pl.pallas_call(
        body, name="rms_first", grid=(t // tm,),
        in_specs=[pl.BlockSpec((tm, d), lambda m: (m, 0)), pl.BlockSpec((1, d), lambda m: (0, 0))],
        out_specs=pl.BlockSpec((tm, d), lambda m: (m, 0)), out_shape=jax.ShapeDtypeStruct((t, d), BF16),
        compiler_params=_cp(("parallel",)),
    )(x, gain)


def _qkv_proj(h, w, layer, out_dtype):
    t, d = h.shape
    tm, tn = _tile(t, 512), _tile(d, 1024)
    npp = d // tn

    def epi(acc, ins, outs):
        outs[0][...] = acc.astype(out_dtype)

    return _matmul(
        "qkv_proj", h, w, pl.BlockSpec((tm, d), lambda m, n, k: (m, 0)), pl.BlockSpec((None, d, tn), lambda m, n, k: (layer, 0, n)),
        (t // tm, 3 * npp, 1), False, None, (), (), (jax.ShapeDtypeStruct((3, t, d), out_dtype),),
        (pl.BlockSpec((None, tm, tn), lambda m, n, k: (n // npp, m, n % npp)),), epi)[0]


def _out_proj(name, a, w, layer, res, gain, tk):
    t, kk = a.shape
    d = res.shape[1]
    tm = _tile(t, 512)
    tk = _tile(kk, tk)

    def epi(acc, ins, outs):
        xn = ins[0][...] + acc
        outs[0][...] = xn
        outs[1][...] = _rms_rows(xn, ins[1][...]).astype(BF16)

    row = pl.BlockSpec((tm, d), lambda m, n, k: (m, 0))
    return _matmul(
        name, a, w, pl.BlockSpec((tm, tk), lambda m, n, k: (m, k)), pl.BlockSpec((None, tk, d), lambda m, n, k: (layer, k, 0)),
        (t // tm, 1, kk // tk), False, (tm, d), (res, gain), (row, pl.BlockSpec((1, d), lambda m, n, k: (0, 0))),
        (jax.ShapeDtypeStruct((t, d), F32), jax.ShapeDtypeStruct((t, d), BF16)), (row, row), epi)


def _up_proj(h, w, layer):
    t, d = h.shape
    ff = w.shape[2]
    tm, tn = _tile(t, 512), _tile(ff, 1024)

    def epi(acc, ins, outs):
        s = jnp.maximum(acc, 0.0)
        outs[0][...] = s.astype(BF16)
        outs[1][...] = (s * s).astype(BF16)

    o = pl.BlockSpec((tm, tn), lambda m, n, k: (m, n))
    return _matmul(
        "up_proj", h, w, pl.BlockSpec((tm, d), lambda m, n, k: (m, 0)), pl.BlockSpec((None, d, tn), lambda m, n, k: (layer, 0, n)),
        (t // tm, ff // tn, 1), False, None, (), (), (jax.ShapeDtypeStruct((t, ff), BF16),) * 2, (o, o), epi)


def _down_bwd(dx, w, layer, s):
    t, d = dx.shape
    ff = w.shape[1]
    tm, tn = _tile(t, 512), _tile(ff, 1024)

    def epi(acc, ins, outs):
        outs[0][...] = (acc * (2.0 * ins[0][...].astype(F32))).astype(BF16)

    o = pl.BlockSpec((tm, tn), lambda m, n, k: (m, n))
    return _matmul(
        "down_bwd", dx, w, pl.BlockSpec((tm, d), lambda m, n, k: (m, 0)), pl.BlockSpec((None, tn, d), lambda m, n, k: (layer, n, 0)),
        (t // tm, ff // tn, 1), True, None, (s,), (o,), (jax.ShapeDtypeStruct((t, ff), BF16),), (o,), epi)[0]


def _norm_bwd_proj(name, a, a_spec, nk, w, w_spec, x, gain, dres):
    t, d = x.shape
    tm = _tile(t, 512)

    def epi(acc, ins, outs):
        dx, dg = _rms_bwd_rows(acc, ins[0][...], ins[1][...], ins[2][...])
        outs[0][...] = dx

        @pl.when(pl.program_id(0) == 0)
        def _():
            outs[1][...] = dg

        @pl.when(pl.program_id(0) > 0)
        def _():
            outs[1][...] += dg

    row = pl.BlockSpec((tm, d), lambda m, n, k: (m, 0))
    vec = pl.BlockSpec((1, d), lambda m, n, k: (0, 0))
    return _matmul(
        name, a, w, a_spec, w_spec, (t // tm, 1, nk), True, (tm, d), (x, gain, dres), (row, vec, row),
        (jax.ShapeDtypeStruct((t, d), F32), jax.ShapeDtypeStruct((1, d), F32)), (row, vec), epi,
        sem=("arbitrary", "arbitrary", "arbitrary"))


def _plain_nt(name, a, w, layer, out_dtype):
    t, n = a.shape
    m_out = w.shape[1]
    tm, tn = _tile(t, 512), _tile(m_out, 1024)

    def epi(acc, ins, outs):
        outs[0][...] = acc.astype(out_dtype)

    return _matmul(
        name, a, w, pl.BlockSpec((tm, n), lambda m, j, k: (m, 0)), pl.BlockSpec((None, tn, n), lambda m, j, k: (layer, j, 0)),
        (t // tm, m_out // tn, 1), True, None, (), (), (jax.ShapeDtypeStruct((t, m_out), out_dtype),),
        (pl.BlockSpec((tm, tn), lambda m, j, k: (m, j)),), epi)[0]


def _wgrad(name, a, a_spec_fn, b, b_spec_fn, t, mo, no, layer, n_layers, prev, tno=1024):
    tmo, tno, tk = _tile(mo, 1024), _tile(no, tno), _tile(t, 512)
    nk = t // tk

    def body(a_ref, b_ref, *rest):
        o_ref = rest[-1]
        part = _dot_tn(a_ref[...].astype(BF16), b_ref[...].astype(BF16))

        @pl.when(pl.program_id(2) == 0)
        def _():
            o_ref[...] = part

        @pl.when(pl.program_id(2) > 0)
        def _():
            o_ref[...] += part

    ins = [a, b] + ([] if prev is None else [prev])
    in_specs = [a_spec_fn(tk, tmo), b_spec_fn(tk, tno)] + ([] if prev is None else [ANY])
    return pl.pallas_call(
        body, name=name, grid=(mo // tmo, no // tno, nk), in_specs=in_specs,
        out_specs=pl.BlockSpec((None, tmo, tno), lambda m, n, k: (layer, m, n)),
        out_shape=jax.ShapeDtypeStruct((n_layers, mo, no), F32),
        input_output_aliases={} if prev is None else {2: 0},
        compiler_params=_cp(("parallel", "parallel", "arbitrary")),
    )(*ins)


def _mat_spec(tk, tw):
    return pl.BlockSpec((tk, tw), lambda m, n, k: (k, m))


def _mat_spec_b(tk, tw):
    return pl.BlockSpec((tk, tw), lambda m, n, k: (k, n))


def _plane_spec_b(d):
    def fn(tk, tw):
        npp = d // tw
        return pl.BlockSpec((None, tk, tw), lambda m, n, k: (n // npp, k, n % npp))
    return fn


def _sb_masks(t):
    row = lax.broadcasted_iota(jnp.int32, (t, t), 0)
    col = lax.broadcasted_iota(jnp.int32, (t, t), 1)
    lane = lax.broadcasted_iota(jnp.int32, (1, PAIR), 1)
    return row, col, [(lane // HEAD_DIM) == h for h in (0, 1)]


def _sb_tile(qh, kb, valid, after, carry):
    z = _dot_nt(qh, kb)
    lb = -(jnp.maximum(z, 0.0) + jnp.log(1.0 + jnp.exp(-jnp.abs(z))))
    lb = jnp.where(valid, lb, 0.0)
    hi, lo = _split_bf16(lb)
    between = _dot(hi, after) + _dot(lo, after) + carry
    a = jnp.where(valid, jnp.exp(z + lb + between), 0.0)
    return lb, a


def _sb_fwd(qkv):
    _, t, d = qkv.shape
    hp, tt = d // PAIR, _tile(t, SB_T)

    def body(q_ref, k_ref, v_ref, o_ref, o32_ref):
        i = pl.program_id(1)
        row, col, head = _sb_masks(tt)
        after = (row > col).astype(BF16)
        q = q_ref[...]
        qh = [jnp.where(head[h], q, jnp.zeros_like(q)) * jnp.asarray(HEAD_DIM ** -0.5, BF16) for h in (0, 1)]

        def cond(st):
            j, c0, c1 = st[0], st[1], st[2]
            return jnp.logical_and(j >= 0, jnp.maximum(jnp.max(c0), jnp.max(c1)) > SB_EXIT)

        def step(st):
            j, carry, acc = st[0], [st[1], st[2]], st[3]
            rows = pl.ds(pl.multiple_of(j * tt, tt), tt)
            kb, vb = k_ref[rows, :], v_ref[rows, :]
            valid = jnp.logical_or(j < i, col < row)
            for h in (0, 1):
                lb, a = _sb_tile(qh[h], kb, valid, after, carry[h])
                acc = acc + _dot(a.astype(BF16), jnp.where(head[h], vb, jnp.zeros_like(vb)))
                carry[h] = carry[h] + jnp.sum(lb, axis=1, keepdims=True)
            return j - 1, carry[0], carry[1], acc

        zc = jnp.zeros((tt, 1), F32)
        o = lax.while_loop(cond, step, (i, zc, zc, jnp.zeros((tt, PAIR), F32)))[3]
        o_ref[...] = o.astype(BF16)
        o32_ref[...] = o

    qs = pl.BlockSpec((None, tt, PAIR), lambda p, i: (0, i, p))
    ks = pl.BlockSpec((None, t, PAIR), lambda p, i: (1, 0, p))
    vs = pl.BlockSpec((None, t, PAIR), lambda p, i: (2, 0, p))
    os_ = pl.BlockSpec((tt, PAIR), lambda p, i: (i, p))
    return pl.pallas_call(
        body, name="sb_fwd", grid=(hp, t // tt), in_specs=[qs, ks, vs], out_specs=(os_, os_),
        out_shape=(jax.ShapeDtypeStruct((t, d), BF16), jax.ShapeDtypeStruct((t, d), F32)),
        compiler_params=_cp(("parallel", "arbitrary")),
    )(qkv, qkv, qkv)


def _sb_bwd(qkv, o32, do):
    _, t, d = qkv.shape
    hp, tt = d // PAIR, _tile(t, SB_T)
    nq = t // tt
    scale = HEAD_DIM ** -0.5

    def body(q_ref, k_ref, v_ref, o32_ref, do_ref, dqkv_ref, dk_acc, dv_acc):
        i = pl.program_id(1)
        row, col, head = _sb_masks(tt)
        after = (row > col).astype(BF16)
        from_s = (row >= col).astype(BF16)

        @pl.when(i == 0)
        def _():
            dk_acc[...] = jnp.zeros_like(dk_acc)
            dv_acc[...] = jnp.zeros_like(dv_acc)

        q = q_ref[...]
        dob = do_ref[...]
        prod = dob.astype(F32) * o32_ref[...]
        zb = jnp.zeros_like(q)
        qh = [jnp.where(head[h], q, zb) * jnp.asarray(scale, BF16) for h in (0, 1)]
        doh = [jnp.where(head[h], dob, zb) for h in (0, 1)]
        tot = [jnp.sum(jnp.where(head[h], prod, 0.0), axis=1, keepdims=True) for h in (0, 1)]

        def cond(st):
            j, c0, c1 = st[0], st[1], st[2]
            return jnp.logical_and(j >= 0, jnp.maximum(jnp.max(c0), jnp.max(c1)) > SB_EXIT)

        def step(st):
            j, carry, seen, dq = st[0], [st[1], st[2]], [st[3], st[4]], st[5]
            rows = pl.ds(pl.multiple_of(j * tt, tt), tt)
            kb, vb = k_ref[rows, :], v_ref[rows, :]
            valid = jnp.logical_or(j < i, col < row)
            dk_t = jnp.zeros((tt, PAIR), F32)
            dv_t = jnp.zeros((tt, PAIR), F32)
            for h in (0, 1):
                kh = jnp.where(head[h], kb, jnp.zeros_like(kb))
                lb, a = _sb_tile(qh[h], kb, valid, after, carry[h])
                ab = a.astype(BF16)
                g = ab.astype(F32) * _dot_nt(doh[h], vb)
                ghi, glo = _split_bf16(g)
                g_from = _dot(ghi, from_s) + _dot(glo, from_s)
                before = tot[h] - seen[h] - g_from
                e = jnp.exp(lb)
                dz = jnp.where(valid, g * e - (1.0 - e) * before, 0.0).astype(BF16)
                dq = dq + _dot(dz, kh)
                dk_t = dk_t + _dot_tn(dz, qh[h])
                dv_t = dv_t + _dot_tn(ab, doh[h])
                carry[h] = carry[h] + jnp.sum(lb, axis=1, keepdims=True)
                seen[h] = seen[h] + jnp.sum(g, axis=1, keepdims=True)
            dk_acc[rows, :] += dk_t
            dv_acc[rows, :] += dv_t
            return j - 1, carry[0], carry[1], seen[0], seen[1], dq

        zc = jnp.zeros((tt, 1), F32)
        dq = lax.while_loop(cond, step, (i, zc, zc, zc, zc, jnp.zeros((tt, PAIR), F32)))[5]
        dqkv_ref[0, pl.ds(pl.multiple_of(i * tt, tt), tt), :] = (dq * scale).astype(BF16)

        @pl.when(i == nq - 1)
        def _():
            dqkv_ref[1, :, :] = dk_acc[...].astype(BF16)
            dqkv_ref[2, :, :] = dv_acc[...].astype(BF16)

    qs = pl.BlockSpec((None, tt, PAIR), lambda p, i: (0, i, p))
    ks = pl.BlockSpec((None, t, PAIR), lambda p, i: (1, 0, p))
    vs = pl.BlockSpec((None, t, PAIR), lambda p, i: (2, 0, p))
    ts = pl.BlockSpec((tt, PAIR), lambda p, i: (i, p))
    return pl.pallas_call(
        body, name="sb_bwd", grid=(hp, nq), in_specs=[qs, ks, vs, ts, ts],
        out_specs=pl.BlockSpec((3, t, PAIR), lambda p, i: (0, 0, p)), out_shape=jax.ShapeDtypeStruct((3, t, d), BF16),
        scratch_shapes=[pltpu.VMEM((t, PAIR), F32), pltpu.VMEM((t, PAIR), F32)],
        compiler_params=_cp(("parallel", "arbitrary")),
    )(qkv, qkv, qkv, o32, do)


def _pair_sum_matrix():
    r = lax.broadcasted_iota(jnp.int32, (PAIR, PAIR), 0) // HEAD_DIM
    c = lax.broadcasted_iota(jnp.int32, (PAIR, PAIR), 1) // HEAD_DIM
    return (r == c).astype(BF16)


def _head_mean(v, ones):
    hi, lo = _split_bf16(v)
    return (_dot(hi, ones) + _dot(lo, ones)) * (1.0 / HEAD_DIM)


def _ca_prep(qkv32, gq, gk):
    _, t, d = qkv32.shape
    tr = _tile(t, CA_PAD)
    assert CA_PAD % tr == 0
    npad = CA_PAD // tr

    def body(x_ref, gq_ref, gk_ref, o_ref):
        p, r = pl.program_id(0), pl.program_id(1)

        @pl.when(r < npad)
        def _():
            o_ref[...] = jnp.zeros_like(o_ref)

        @pl.when(jnp.logical_and(r >= npad, p == 2))
        def _():
            o_ref[...] = x_ref[...].astype(BF16)

        @pl.when(jnp.logical_and(r >= npad, p < 2))
        def _():
            ones = _pair_sum_matrix()
            g = jnp.where(p == 0, gq_ref[...], gk_ref[...])
            for c in range(d // PAIR):
                x = x_ref[:, c * PAIR:(c + 1) * PAIR]
                rs = lax.rsqrt(_head_mean(x * x, ones) + RMS_EPS)
                o_ref[:, c * PAIR:(c + 1) * PAIR] = (x * rs * g).astype(BF16)

    vec = pl.BlockSpec((1, PAIR), lambda p, r: (0, 0))
    return pl.pallas_call(
        body, name="ca_prep", grid=(3, npad + t // tr),
        in_specs=[pl.BlockSpec((None, tr, d), lambda p, r: (p, jnp.maximum(r - npad, 0), 0)), vec, vec],
        out_specs=pl.BlockSpec((None, tr, d), lambda p, r: (p, r, 0)),
        out_shape=jax.ShapeDtypeStruct((3, CA_PAD + t, d), BF16), compiler_params=_cp(("parallel", "parallel")),
    )(qkv32, gq, gk)


def _ca_unprep(dq, dkv, qkv32, gq, gk):
    _, t, d = qkv32.shape
    tr = _tile(t, 512)
    nr = t // tr

    def body(dq_ref, dkv_ref, x_ref, gq_ref, gk_ref, o_ref, dgq_ref, dgk_ref):
        p, r = pl.program_id(0), pl.program_id(1)

        @pl.when(jnp.logical_and(p == 0, r == 0))
        def _():
            dgq_ref[...] = jnp.zeros_like(dgq_ref)
            dgk_ref[...] = jnp.zeros_like(dgk_ref)

        @pl.when(p == 2)
        def _():
            o_ref[...] = dkv_ref[...].astype(BF16)

        @pl.when(p < 2)
        def _():
            ones = _pair_sum_matrix()
            g = jnp.where(p == 0, gq_ref[...], gk_ref[...])
            dg = jnp.zeros((1, PAIR), F32)
            for c in range(d // PAIR):
                cols = slice(c * PAIR, (c + 1) * PAIR)
                x = x_ref[:, cols]
                dy = jnp.where(p == 0, dq_ref[:, cols], dkv_ref[:, cols])
                rs = lax.rsqrt(_head_mean(x * x, ones) + RMS_EPS)
                xhat = x * rs
                dxn = dy * g
                o_ref[:, cols] = (rs * (dxn - xhat * _head_mean(dxn * xhat, ones))).astype(BF16)
                dg = dg + jnp.sum(dy * xhat, axis=0, keepdims=True)
            dg = dg + pltpu.roll(dg, HEAD_DIM, 1)

            @pl.when(p == 0)
            def _():
                dgq_ref[...] += dg

            @pl.when(p == 1)
            def _():
                dgk_ref[...] += dg

    vec = pl.BlockSpec((1, PAIR), lambda p, r: (0, 0))
    return pl.pallas_call(
        body, name="ca_unprep", grid=(3, nr),
        in_specs=[pl.BlockSpec((tr, d), lambda p, r: (r, 0)),
                  pl.BlockSpec((None, tr, d), lambda p, r: (jnp.maximum(p - 1, 0), r, 0)),
                  pl.BlockSpec((None, tr, d), lambda p, r: (p, r, 0)), vec, vec],
        out_specs=(pl.BlockSpec((None, tr, d), lambda p, r: (p, r, 0)), vec, vec),
        out_shape=(jax.ShapeDtypeStruct((3, t, d), BF16), jax.ShapeDtypeStruct((1, PAIR), F32), jax.ShapeDtypeStruct((1, PAIR), F32)),
        compiler_params=_cp(("arbitrary", "arbitrary")),
    )(dq, dkv, qkv32, gq, gk)


def _ca_ext_index():
    m = np.arange(CA_EXT)
    return np.where(m <= CA_WIN, np.clip(CA_PAD - m, -MAX_REL, MAX_REL) + MAX_REL, 2 * MAX_REL).astype(np.int32)


def _ca_valid(i):
    a = lax.broadcasted_iota(jnp.int32, (CA_TQ, CA_WIN), 0)
    b = lax.broadcasted_iota(jnp.int32, (CA_TQ, CA_WIN), 1)
    ca, cb = a // CHUNK, b // CHUNK
    return jnp.logical_and(jnp.logical_and(cb >= ca, cb <= ca + LEFT_CHUNKS), b >= CA_PAD - CA_TQ * i)


def _ca_bias_tiles(ext_ref, bias_ref):
    for h in (0, 1):
        e = ext_ref[pl.ds(h, 1), :]

        def put(a, carry, h=h, e=e):
            bias_ref[h, pl.ds(a, 1), :] = pltpu.roll(e, a, 1)
            return carry

        lax.fori_loop(0, CA_TQ, put, 0)


def _ca_probs(qh, kw, bias, valid):
    z = jnp.where(valid, _dot_nt(qh, kw) + bias, NEG_BIG)
    p = jnp.exp(z - jnp.max(z, axis=1, keepdims=True))
    return p / jnp.sum(p, axis=1, keepdims=True)


def _ca_fwd(qkvn, ext):
    _, tp, d = qkvn.shape
    t = tp - CA_PAD
    hp, npad = d // PAIR, CA_PAD // CA_TQ

    def body(q_ref, k_ref, v_ref, ext_ref, o_ref, bias_ref):
        i = pl.program_id(1)

        @pl.when(i == 0)
        def _():
            _ca_bias_tiles(ext_ref, bias_ref)

        lane = lax.broadcasted_iota(jnp.int32, (1, PAIR), 1)
        valid = _ca_valid(i)
        win = pl.ds(pl.multiple_of(i * CA_TQ, CA_TQ), CA_WIN)
        kw, vw, q = k_ref[win, :], v_ref[win, :], q_ref[...]
        o = jnp.zeros((CA_TQ, PAIR), F32)
        for h in (0, 1):
            hm = (lane // HEAD_DIM) == h
            qh = jnp.where(hm, q, jnp.zeros_like(q)) * jnp.asarray(HEAD_DIM ** -0.5, BF16)
            p = _ca_probs(qh, kw, bias_ref[h, :, :CA_WIN], valid)
            o = o + _dot(p.astype(BF16), jnp.where(hm, vw, jnp.zeros_like(vw)))
        o_ref[...] = o.astype(BF16)

    return pl.pallas_call(
        body, name="ca_fwd", grid=(hp, t // CA_TQ),
        in_specs=[pl.BlockSpec((None, CA_TQ, PAIR), lambda p, i: (0, i + npad, p)),
                  pl.BlockSpec((None, tp, PAIR), lambda p, i: (1, 0, p)),
                  pl.BlockSpec((None, tp, PAIR), lambda p, i: (2, 0, p)),
                  pl.BlockSpec((None, 2, CA_EXT), lambda p, i: (p, 0, 0))],
        out_specs=pl.BlockSpec((CA_TQ, PAIR), lambda p, i: (i, p)), out_shape=jax.ShapeDtypeStruct((t, d), BF16),
        scratch_shapes=[pltpu.VMEM((2, CA_TQ, CA_EXT), F32)], compiler_params=_cp(("parallel", "arbitrary")),
    )(qkvn, qkvn, qkvn, ext)


def _ca_bwd(qkvn, ext, do):
    _, tp, d = qkvn.shape
    t = tp - CA_PAD
    hp, npad, nq = d // PAIR, CA_PAD // CA_TQ, t // CA_TQ
    scale = HEAD_DIM ** -0.5

    def body(q_ref, k_ref, v_ref, ext_ref, do_ref, dq_ref, dkv_ref, dext_ref, bias_ref, dbias_ref, dk_acc, dv_acc):
        i = pl.program_id(1)

        @pl.when(i == 0)
        def _():
            _ca_bias_tiles(ext_ref, bias_ref)
            dbias_ref[...] = jnp.zeros_like(dbias_ref)
            dk_acc[...] = jnp.zeros_like(dk_acc)
            dv_acc[...] = jnp.zeros_like(dv_acc)

        lane = lax.broadcasted_iota(jnp.int32, (1, PAIR), 1)
        valid = _ca_valid(i)
        win = pl.ds(pl.multiple_of(i * CA_TQ, CA_TQ), CA_WIN)
        kw, vw, q, dob = k_ref[win, :], v_ref[win, :], q_ref[...], do_ref[...]
        dq = jnp.zeros((CA_TQ, PAIR), F32)
        dk_t = jnp.zeros((CA_WIN, PAIR), F32)
        dv_t = jnp.zeros((CA_WIN, PAIR), F32)
        for h in (0, 1):
            hm = (lane // HEAD_DIM) == h
            qh = jnp.where(hm, q, jnp.zeros_like(q)) * jnp.asarray(scale, BF16)
            doh = jnp.where(hm, dob, jnp.zeros_like(dob))
            p = _ca_probs(qh, kw, bias_ref[h, :, :CA_WIN], valid)
            dp = _dot_nt(doh, vw)
            ds = p * (dp - jnp.sum(p * dp, axis=1, keepdims=True))
            dbias_ref[h, :, :CA_WIN] += ds
            dsb = ds.astype(BF16)
            dq = dq + _dot(dsb, jnp.where(hm, kw, jnp.zeros_like(kw)))
            dk_t = dk_t + _dot_tn(dsb, qh)
            dv_t = dv_t + _dot_tn(p.astype(BF16), doh)
        dq_ref[...] = dq * scale
        dk_acc[win, :] += dk_t
        dv_acc[win, :] += dv_t

        @pl.when(i == nq - 1)
        def _():
            dkv_ref[0, :, :] = dk_acc[CA_PAD:, :]
            dkv_ref[1, :, :] = dv_acc[CA_PAD:, :]
            for h in (0, 1):
                def take(a, acc, h=h):
                    return acc + pltpu.roll(dbias_ref[h, pl.ds(a, 1), :], lax.rem(CA_EXT - a, CA_EXT), 1)

                dext_ref[pl.ds(h, 1), :] = lax.fori_loop(0, CA_TQ, take, jnp.zeros((1, CA_EXT), F32))

    es = pl.BlockSpec((None, 2, CA_EXT), lambda p, i: (p, 0, 0))
    ts = pl.BlockSpec((CA_TQ, PAIR), lambda p, i: (i, p))
    return pl.pallas_call(
        body, name="ca_bwd", grid=(hp, nq),
        in_specs=[pl.BlockSpec((None, CA_TQ, PAIR), lambda p, i: (0, i + npad, p)),
                  pl.BlockSpec((None, tp, PAIR), lambda p, i: (1, 0, p)),
                  pl.BlockSpec((None, tp, PAIR), lambda p, i: (2, 0, p)), es, ts],
        out_specs=(ts, pl.BlockSpec((2, t, PAIR), lambda p, i: (0, 0, p)), es),
        out_shape=(jax.ShapeDtypeStruct((t, d), F32), jax.ShapeDtypeStruct((2, t, d), F32), jax.ShapeDtypeStruct((hp, 2, CA_EXT), F32)),
        scratch_shapes=[pltpu.VMEM((2, CA_TQ, CA_EXT), F32), pltpu.VMEM((2, CA_TQ, CA_EXT), F32),
                        pltpu.VMEM((tp, PAIR), F32), pltpu.VMEM((tp, PAIR), F32)],
        compiler_params=_cp(("parallel", "arbitrary")),
    )(qkvn, qkvn, qkvn, ext, do)


def _loss_head(y, target):
    t, d = y.shape
    tm = _tile(t, 512)

    def body(y_ref, t_ref, dy_ref, loss_ref):
        diff = y_ref[...] - t_ref[...]
        dy_ref[...] = diff * (1.0 / d)
        part = 0.5 * jnp.sum(jnp.mean(diff * diff, axis=-1, keepdims=True), axis=0, keepdims=True)

        @pl.when(pl.program_id(0) == 0)
        def _():
            loss_ref[...] = jnp.zeros_like(loss_ref)

        loss_ref[...] += jnp.broadcast_to(part, loss_ref.shape)

    row = pl.BlockSpec((tm, d), lambda m: (m, 0))
    return pl.pallas_call(
        body, name="loss_head", grid=(t // tm,), in_specs=[row, row],
        out_specs=(row, pl.BlockSpec((8, 128), lambda m: (0, 0))),
        out_shape=(jax.ShapeDtypeStruct((t, d), F32), jax.ShapeDtypeStruct((8, 128), F32)),
        compiler_params=_cp(("arbitrary",)),
    )(y, target)


def _adamw_math(w, g, m, v):
    m = ADAM_B1 * m + (1.0 - ADAM_B1) * g
    v = ADAM_B2 * v + (1.0 - ADAM_B2) * (g * g)
    m_hat = m / (1.0 - ADAM_B1 ** ADAM_STEP)
    v_hat = v / (1.0 - ADAM_B2 ** ADAM_STEP)
    delta = -ADAM_LR * (m_hat / (jnp.sqrt(v_hat) + ADAM_EPS) + ADAM_WD * w)
    return delta, m, v


def _adamw(name, w, g, m, v):
    shape = w.shape
    cols = shape[-1]
    rows = int(np.prod(shape[:-1]))
    tr = _tile(rows, 512)
    flat = [a.reshape(rows, cols) for a in (w, g, m, v)]

    def body(w_ref, g_ref, m_ref, v_ref, d_ref, nm_ref, nv_ref):
        d_ref[...], nm_ref[...], nv_ref[...] = _adamw_math(w_ref[...], g_ref[...], m_ref[...], v_ref[...])

    blk = pl.BlockSpec((tr, cols), lambda r: (r, 0))
    outs = pl.pallas_call(
        body, name=name, grid=(rows // tr,), in_specs=[blk] * 4, out_specs=(blk,) * 3,
        out_shape=(jax.ShapeDtypeStruct((rows, cols), F32),) * 3, compiler_params=_cp(("parallel",)),
    )(*flat)
    return [o.reshape(shape) for o in outs]


def _place():
    x, y, c = lax.axis_index("x"), lax.axis_index("y"), lax.axis_index("c")
    chips = [(1 - x, y), (x, 1 - y), (1 - x, 1 - y)]
    return x, y, c, chips


def _shard_slab(ref, kind, layer0, n_layers, shard, width):
    lay = pl.ds(layer0, n_layers)
    if kind == "cols":
        return ref.at[lay, :, pl.ds(shard * width, width)]
    return ref.at[lay, pl.ds(shard * width, width), :]


def _gather_weights(shards, kinds):
    n_layers = shards[0].shape[0]
    lh = n_layers // 2
    na = len(shards)
    widths = [s.shape[2] if k == "cols" else s.shape[1] for s, k in zip(shards, kinds)]
    full_shapes = []
    for s, k in zip(shards, kinds):
        shp = list(s.shape)
        shp[2 if k == "cols" else 1] *= N_CHIPS
        full_shapes.append(jax.ShapeDtypeStruct(tuple(shp), s.dtype))

    def body(*refs):
        src, dst = refs[:na], refs[na:2 * na]
        send_sems, recv_sems, local_sems = refs[2 * na:]
        x, y, c, chips = _place()
        mine = 2 * x + y
        sibling = (x, y, 1 - c)

        def slab(a, half, shard):
            return _shard_slab(dst[a], kinds[a], half * lh, lh, shard, widths[a])

        def remote(k, a, half, shard, to, source=None):
            return pltpu.make_async_remote_copy(
                src_ref=slab(a, half, shard) if source is None else source, dst_ref=slab(a, half, shard),
                send_sem=send_sems.at[k], recv_sem=recv_sems.at[k], device_id=to, device_id_type=MESH)

        local = [pltpu.make_async_copy(src[a], _shard_slab(dst[a], kinds[a], 0, n_layers, mine, widths[a]), local_sems.at[a])
                 for a in range(na)]
        for cp in local:
            cp.start()
        sends = []
        for j, chip in enumerate(chips):
            for a in range(na):
                sends.append(remote(j * na + a, a, c, mine, (*chip, c), source=src[a].at[pl.ds(c * lh, lh)]))
        for cp in sends:
            cp.start()
        passed = []
        for j, (cx, cy) in enumerate(chips):
            for a in range(na):
                k = j * na + a
                remote(k, a, c, 2 * cx + cy, (x, y, c)).wait_recv()
                fwd = remote(3 * na + k, a, c, 2 * cx + cy, sibling)
                fwd.start()
                passed.append(fwd)
        for j, (cx, cy) in enumerate(chips):
            for a in range(na):
                remote(3 * na + j * na + a, a, 1 - c, 2 * cx + cy, (x, y, c)).wait_recv()
        for cp in sends + passed:
            cp.wait_send()
        for cp in local:
            cp.wait()

    n_sem = 6 * na
    return pl.pallas_call(
        body, name="gather_weights", in_specs=[ANY] * na, out_specs=[ANY] * na, out_shape=full_shapes,
        scratch_shapes=[pltpu.SemaphoreType.DMA((n_sem,)), pltpu.SemaphoreType.DMA((n_sem,)), pltpu.SemaphoreType.DMA((na,))],
    )(*shards)


def _pair_exchange(grads):
    n_layers = grads[0].shape[0]
    lh = n_layers // 2
    na = len(grads)

    def body(*refs):
        src, dst = refs[:na], refs[na:2 * na]
        send_sems, recv_sems = refs[2 * na:]
        x, y, c, _ = _place()
        cps = [pltpu.make_async_remote_copy(src_ref=src[a].at[pl.ds((1 - c) * lh, lh)], dst_ref=dst[a], send_sem=send_sems.at[a],
                                            recv_sem=recv_sems.at[a], device_id=(x, y, 1 - c), device_id_type=MESH) for a in range(na)]
        for cp in cps:
            cp.start()
        for cp in cps:
            cp.wait()

    return pl.pallas_call(
        body, name="pair_exchange", in_specs=[ANY] * na, out_specs=[ANY] * na,
        out_shape=[jax.ShapeDtypeStruct((lh,) + g.shape[1:], g.dtype) for g in grads],
        scratch_shapes=[pltpu.SemaphoreType.DMA((na,)), pltpu.SemaphoreType.DMA((na,))],
    )(*grads)


def _add_half(name, full, other, c):
    n_layers, rows, cols = full.shape
    lh = n_layers // 2
    tr = _tile(rows, 512)
    nr = rows // tr

    def body(c_ref, a_ref, b_ref, o_ref):
        o_ref[...] = a_ref[...] + b_ref[...]

    grid_spec = pltpu.PrefetchScalarGridSpec(
        num_scalar_prefetch=1, grid=(lh, nr),
        in_specs=[pl.BlockSpec((None, tr, cols), lambda l, r, c_ref: (c_ref[0] * lh + l, r, 0)),
                  pl.BlockSpec((None, tr, cols), lambda l, r, c_ref: (l, r, 0))],
        out_specs=pl.BlockSpec((None, tr, cols), lambda l, r, c_ref: (l, r, 0)))
    return pl.pallas_call(body, name=name, grid_spec=grid_spec, out_shape=jax.ShapeDtypeStruct((lh, rows, cols), F32),
                          compiler_params=_cp(("parallel", "parallel")))(c, full, other)


def _chip_exchange(partials, kinds, widths):
    lh = partials[0].shape[0]
    na = len(partials)
    out_shapes = []
    for p, k, w in zip(partials, kinds, widths):
        shp = (lh, p.shape[1], w) if k == "cols" else (lh, w, p.shape[2])
        out_shapes.append(jax.ShapeDtypeStruct((3,) + shp, p.dtype))

    def body(*refs):
        src, dst = refs[:na], refs[na:2 * na]
        send_sems, recv_sems = refs[2 * na:]
        x, y, c, chips = _place()
        cps = []
        for j, (cx, cy) in enumerate(chips):
            for a in range(na):
                k = j * na + a
                cps.append(pltpu.make_async_remote_copy(
                    src_ref=_shard_slab(src[a], kinds[a], 0, lh, 2 * cx + cy, widths[a]), dst_ref=dst[a].at[j],
                    send_sem=send_sems.at[k], recv_sem=recv_sems.at[k], device_id=(cx, cy, c), device_id_type=MESH))
        for cp in cps:
            cp.start()
        for cp in cps:
            cp.wait()

    return pl.pallas_call(
        body, name="chip_exchange", in_specs=[ANY] * na, out_specs=[ANY] * na, out_shape=out_shapes,
        scratch_shapes=[pltpu.SemaphoreType.DMA((3 * na,)), pltpu.SemaphoreType.DMA((3 * na,))],
    )(*partials)


def _add_chips(name, partial, got, kind, width, shard):
    lh = partial.shape[0]
    _, _, rows, cols = got.shape
    tr = _tile(rows, 512)
    nr = rows // tr

    def body(s_ref, a_ref, b_ref, o_ref):
        o_ref[...] = ((a_ref[...] + b_ref[0]) + b_ref[1]) + b_ref[2]

    if kind == "cols":
        own = pl.BlockSpec((None, tr, width), lambda l, r, s_ref: (l, r, s_ref[0]))
    else:
        own = pl.BlockSpec((None, tr, cols), lambda l, r, s_ref: (l, s_ref[0] * nr + r, 0))
    grid_spec = pltpu.PrefetchScalarGridSpec(
        num_scalar_prefetch=1, grid=(lh, nr),
        in_specs=[own, pl.BlockSpec((3, None, tr, cols), lambda l, r, s_ref: (0, l, r, 0))],
        out_specs=pl.BlockSpec((None, tr, cols), lambda l, r, s_ref: (l, r, 0)))
    return pl.pallas_call(body, name=name, grid_spec=grid_spec, out_shape=jax.ShapeDtypeStruct((lh, rows, cols), F32),
                          compiler_params=_cp(("parallel", "parallel")))(shard, partial, got)


def _pair_share(halves):
    lh = halves[0].shape[0]
    na = len(halves)

    def body(*refs):
        src, dst = refs[:na], refs[na:2 * na]
        send_sems, recv_sems, local_sems = refs[2 * na:]
        x, y, c, _ = _place()
        local = [pltpu.make_async_copy(src[a], dst[a].at[pl.ds(c * lh, lh)], local_sems.at[a]) for a in range(na)]
        for cp in local:
            cp.start()
        cps = [pltpu.make_async_remote_copy(src_ref=src[a], dst_ref=dst[a].at[pl.ds(c * lh, lh)], send_sem=send_sems.at[a],
                                            recv_sem=recv_sems.at[a], device_id=(x, y, 1 - c), device_id_type=MESH) for a in range(na)]
        for cp in cps:
            cp.start()
        for a, cp in enumerate(cps):
            cp.wait_send()
            pltpu.make_async_remote_copy(src_ref=src[a], dst_ref=dst[a].at[pl.ds((1 - c) * lh, lh)], send_sem=send_sems.at[a],
                                         recv_sem=recv_sems.at[a], device_id=(x, y, 1 - c), device_id_type=MESH).wait_recv()
        for cp in local:
            cp.wait()

    return pl.pallas_call(
        body, name="pair_share", in_specs=[ANY] * na, out_specs=[ANY] * na,
        out_shape=[jax.ShapeDtypeStruct((2 * lh,) + h.shape[1:], h.dtype) for h in halves],
        scratch_shapes=[pltpu.SemaphoreType.DMA((na,)), pltpu.SemaphoreType.DMA((na,)), pltpu.SemaphoreType.DMA((na,))],
    )(*halves)


def _small_step(g_part, w, m, v):
    r = g_part.shape[0]

    def body(g_ref, w_ref, m_ref, v_ref, go_ref, d_ref, nm_ref, nv_ref, all_ref, send_sems, recv_sems):
        x, y, c, _ = _place()
        me = 4 * x + 2 * y + c
        all_ref[me] = g_ref[...]
        cps = []
        for k in range(1, N_DEV):
            px, py, pc = (x + (k >> 2)) % 2, (y + ((k >> 1) & 1)) % 2, (c + (k & 1)) % 2
            cps.append(pltpu.make_async_remote_copy(src_ref=g_ref, dst_ref=all_ref.at[me], send_sem=send_sems.at[k - 1],
                                                    recv_sem=recv_sems.at[k - 1], device_id=(px, py, pc), device_id_type=MESH))
        for cp in cps:
            cp.start()
        for cp in cps:
            cp.wait()
        g = all_ref[0]
        for k in range(1, N_DEV):
            g = g + all_ref[k]
        go_ref[...] = g
        d_ref[...], nm_ref[...], nv_ref[...] = _adamw_math(w_ref[...], g, m_ref[...], v_ref[...])

    vm = pl.BlockSpec(memory_space=pltpu.VMEM)
    return pl.pallas_call(
        body, name="small_step", in_specs=[vm] * 4, out_specs=[vm] * 4, out_shape=[jax.ShapeDtypeStruct((r, 128), F32)] * 4,
        scratch_shapes=[pltpu.VMEM((N_DEV, r, 128), F32), pltpu.SemaphoreType.DMA((N_DEV - 1,)), pltpu.SemaphoreType.DMA((N_DEV - 1,))],
    )(g_part, w, m, v)


def _onehot_mm(name, a, onehot):
    def body(a_ref, oh_ref, o_ref):
        v = a_ref[...]
        oh = oh_ref[...]
        hi, lo = _split_bf16(v)
        lo2 = (v - hi.astype(F32) - lo.astype(F32)).astype(BF16)
        o_ref[...] = _dot(hi, oh) + _dot(lo, oh) + _dot(lo2, oh)

    return pl.pallas_call(body, name=name, out_shape=jax.ShapeDtypeStruct((a.shape[0], onehot.shape[1]), F32))(a, onehot)


def _pack_small(parts):
    flat = jnp.concatenate([p.reshape(-1) for p in parts])
    n = flat.shape[0]
    rows = -(-n // 128)
    rows = -(-rows // 8) * 8
    return jnp.pad(flat, (0, rows * 128 - n)).reshape(rows, 128)


def _unpack_small(packed, like):
    flat = packed.reshape(-1)
    out, off = [], 0
    for p in like:
        out.append(flat[off:off + p.size].reshape(p.shape))
        off += p.size
    return out


def kernel(x, mix_norm, w_qkv, w_o, q_norm, k_norm, rel_bias, ffn_norm, w_up, w_down, loss_target, m_mix_norm, m_w_qkv, m_w_o, m_q_norm, m_k_norm, m_rel_bias, m_ffn_norm, m_w_up, m_w_down, v_mix_norm, v_w_qkv, v_w_o, v_q_norm, v_k_norm, v_rel_bias, v_ffn_norm, v_w_up, v_w_down):
    n_layers, d = mix_norm.shape
    t = x.shape[1]
    ff = w_down.shape[1] * N_CHIPS
    heads = d // HEAD_DIM
    cx, cy, cc = lax.axis_index("x"), lax.axis_index("y"), lax.axis_index("c")
    shard = (2 * cx + cy).astype(jnp.int32).reshape(1)
    core = cc.astype(jnp.int32).reshape(1)

    big = [w_qkv, w_o, w_up, w_down]
    kinds = ["cols", "rows", "cols", "rows"]
    widths = [w_qkv.shape[2], w_o.shape[1], w_up.shape[2], w_down.shape[1]]
    wq, wo, wu, wd = _gather_weights([w.astype(BF16) for w in big], kinds)

    rel_pad = -(-N_REL // PAIR) * PAIR
    ext_hot = _ca_ext_index()[:, None] == np.arange(rel_pad)[None, :]
    fold_hot, spread_hot = jnp.asarray(ext_hot, BF16), jnp.asarray(ext_hot.T, BF16)
    rel_tab = jnp.pad(rel_bias, ((0, 0), (0, 0), (0, rel_pad - N_REL)))
    exts = [_onehot_mm("relbias_spread", rel_tab[i], spread_hot).reshape(heads // 2, 2, CA_EXT) for i in range(n_layers // 2)]

    xs, h1s, qkvs, os_, o32s, x2s, h2s, ss, us, q32s = [], [], [], [], [], [], [], [], [], []
    xc = x[0]
    h = _rms_first(xc, mix_norm[0:1])
    for layer in range(n_layers):
        xs.append(xc)
        h1s.append(h)
        if layer % 2 == 0:
            qkv = _qkv_proj(h, wq, layer, BF16)
            o, o32 = _sb_fwd(qkv)
            q32s.append(None)
        else:
            idx = layer // 2
            q32 = _qkv_proj(h, wq, layer, F32)
            gq = jnp.tile(q_norm[idx], 2).reshape(1, PAIR)
            gk = jnp.tile(k_norm[idx], 2).reshape(1, PAIR)
            qkv = _ca_prep(q32, gq, gk)
            o, o32 = _ca_fwd(qkv, exts[idx]), None
            q32s.append(q32)
        qkvs.append(qkv)
        os_.append(o)
        o32s.append(o32)
        x2, h2 = _out_proj("attn_out", o, wo, layer, xc, ffn_norm[layer:layer + 1], 1024)
        s, u = _up_proj(h2, wu, layer)
        nxt = mix_norm[layer + 1:layer + 2] if layer + 1 < n_layers else mix_norm[0:1]
        xc, h = _out_proj("mlp_out", u, wd, layer, x2, nxt, 1024)
        x2s.append(x2)
        h2s.append(h2)
        ss.append(s)
        us.append(u)

    dx, loss_part = _loss_head(xc, loss_target[0])
    loss = lax.psum(loss_part[0, 0], ("x", "y", "c"))

    g_qkv = g_o = g_up = g_down = None
    d_mix, d_ffn = [None] * n_layers, [None] * n_layers
    d_qn, d_kn, d_rb = [], [], []
    for layer in reversed(range(n_layers)):
        du = _down_bwd(dx, wd, layer, ss[layer])
        g_down = _wgrad("wgrad_down", us[layer], _mat_spec, dx, _mat_spec_b, t, ff, d, layer, n_layers, g_down)
        g_up = _wgrad("wgrad_up", h2s[layer], _mat_spec, du, _mat_spec_b, t, d, ff, layer, n_layers, g_up)
        tk = _tile(ff, 1024)
        dx2, d_ffn[layer] = _norm_bwd_proj(
            "up_bwd", du, pl.BlockSpec((_tile(t, 512), tk), lambda m, n, k: (m, k)), ff // tk,
            wu, pl.BlockSpec((None, d, tk), lambda m, n, k, layer=layer: (layer, 0, k)),
            x2s[layer], ffn_norm[layer:layer + 1], dx)
        do = _plain_nt("attn_out_bwd", dx2, wo, layer, BF16)
        g_o = _wgrad("wgrad_o", os_[layer], _mat_spec, dx2, _mat_spec_b, t, d, d, layer, n_layers, g_o)
        if layer % 2 == 0:
            dqkv = _sb_bwd(qkvs[layer], o32s[layer], do)
        else:
            idx = layer // 2
            gq = jnp.tile(q_norm[idx], 2).reshape(1, PAIR)
            gk = jnp.tile(k_norm[idx], 2).reshape(1, PAIR)
            dqn, dkv, dext = _ca_bwd(qkvs[layer], exts[idx], do)
            dqkv, dgq, dgk = _ca_unprep(dqn, dkv, q32s[layer], gq, gk)
            d_qn.insert(0, dgq[0, :HEAD_DIM])
            d_kn.insert(0, dgk[0, :HEAD_DIM])
            d_rb.insert(0, _onehot_mm("relbias_fold", dext.reshape(heads, CA_EXT), fold_hot)[:, :N_REL])
        g_qkv = _wgrad("wgrad_qkv", h1s[layer], _mat_spec, dqkv, _plane_spec_b(d), t, d, 3 * d, layer, n_layers, g_qkv,
                       tno=_tile(d, 1024))
        tk = _tile(d, 1024)
        kpp = d // tk
        dx, d_mix[layer] = _norm_bwd_proj(
            "qkv_bwd", dqkv, pl.BlockSpec((None, _tile(t, 512), tk), lambda m, n, k: (k // kpp, m, k % kpp)), 3 * kpp,
            wq, pl.BlockSpec((None, d, tk), lambda m, n, k, layer=layer: (layer, 0, k)),
            xs[layer], mix_norm[layer:layer + 1], dx2)
    grad_x = dx.reshape(x.shape)

    grads = [g_qkv, g_o, g_up, g_down]
    other = _pair_exchange(grads)
    partial = [_add_half("add_pair", g, o_, core) for g, o_ in zip(grads, other)]
    got = _chip_exchange(partial, kinds, widths)
    halves = [_add_chips("add_chips", p, r_, k, w_, shard) for p, r_, k, w_ in zip(partial, got, kinds, widths)]
    g_big = _pair_share(halves)

    upd = [_adamw("adamw", w_, g_, m_, v_) for w_, g_, m_, v_ in
           zip(big, g_big, [m_w_qkv, m_w_o, m_w_up, m_w_down], [v_w_qkv, v_w_o, v_w_up, v_w_down])]

    small_w = [mix_norm, q_norm, k_norm, rel_bias, ffn_norm]
    small_g = [jnp.concatenate(d_mix, 0), jnp.stack(d_qn), jnp.stack(d_kn), jnp.stack(d_rb), jnp.concatenate(d_ffn, 0)]
    packed = _small_step(_pack_small(small_g), _pack_small(small_w),
                         _pack_small([m_mix_norm, m_q_norm, m_k_norm, m_rel_bias, m_ffn_norm]),
                         _pack_small([v_mix_norm, v_q_norm, v_k_norm, v_rel_bias, v_ffn_norm]))
    sg, sd, sm, sv = [_unpack_small(p, small_w) for p in packed]

    def order(small, bigs):
        return [small[0], bigs[0], bigs[1], small[1], small[2], small[3], small[4], bigs[2], bigs[3]]

    return (loss, grad_x, *order(sg, g_big), *order(sd, [u_[0] for u_ in upd]),
            *order(sm, [u_[1] for u_ in upd]), *order(sv, [u_[2] for u_ in upd]))
```

```python
import functools

import numpy as np
import jax
import jax.numpy as jnp
from jax import lax
from jax.experimental import pallas as pl
from jax.experimental.pallas import tpu as pltpu

F32, BF16 = jnp.float32, jnp.bfloat16
MESH = pl.DeviceIdType.MESH

HEAD_DIM = 64
PAIR = 128
CHUNK = 64
LEFT_CHUNKS = 8
MAX_REL = 256
N_REL = 2 * MAX_REL + 1
CA_TQ = 128
CA_SUB = 2
CA_PAD = LEFT_CHUNKS * CHUNK
CA_WIN = CA_PAD + CA_TQ
CA_EXT = CA_WIN + CA_TQ
SB_T = 128
SB_WIN = 3
SB_SUB = 2
RMS_EPS = 1e-6
SB_EXIT = -104.0
NEG_BIG = -1e30
ADAM_LR, ADAM_B1, ADAM_B2, ADAM_EPS, ADAM_WD, ADAM_STEP = 0.001, 0.9, 0.999, 1e-08, 0.01, 10
VMEM_LIMIT = 56 << 20
N_CHIPS = 4
N_DEV = 8
ANY = pl.BlockSpec(memory_space=pl.ANY)


def _cp(sem=None, **kw):
    return pltpu.CompilerParams(dimension_semantics=sem, vmem_limit_bytes=VMEM_LIMIT, **kw)


def _tile(n, want):
    t = min(n, want)
    assert n % t == 0, (n, t)
    return t


def _split_bf16(v):
    hi = v.astype(BF16)
    lo = (v - hi.astype(F32)).astype(BF16)
    return hi, lo


def _dot(a, b):
    return jnp.dot(a, b, preferred_element_type=F32)


def _dot_nt(a, b):
    return lax.dot_general(a, b, (((1,), (1,)), ((), ())), preferred_element_type=F32)


def _dot_tn(a, b):
    return lax.dot_general(a, b, (((0,), (0,)), ((), ())), preferred_element_type=F32)


TM = 512
TN = 1024


def _row_matmul(name, a, a_spec, w, layer, nt, chunks, extra, extra_specs, out_shape, out_specs, epilogue, t,
                sem=("parallel",)):
    tm = _tile(t, TM)
    n_extra = len(extra)

    def body(a_ref, w_ref, *rest):
        ins, outs = rest[:n_extra], rest[n_extra:]
        for n, terms in enumerate(chunks):
            acc = None
            for plane, rows, cols in terms:
                av = (a_ref[...] if plane is None else a_ref[plane]).astype(BF16)
                part = _dot_nt(av, w_ref[rows, cols]) if nt else _dot(av, w_ref[rows, cols])
                acc = part if acc is None else acc + part
            epilogue(n, acc, ins, outs)

    w_spec = pl.BlockSpec((None,) + w.shape[1:], lambda m: (layer, 0, 0))
    return pl.pallas_call(
        body, name=name, grid=(t // tm,), in_specs=[a_spec(tm), w_spec, *extra_specs], out_specs=out_specs, out_shape=out_shape,
        compiler_params=_cp(sem),
    )(a, w, *extra)


def _rows2(width):
    return lambda tm: pl.BlockSpec((tm, width), lambda m: (m, 0))


def _col_chunks(n, width=None):
    tn = _tile(n, TN)
    return [[(None, slice(None), slice(c * tn, (c + 1) * tn))] for c in range(n // tn)], tn


def _rms_rows(x, gain):
    r = lax.rsqrt(jnp.mean(x * x, axis=-1, keepdims=True) + RMS_EPS)
    return x * r * gain


def _rms_bwd_rows(dh, x, gain, dres):
    r = lax.rsqrt(jnp.mean(x * x, axis=-1, keepdims=True) + RMS_EPS)
    xhat = x * r
    dxn = dh * gain
    dx = r * (dxn - xhat * jnp.mean(dxn * xhat, axis=-1, keepdims=True))
    return dx + dres, jnp.sum(dh * xhat, axis=0, keepdims=True)


def _rms_first(x, gain):
    t, d = x.shape
    tm = _tile(t, 512)

    def body(x_ref, g_ref, h_ref):
        h_ref[...] = _rms_rows(x_ref[...], g_ref[...]).astype(BF16)

    return pl.pallas_call(
        body, name="rms_first", grid=(t // tm,),
        in_specs=[pl.BlockSpec((tm, d), lambda m: (m, 0)), pl.BlockSpec((1, d), lambda m: (0, 0))],
        out_specs=pl.BlockSpec((tm, d), lambda m: (m, 0)), out_shape=jax.ShapeDtypeStruct((t, d), BF16),
        compiler_params=_cp(("parallel",)),
    )(x, gain)


def _qkv_proj(h, w, layer, out_dtype):
    t, d = h.shape
    chunks = [[(None, slice(None), slice(p * d, (p + 1) * d))] for p in range(3)]

    def epi(n, acc, ins, outs):
        outs[0][n] = acc.astype(out_dtype)

    return _row_matmul(
        "qkv_proj", h, _rows2(d), w, layer, False, chunks, (), (), (jax.ShapeDtypeStruct((3, t, d), out_dtype),),
        (pl.BlockSpec((3, _tile(t, TM), d), lambda m: (0, m, 0)),), epi, t)[0]


def _out_proj(name, a, w, layer, res, gain):
    t, kk = a.shape
    d = res.shape[1]

    def epi(n, acc, ins, outs):
        xn = ins[0][...] + acc
        outs[0][...] = xn
        outs[1][...] = _rms_rows(xn, ins[1][...]).astype(BF16)

    row = _rows2(d)(_tile(t, TM))
    return _row_matmul(
        name, a, _rows2(kk), w, layer, False, [[(None, slice(None), slice(None))]], (res, gain),
        (row, pl.BlockSpec((1, d), lambda m: (0, 0))),
        (jax.ShapeDtypeStruct((t, d), F32), jax.ShapeDtypeStruct((t, d), BF16)), (row, row), epi, t)


def _up_proj(h, w, layer):
    t, d = h.shape
    ff = w.shape[2]
    chunks, tn = _col_chunks(ff)

    def epi(n, acc, ins, outs):
        s = jnp.maximum(acc, 0.0)
        outs[0][:, n * tn:(n + 1) * tn] = s.astype(BF16)
        outs[1][:, n * tn:(n + 1) * tn] = (s * s).astype(BF16)

    o = _rows2(ff)(_tile(t, TM))
    return _row_matmul("up_proj", h, _rows2(d), w, layer, False, chunks, (), (), (jax.ShapeDtypeStruct((t, ff), BF16),) * 2, (o, o), epi, t)


def _down_bwd(dx, w, layer, s):
    t, d = dx.shape
    ff = w.shape[1]
    tn = _tile(ff, TN)
    chunks = [[(None, slice(c * tn, (c + 1) * tn), slice(None))] for c in range(ff // tn)]

    def epi(n, acc, ins, outs):
        cols = slice(n * tn, (n + 1) * tn)
        outs[0][:, cols] = (acc * (2.0 * ins[0][:, cols].astype(F32))).astype(BF16)

    o = _rows2(ff)(_tile(t, TM))
    return _row_matmul("down_bwd", dx, _rows2(d), w, layer, True, chunks, (s,), (o,), (jax.ShapeDtypeStruct((t, ff), BF16),), (o,), epi, t)[0]


def _norm_bwd_proj(name, a, a_spec, terms, w, layer, x, gain, dres):
    t, d = x.shape

    def epi(n, acc, ins, outs):
        dx, dg = _rms_bwd_rows(acc, ins[0][...], ins[1][...], ins[2][...])
        outs[0][...] = dx

        @pl.when(pl.program_id(0) == 0)
        def _():
            outs[1][...] = dg

        @pl.when(pl.program_id(0) > 0)
        def _():
            outs[1][...] += dg

    row = _rows2(d)(_tile(t, TM))
    vec = pl.BlockSpec((1, d), lambda m: (0, 0))
    return _row_matmul(
        name, a, a_spec, w, layer, True, [terms], (x, gain, dres), (row, vec, row),
        (jax.ShapeDtypeStruct((t, d), F32), jax.ShapeDtypeStruct((1, d), F32)), (row, vec), epi, t, sem=("arbitrary",))


def _plain_nt(name, a, w, layer, out_dtype):
    t, n = a.shape
    m_out = w.shape[1]

    def epi(n_, acc, ins, outs):
        outs[0][...] = acc.astype(out_dtype)

    return _row_matmul(
        name, a, _rows2(n), w, layer, True, [[(None, slice(None), slice(None))]], (), (),
        (jax.ShapeDtypeStruct((t, m_out), out_dtype),), (_rows2(m_out)(_tile(t, TM)),), epi, t)[0]


def _wgrad(name, a, a_spec_fn, b, b_spec_fn, t, mo, no, layer, n_layers, prev, tno=1024):
    tmo, tno, tk = _tile(mo, 1024), _tile(no, tno), _tile(t, 2048)
    nk = t // tk

    def body(a_ref, b_ref, *rest):
        o_ref = rest[-1]
        part = _dot_tn(a_ref[...].astype(BF16), b_ref[...].astype(BF16))

        @pl.when(pl.program_id(2) == 0)
        def _():
            o_ref[...] = part

        @pl.when(pl.program_id(2) > 0)
        def _():
            o_ref[...] += part

    ins = [a, b] + ([] if prev is None else [prev])
    in_specs = [a_spec_fn(tk, tmo), b_spec_fn(tk, tno)] + ([] if prev is None else [ANY])
    return pl.pallas_call(
        body, name=name, grid=(mo // tmo, no // tno, nk), in_specs=in_specs,
        out_specs=pl.BlockSpec((None, tmo, tno), lambda m, n, k: (layer, m, n)),
        out_shape=jax.ShapeDtypeStruct((n_layers, mo, no), F32),
        input_output_aliases={} if prev is None else {2: 0},
        compiler_params=_cp(("parallel", "parallel", "arbitrary")),
    )(*ins)


def _mat_spec(tk, tw):
    return pl.BlockSpec((tk, tw), lambda m, n, k: (k, m))


def _mat_spec_b(tk, tw):
    return pl.BlockSpec((tk, tw), lambda m, n, k: (k, n))


def _plane_spec_b(d):
    def fn(tk, tw):
        npp = d // tw
        return pl.BlockSpec((None, tk, tw), lambda m, n, k: (n // npp, k, n % npp))
    return fn


def _sb_masks(t):
    row = lax.broadcasted_iota(jnp.int32, (t, t), 0)
    col = lax.broadcasted_iota(jnp.int32, (t, t), 1)
    lane = lax.broadcasted_iota(jnp.int32, (1, PAIR), 1)
    return row, col, [(lane // HEAD_DIM) == h for h in (0, 1)]


def _suffix_sums(v, tri, tail, tt):
    hi, lo = _split_bf16(v)
    parts = []
    for b in reversed(range(v.shape[1] // tt)):
        cols = slice(b * tt, (b + 1) * tt)
        parts.insert(0, _dot(hi[:, cols], tri) + _dot(lo[:, cols], tri) + tail)
        tail = tail + jnp.sum(v[:, cols], axis=1, keepdims=True)
    return (parts[0] if len(parts) == 1 else jnp.concatenate(parts, axis=1)), tail


def _sb_tile(qh, kb, valid, after, carry, tt):
    z = _dot_nt(qh, kb)
    lb = -(jnp.maximum(z, 0.0) + jnp.log(1.0 + jnp.exp(-jnp.abs(z))))
    if valid is not None:
        lb = jnp.where(valid, lb, 0.0)
    between, carry = _suffix_sums(lb, after, carry, tt)
    a = jnp.exp(z + lb + between)
    if valid is not None:
        a = jnp.where(valid, a, 0.0)
    return lb, a, carry


def _sb_window_valid(i, first, tt):
    t_pos = i * tt + lax.broadcasted_iota(jnp.int32, (tt, SB_WIN * tt), 0)
    s_pos = first * tt + lax.broadcasted_iota(jnp.int32, (tt, SB_WIN * tt), 1)
    return s_pos < t_pos


def _sb_walk_on(st):
    return jnp.logical_and(st[0] >= 0, jnp.maximum(jnp.max(st[1]), jnp.max(st[2])) > SB_EXIT)


def _sb_fwd(qkv):
    _, t, d = qkv.shape
    hp, tt = d // PAIR, _tile(t, SB_T)

    def body(q_ref, k_ref, v_ref, o_ref, o32_ref):
        row, col, head = _sb_masks(tt)
        after = (row > col).astype(BF16)

        def tile(kb, vb, valid, carry, acc, qh):
            for h in (0, 1):
                _, a, carry[h] = _sb_tile(qh[h], kb, valid, after, carry[h], tt)
                acc = acc + _dot(a.astype(BF16), jnp.where(head[h], vb, jnp.zeros_like(vb)))
            return carry, acc

        states = []
        for sub in range(SB_SUB):
            i = pl.program_id(1) * SB_SUB + sub
            q = q_ref[sub * tt:(sub + 1) * tt, :]
            qh = [jnp.where(head[h], q, jnp.zeros_like(q)) * jnp.asarray(HEAD_DIM ** -0.5, BF16) for h in (0, 1)]
            first = jnp.maximum(i - (SB_WIN - 1), 0)
            rows_w = pl.ds(pl.multiple_of(first * tt, tt), SB_WIN * tt)
            zc = jnp.zeros((tt, 1), F32)
            carry, acc = tile(k_ref[rows_w, :], v_ref[rows_w, :], _sb_window_valid(i, first, tt), [zc, zc],
                              jnp.zeros((tt, PAIR), F32), qh)
            states.append((first, carry, acc, qh))

        for sub, (first, carry, acc, qh) in enumerate(states):
            def step(st, qh=qh):
                rows = pl.ds(pl.multiple_of(st[0] * tt, tt), tt)
                carry, acc = tile(k_ref[rows, :], v_ref[rows, :], None, [st[1], st[2]], st[3], qh)
                return st[0] - 1, carry[0], carry[1], acc

            o = lax.while_loop(_sb_walk_on, step, (first - 1, carry[0], carry[1], acc))[3]
            o_ref[sub * tt:(sub + 1) * tt, :] = o.astype(BF16)
            o32_ref[sub * tt:(sub + 1) * tt, :] = o

    tq = SB_SUB * tt
    qs = pl.BlockSpec((None, tq, PAIR), lambda p, i: (0, i, p))
    ks = pl.BlockSpec((None, t, PAIR), lambda p, i: (1, 0, p))
    vs = pl.BlockSpec((None, t, PAIR), lambda p, i: (2, 0, p))
    os_ = pl.BlockSpec((tq, PAIR), lambda p, i: (i, p))
    return pl.pallas_call(
        body, name="sb_fwd", grid=(hp, t // tq), in_specs=[qs, ks, vs], out_specs=(os_, os_),
        out_shape=(jax.ShapeDtypeStruct((t, d), BF16), jax.ShapeDtypeStruct((t, d), F32)),
        compiler_params=_cp(("parallel", "arbitrary")),
    )(qkv, qkv, qkv)


def _sb_bwd(qkv, o32, do):
    _, t, d = qkv.shape
    hp, tt = d // PAIR, _tile(t, SB_T)
    nq = t // (SB_SUB * tt)
    scale = HEAD_DIM ** -0.5

    def body(q_ref, k_ref, v_ref, o32_ref, do_ref, dqkv_ref, dk_acc, dv_acc):
        i = pl.program_id(1)
        row, col, head = _sb_masks(tt)
        after = (row > col).astype(BF16)
        from_s = (row >= col).astype(BF16)

        @pl.when(i == 0)
        def _():
            dk_acc[...] = jnp.zeros_like(dk_acc)
            dv_acc[...] = jnp.zeros_like(dv_acc)

        def tile(rows, valid, carry, seen, dq, qh, doh, tot):
            kb, vb = k_ref[rows, :], v_ref[rows, :]
            dk_t = dv_t = None
            for h in (0, 1):
                lb, a, carry[h] = _sb_tile(qh[h], kb, valid, after, carry[h], tt)
                ab = a.astype(BF16)
                g = ab.astype(F32) * _dot_nt(doh[h], vb)
                g_from, seen[h] = _suffix_sums(g, from_s, seen[h], tt)
                e = jnp.exp(lb)
                dz = g * e - (1.0 - e) * (tot[h] - g_from)
                if valid is not None:
                    dz = jnp.where(valid, dz, 0.0)
                dz = dz.astype(BF16)
                dq = dq + _dot(dz, jnp.where(head[h], kb, jnp.zeros_like(kb)))
                dk_h, dv_h = _dot_tn(dz, qh[h]), _dot_tn(ab, doh[h])
                dk_t = dk_h if dk_t is None else dk_t + dk_h
                dv_t = dv_h if dv_t is None else dv_t + dv_h
            dk_acc[rows, :] += dk_t
            dv_acc[rows, :] += dv_t
            return carry, seen, dq

        states = []
        for sub in range(SB_SUB):
            ii = i * SB_SUB + sub
            q = q_ref[sub * tt:(sub + 1) * tt, :]
            dob = do_ref[sub * tt:(sub + 1) * tt, :]
            prod = dob.astype(F32) * o32_ref[sub * tt:(sub + 1) * tt, :]
            zb = jnp.zeros_like(q)
            qh = [jnp.where(head[h], q, zb) * jnp.asarray(scale, BF16) for h in (0, 1)]
            doh = [jnp.where(head[h], dob, zb) for h in (0, 1)]
            tot = [jnp.sum(jnp.where(head[h], prod, 0.0), axis=1, keepdims=True) for h in (0, 1)]
            first = jnp.maximum(ii - (SB_WIN - 1), 0)
            zc = jnp.zeros((tt, 1), F32)
            carry, seen, dq = tile(pl.ds(pl.multiple_of(first * tt, tt), SB_WIN * tt), _sb_window_valid(ii, first, tt),
                                   [zc, zc], [zc, zc], jnp.zeros((tt, PAIR), F32), qh, doh, tot)
            states.append((first, carry, seen, dq, qh, doh, tot))

        for sub, (first, carry, seen, dq, qh, doh, tot) in enumerate(states):
            def step(st, qh=qh, doh=doh, tot=tot):
                carry, seen, dq = tile(pl.ds(pl.multiple_of(st[0] * tt, tt), tt), None, [st[1], st[2]], [st[3], st[4]], st[5],
                                       qh, doh, tot)
                return st[0] - 1, carry[0], carry[1], seen[0], seen[1], dq

            dq = lax.while_loop(_sb_walk_on, step, (first - 1, carry[0], carry[1], seen[0], seen[1], dq))[5]
            dqkv_ref[0, pl.ds(pl.multiple_of((i * SB_SUB + sub) * tt, tt), tt), :] = (dq * scale).astype(BF16)

        @pl.when(i == nq - 1)
        def _():
            dqkv_ref[1, :, :] = dk_acc[...].astype(BF16)
            dqkv_ref[2, :, :] = dv_acc[...].astype(BF16)

    qs = pl.BlockSpec((None, SB_SUB * tt, PAIR), lambda p, i: (0, i, p))
    ks = pl.BlockSpec((None, t, PAIR), lambda p, i: (1, 0, p))
    vs = pl.BlockSpec((None, t, PAIR), lambda p, i: (2, 0, p))
    ts = pl.BlockSpec((SB_SUB * tt, PAIR), lambda p, i: (i, p))
    return pl.pallas_call(
        body, name="sb_bwd", grid=(hp, nq), in_specs=[qs, ks, vs, ts, ts],
        out_specs=pl.BlockSpec((3, t, PAIR), lambda p, i: (0, 0, p)), out_shape=jax.ShapeDtypeStruct((3, t, d), BF16),
        scratch_shapes=[pltpu.VMEM((t, PAIR), F32), pltpu.VMEM((t, PAIR), F32)],
        compiler_params=_cp(("parallel", "arbitrary")),
    )(qkv, qkv, qkv, o32, do)


def _pair_sum_matrix():
    r = lax.broadcasted_iota(jnp.int32, (PAIR, PAIR), 0) // HEAD_DIM
    c = lax.broadcasted_iota(jnp.int32, (PAIR, PAIR), 1) // HEAD_DIM
    return (r == c).astype(BF16)


def _head_mean(v, ones):
    hi, lo = _split_bf16(v)
    return (_dot(hi, ones) + _dot(lo, ones)) * (1.0 / HEAD_DIM)


def _ca_prep(qkv32, gq, gk):
    _, t, d = qkv32.shape
    tr = _tile(t, CA_PAD)
    assert CA_PAD % tr == 0
    npad = CA_PAD // tr

    def body(x_ref, gq_ref, gk_ref, o_ref):
        p, r = pl.program_id(0), pl.program_id(1)

        @pl.when(r < npad)
        def _():
            o_ref[...] = jnp.zeros_like(o_ref)

        @pl.when(jnp.logical_and(r >= npad, p == 2))
        def _():
            o_ref[...] = x_ref[...].astype(BF16)

        @pl.when(jnp.logical_and(r >= npad, p < 2))
        def _():
            ones = _pair_sum_matrix()
            g = jnp.where(p == 0, gq_ref[...], gk_ref[...])
            for c in range(d // PAIR):
                x = x_ref[:, c * PAIR:(c + 1) * PAIR]
                rs = lax.rsqrt(_head_mean(x * x, ones) + RMS_EPS)
                o_ref[:, c * PAIR:(c + 1) * PAIR] = (x * rs * g).astype(BF16)

    vec = pl.BlockSpec((1, PAIR), lambda p, r: (0, 0))
    return pl.pallas_call(
        body, name="ca_prep", grid=(3, npad + t // tr),
        in_specs=[pl.BlockSpec((None, tr, d), lambda p, r: (p, jnp.maximum(r - npad, 0), 0)), vec, vec],
        out_specs=pl.BlockSpec((None, tr, d), lambda p, r: (p, r, 0)),
        out_shape=jax.ShapeDtypeStruct((3, CA_PAD + t, d), BF16), compiler_params=_cp(("parallel", "parallel")),
    )(qkv32, gq, gk)


def _ca_unprep(dq, dkv, qkv32, gq, gk):
    _, t, d = qkv32.shape
    tr = _tile(t, 512)
    nr = t // tr

    def body(dq_ref, dkv_ref, x_ref, gq_ref, gk_ref, o_ref, dgq_ref, dgk_ref):
        p, r = pl.program_id(0), pl.program_id(1)

        @pl.when(jnp.logical_and(p == 0, r == 0))
        def _():
            dgq_ref[...] = jnp.zeros_like(dgq_ref)
            dgk_ref[...] = jnp.zeros_like(dgk_ref)

        @pl.when(p == 2)
        def _():
            o_ref[...] = dkv_ref[...].astype(BF16)

        @pl.when(p < 2)
        def _():
            ones = _pair_sum_matrix()
            g = jnp.where(p == 0, gq_ref[...], gk_ref[...])
            dg = jnp.zeros((1, PAIR), F32)
            for c in range(d // PAIR):
                cols = slice(c * PAIR, (c + 1) * PAIR)
                x = x_ref[:, cols]
                dy = jnp.where(p == 0, dq_ref[:, cols], dkv_ref[:, cols])
                rs = lax.rsqrt(_head_mean(x * x, ones) + RMS_EPS)
                xhat = x * rs
                dxn = dy * g
                o_ref[:, cols] = (rs * (dxn - xhat * _head_mean(dxn * xhat, ones))).astype(BF16)
                dg = dg + jnp.sum(dy * xhat, axis=0, keepdims=True)
            dg = dg + pltpu.roll(dg, HEAD_DIM, 1)

            @pl.when(p == 0)
            def _():
                dgq_ref[...] += dg

            @pl.when(p == 1)
            def _():
                dgk_ref[...] += dg

    vec = pl.BlockSpec((1, PAIR), lambda p, r: (0, 0))
    return pl.pallas_call(
        body, name="ca_unprep", grid=(3, nr),
        in_specs=[pl.BlockSpec((tr, d), lambda p, r: (r, 0)),
                  pl.BlockSpec((None, tr, d), lambda p, r: (jnp.maximum(p - 1, 0), r, 0)),
                  pl.BlockSpec((None, tr, d), lambda p, r: (p, r, 0)), vec, vec],
        out_specs=(pl.BlockSpec((None, tr, d), lambda p, r: (p, r, 0)), vec, vec),
        out_shape=(jax.ShapeDtypeStruct((3, t, d), BF16), jax.ShapeDtypeStruct((1, PAIR), F32), jax.ShapeDtypeStruct((1, PAIR), F32)),
        compiler_params=_cp(("arbitrary", "arbitrary")),
    )(dq, dkv, qkv32, gq, gk)


def _ca_ext_index():
    m = np.arange(CA_EXT)
    return np.where(m <= CA_WIN, np.clip(CA_PAD - m, -MAX_REL, MAX_REL) + MAX_REL, 2 * MAX_REL).astype(np.int32)


def _ca_valid(i):
    a = lax.broadcasted_iota(jnp.int32, (CA_TQ, CA_WIN), 0)
    b = lax.broadcasted_iota(jnp.int32, (CA_TQ, CA_WIN), 1)
    ca, cb = a // CHUNK, b // CHUNK
    return jnp.logical_and(jnp.logical_and(cb >= ca, cb <= ca + LEFT_CHUNKS), b >= CA_PAD - CA_TQ * i)


def _ca_bias_tiles(ext_ref, bias_ref):
    for h in (0, 1):
        e = ext_ref[pl.ds(h, 1), :]

        def put(a, carry, h=h, e=e):
            bias_ref[h, pl.ds(a, 1), :] = pltpu.roll(e, a, 1)
            return carry

        lax.fori_loop(0, CA_TQ, put, 0)


def _ca_probs(qh, kw, bias, valid):
    z = jnp.where(valid, _dot_nt(qh, kw) + bias, NEG_BIG)
    p = jnp.exp(z - jnp.max(z, axis=1, keepdims=True))
    return p / jnp.sum(p, axis=1, keepdims=True)


def _ca_fwd(qkvn, ext):
    _, tp, d = qkvn.shape
    t = tp - CA_PAD
    hp, tq = d // PAIR, CA_SUB * CA_TQ
    npad = CA_PAD // tq

    def body(q_ref, k_ref, v_ref, ext_ref, o_ref, bias_ref):
        i = pl.program_id(1)

        @pl.when(i == 0)
        def _():
            _ca_bias_tiles(ext_ref, bias_ref)

        lane = lax.broadcasted_iota(jnp.int32, (1, PAIR), 1)
        for sub in range(CA_SUB):
            ii = i * CA_SUB + sub
            valid = _ca_valid(ii)
            win = pl.ds(pl.multiple_of(ii * CA_TQ, CA_TQ), CA_WIN)
            kw, vw, q = k_ref[win, :], v_ref[win, :], q_ref[sub * CA_TQ:(sub + 1) * CA_TQ, :]
            o = jnp.zeros((CA_TQ, PAIR), F32)
            for h in (0, 1):
                hm = (lane // HEAD_DIM) == h
                qh = jnp.where(hm, q, jnp.zeros_like(q)) * jnp.asarray(HEAD_DIM ** -0.5, BF16)
                p = _ca_probs(qh, kw, bias_ref[h, :, :CA_WIN], valid)
                o = o + _dot(p.astype(BF16), jnp.where(hm, vw, jnp.zeros_like(vw)))
            o_ref[sub * CA_TQ:(sub + 1) * CA_TQ, :] = o.astype(BF16)

    return pl.pallas_call(
        body, name="ca_fwd", grid=(hp, t // tq),
        in_specs=[pl.BlockSpec((None, tq, PAIR), lambda p, i: (0, i + npad, p)),
                  pl.BlockSpec((None, tp, PAIR), lambda p, i: (1, 0, p)),
                  pl.BlockSpec((None, tp, PAIR), lambda p, i: (2, 0, p)),
                  pl.BlockSpec((None, 2, CA_EXT), lambda p, i: (p, 0, 0))],
        out_specs=pl.BlockSpec((tq, PAIR), lambda p, i: (i, p)), out_shape=jax.ShapeDtypeStruct((t, d), BF16),
        scratch_shapes=[pltpu.VMEM((2, CA_TQ, CA_EXT), F32)], compiler_params=_cp(("parallel", "arbitrary")),
    )(qkvn, qkvn, qkvn, ext)


def _ca_bwd(qkvn, ext, do):
    _, tp, d = qkvn.shape
    t = tp - CA_PAD
    hp, tq = d // PAIR, CA_SUB * CA_TQ
    npad, nq = CA_PAD // tq, t // tq
    scale = HEAD_DIM ** -0.5

    def body(q_ref, k_ref, v_ref, ext_ref, do_ref, dq_ref, dkv_ref, dext_ref, bias_ref, dbias_ref, dk_acc, dv_acc):
        i = pl.program_id(1)

        @pl.when(i == 0)
        def _():
            _ca_bias_tiles(ext_ref, bias_ref)
            dbias_ref[...] = jnp.zeros_like(dbias_ref)
            dk_acc[...] = jnp.zeros_like(dk_acc)
            dv_acc[...] = jnp.zeros_like(dv_acc)

        lane = lax.broadcasted_iota(jnp.int32, (1, PAIR), 1)
        dbias = [None, None]
        dk_u = dv_u = None

        def spread(v, sub):
            parts = [jnp.zeros((sub * CA_TQ, PAIR), F32)] * (sub > 0) + [v] + \
                    [jnp.zeros(((CA_SUB - 1 - sub) * CA_TQ, PAIR), F32)] * (sub < CA_SUB - 1)
            return jnp.concatenate(parts, axis=0) if len(parts) > 1 else v

        for sub in range(CA_SUB):
            ii = i * CA_SUB + sub
            valid = _ca_valid(ii)
            win = pl.ds(pl.multiple_of(ii * CA_TQ, CA_TQ), CA_WIN)
            rows = slice(sub * CA_TQ, (sub + 1) * CA_TQ)
            kw, vw, q, dob = k_ref[win, :], v_ref[win, :], q_ref[rows, :], do_ref[rows, :]
            dq = jnp.zeros((CA_TQ, PAIR), F32)
            dk_t = jnp.zeros((CA_WIN, PAIR), F32)
            dv_t = jnp.zeros((CA_WIN, PAIR), F32)
            for h in (0, 1):
                hm = (lane // HEAD_DIM) == h
                qh = jnp.where(hm, q, jnp.zeros_like(q)) * jnp.asarray(scale, BF16)
                doh = jnp.where(hm, dob, jnp.zeros_like(dob))
                p = _ca_probs(qh, kw, bias_ref[h, :, :CA_WIN], valid)
                dp = _dot_nt(doh, vw)
                ds = p * (dp - jnp.sum(p * dp, axis=1, keepdims=True))
                dbias[h] = ds if dbias[h] is None else dbias[h] + ds
                dsb = ds.astype(BF16)
                dq = dq + _dot(dsb, jnp.where(hm, kw, jnp.zeros_like(kw)))
                dk_t = dk_t + _dot_tn(dsb, qh)
                dv_t = dv_t + _dot_tn(p.astype(BF16), doh)
            dq_ref[rows, :] = dq * scale
            dk_u = spread(dk_t, sub) if dk_u is None else dk_u + spread(dk_t, sub)
            dv_u = spread(dv_t, sub) if dv_u is None else dv_u + spread(dv_t, sub)
        union = pl.ds(pl.multiple_of(i * tq, CA_TQ), CA_WIN + (CA_SUB - 1) * CA_TQ)
        dk_acc[union, :] += dk_u
        dv_acc[union, :] += dv_u
        for h in (0, 1):
            dbias_ref[h, :, :CA_WIN] += dbias[h]

        @pl.when(i == nq - 1)
        def _():
            dkv_ref[0, :, :] = dk_acc[CA_PAD:, :]
            dkv_ref[1, :, :] = dv_acc[CA_PAD:, :]
            for h in (0, 1):
                def take(a, acc, h=h):
                    return acc + pltpu.roll(dbias_ref[h, pl.ds(a, 1), :], lax.rem(CA_EXT - a, CA_EXT), 1)

                dext_ref[pl.ds(h, 1), :] = lax.fori_loop(0, CA_TQ, take, jnp.zeros((1, CA_EXT), F32))

    es = pl.BlockSpec((None, 2, CA_EXT), lambda p, i: (p, 0, 0))
    ts = pl.BlockSpec((tq, PAIR), lambda p, i: (i, p))
    return pl.pallas_call(
        body, name="ca_bwd", grid=(hp, nq),
        in_specs=[pl.BlockSpec((None, tq, PAIR), lambda p, i: (0, i + npad, p)),
                  pl.BlockSpec((None, tp, PAIR), lambda p, i: (1, 0, p)),
                  pl.BlockSpec((None, tp, PAIR), lambda p, i: (2, 0, p)), es, ts],
        out_specs=(ts, pl.BlockSpec((2, t, PAIR), lambda p, i: (0, 0, p)), es),
        out_shape=(jax.ShapeDtypeStruct((t, d), F32), jax.ShapeDtypeStruct((2, t, d), F32), jax.ShapeDtypeStruct((hp, 2, CA_EXT), F32)),
        scratch_shapes=[pltpu.VMEM((2, CA_TQ, CA_EXT), F32), pltpu.VMEM((2, CA_TQ, CA_EXT), F32),
                        pltpu.VMEM((tp, PAIR), F32), pltpu.VMEM((tp, PAIR), F32)],
        compiler_params=_cp(("parallel", "arbitrary")),
    )(qkvn, qkvn, qkvn, ext, do)


def _loss_head(y, target):
    t, d = y.shape
    tm = _tile(t, 512)

    def body(y_ref, t_ref, dy_ref, loss_ref):
        diff = y_ref[...] - t_ref[...]
        dy_ref[...] = diff * (1.0 / d)
        part = 0.5 * jnp.sum(jnp.mean(diff * diff, axis=-1, keepdims=True), axis=0, keepdims=True)

        @pl.when(pl.program_id(0) == 0)
        def _():
            loss_ref[...] = jnp.zeros_like(loss_ref)

        loss_ref[...] += jnp.broadcast_to(part, loss_ref.shape)

    row = pl.BlockSpec((tm, d), lambda m: (m, 0))
    return pl.pallas_call(
        body, name="loss_head", grid=(t // tm,), in_specs=[row, row],
        out_specs=(row, pl.BlockSpec((8, 128), lambda m: (0, 0))),
        out_shape=(jax.ShapeDtypeStruct((t, d), F32), jax.ShapeDtypeStruct((8, 128), F32)),
        compiler_params=_cp(("arbitrary",)),
    )(y, target)


def _adamw_math(w, g, m, v):
    m = ADAM_B1 * m + (1.0 - ADAM_B1) * g
    v = ADAM_B2 * v + (1.0 - ADAM_B2) * (g * g)
    m_hat = m / (1.0 - ADAM_B1 ** ADAM_STEP)
    v_hat = v / (1.0 - ADAM_B2 ** ADAM_STEP)
    delta = -ADAM_LR * (m_hat / (jnp.sqrt(v_hat) + ADAM_EPS) + ADAM_WD * w)
    return delta, m, v


def _adamw(name, w, g, m, v):
    shape = w.shape
    cols = shape[-1]
    rows = int(np.prod(shape[:-1]))
    tr = _tile(rows, 512)
    flat = [a.reshape(rows, cols) for a in (w, g, m, v)]

    def body(w_ref, g_ref, m_ref, v_ref, d_ref, nm_ref, nv_ref):
        d_ref[...], nm_ref[...], nv_ref[...] = _adamw_math(w_ref[...], g_ref[...], m_ref[...], v_ref[...])

    blk = pl.BlockSpec((tr, cols), lambda r: (r, 0))
    outs = pl.pallas_call(
        body, name=name, grid=(rows // tr,), in_specs=[blk] * 4, out_specs=(blk,) * 3,
        out_shape=(jax.ShapeDtypeStruct((rows, cols), F32),) * 3, compiler_params=_cp(("parallel",)),
    )(*flat)
    return [o.reshape(shape) for o in outs]


def _place():
    x, y, c = lax.axis_index("x"), lax.axis_index("y"), lax.axis_index("c")
    chips = [(1 - x, y), (x, 1 - y), (1 - x, 1 - y)]
    return x, y, c, chips


def _shard_slab(ref, kind, layer0, n_layers, shard, width):
    lay = pl.ds(layer0, n_layers)
    if kind == "cols":
        return ref.at[lay, :, pl.ds(shard * width, width)]
    return ref.at[lay, pl.ds(shard * width, width), :]


def _gather_weights(shards, kinds):
    n_layers = shards[0].shape[0]
    lh = n_layers // 2
    na = len(shards)
    widths = [s.shape[2] if k == "cols" else s.shape[1] for s, k in zip(shards, kinds)]
    full_shapes = []
    for s, k in zip(shards, kinds):
        shp = list(s.shape)
        shp[2 if k == "cols" else 1] *= N_CHIPS
        full_shapes.append(jax.ShapeDtypeStruct(tuple(shp), s.dtype))

    def body(*refs):
        src, dst = refs[:na], refs[na:2 * na]
        send_sems, recv_sems, local_sems = refs[2 * na:]
        x, y, c, chips = _place()
        mine = 2 * x + y
        sibling = (x, y, 1 - c)

        def slab(a, half, shard):
            return _shard_slab(dst[a], kinds[a], half * lh, lh, shard, widths[a])

        def remote(k, a, half, shard, to, source=None):
            return pltpu.make_async_remote_copy(
                src_ref=slab(a, half, shard) if source is None else source, dst_ref=slab(a, half, shard),
                send_sem=send_sems.at[k], recv_sem=recv_sems.at[k], device_id=to, device_id_type=MESH)

        local = [pltpu.make_async_copy(src[a], _shard_slab(dst[a], kinds[a], 0, n_layers, mine, widths[a]), local_sems.at[a])
                 for a in range(na)]
        for cp in local:
            cp.start()
        sends = []
        for j, chip in enumerate(chips):
            for a in range(na):
                sends.append(remote(j * na + a, a, c, mine, (*chip, c), source=src[a].at[pl.ds(c * lh, lh)]))
        for cp in sends:
            cp.start()
        passed = []
        for j, (cx, cy) in enumerate(chips):
            for a in range(na):
                k = j * na + a
                remote(k, a, c, 2 * cx + cy, (x, y, c)).wait_recv()
                fwd = remote(3 * na + k, a, c, 2 * cx + cy, sibling)
                fwd.start()
                passed.append(fwd)
        for j, (cx, cy) in enumerate(chips):
            for a in range(na):
                remote(3 * na + j * na + a, a, 1 - c, 2 * cx + cy, (x, y, c)).wait_recv()
        for cp in sends + passed:
            cp.wait_send()
        for cp in local:
            cp.wait()

    n_sem = 6 * na
    return pl.pallas_call(
        body, name="gather_weights", in_specs=[ANY] * na, out_specs=[ANY] * na, out_shape=full_shapes,
        scratch_shapes=[pltpu.SemaphoreType.DMA((n_sem,)), pltpu.SemaphoreType.DMA((n_sem,)), pltpu.SemaphoreType.DMA((na,))],
    )(*shards)


def _pair_exchange(grads):
    n_layers = grads[0].shape[0]
    lh = n_layers // 2
    na = len(grads)

    def body(*refs):
        src, dst = refs[:na], refs[na:2 * na]
        send_sems, recv_sems = refs[2 * na:]
        x, y, c, _ = _place()
        cps = [pltpu.make_async_remote_copy(src_ref=src[a].at[pl.ds((1 - c) * lh, lh)], dst_ref=dst[a], send_sem=send_sems.at[a],
                                            recv_sem=recv_sems.at[a], device_id=(x, y, 1 - c), device_id_type=MESH) for a in range(na)]
        for cp in cps:
            cp.start()
        for cp in cps:
            cp.wait()

    return pl.pallas_call(
        body, name="pair_exchange", in_specs=[ANY] * na, out_specs=[ANY] * na,
        out_shape=[jax.ShapeDtypeStruct((lh,) + g.shape[1:], g.dtype) for g in grads],
        scratch_shapes=[pltpu.SemaphoreType.DMA((na,)), pltpu.SemaphoreType.DMA((na,))],
    )(*grads)


def _add_half(name, full, other, c):
    n_layers, rows, cols = full.shape
    lh = n_layers // 2
    tr = _tile(rows, 512)
    nr = rows // tr

    def body(c_ref, a_ref, b_ref, o_ref):
        o_ref[...] = (a_ref[...] + b_ref[...]).astype(BF16)

    grid_spec = pltpu.PrefetchScalarGridSpec(
        num_scalar_prefetch=1, grid=(lh, nr),
        in_specs=[pl.BlockSpec((None, tr, cols), lambda l, r, c_ref: (c_ref[0] * lh + l, r, 0)),
                  pl.BlockSpec((None, tr, cols), lambda l, r, c_ref: (l, r, 0))],
        out_specs=pl.BlockSpec((None, tr, cols), lambda l, r, c_ref: (l, r, 0)))
    return pl.pallas_call(body, name=name, grid_spec=grid_spec, out_shape=jax.ShapeDtypeStruct((lh, rows, cols), BF16),
                          compiler_params=_cp(("parallel", "parallel")))(c, full, other)


def _chip_exchange(partials, kinds, widths):
    lh = partials[0].shape[0]
    na = len(partials)
    out_shapes = []
    for p, k, w in zip(partials, kinds, widths):
        shp = (lh, p.shape[1], w) if k == "cols" else (lh, w, p.shape[2])
        out_shapes.append(jax.ShapeDtypeStruct((3,) + shp, p.dtype))

    def body(*refs):
        src, dst = refs[:na], refs[na:2 * na]
        send_sems, recv_sems = refs[2 * na:]
        x, y, c, chips = _place()
        cps = []
        for j, (cx, cy) in enumerate(chips):
            for a in range(na):
                k = j * na + a
                cps.append(pltpu.make_async_remote_copy(
                    src_ref=_shard_slab(src[a], kinds[a], 0, lh, 2 * cx + cy, widths[a]), dst_ref=dst[a].at[j],
                    send_sem=send_sems.at[k], recv_sem=recv_sems.at[k], device_id=(cx, cy, c), device_id_type=MESH))
        for cp in cps:
            cp.start()
        for cp in cps:
            cp.wait()

    return pl.pallas_call(
        body, name="chip_exchange", in_specs=[ANY] * na, out_specs=[ANY] * na, out_shape=out_shapes,
        scratch_shapes=[pltpu.SemaphoreType.DMA((3 * na,)), pltpu.SemaphoreType.DMA((3 * na,))],
    )(*partials)


def _add_chips(name, partial, got, kind, width, shard):
    lh = partial.shape[0]
    _, _, rows, cols = got.shape
    tr = _tile(rows, 512)
    nr = rows // tr

    def body(s_ref, a_ref, b_ref, o_ref):
        o_ref[...] = ((a_ref[...].astype(F32) + b_ref[0].astype(F32)) + b_ref[1].astype(F32)) + b_ref[2].astype(F32)

    if kind == "cols":
        own = pl.BlockSpec((None, tr, width), lambda l, r, s_ref: (l, r, s_ref[0]))
    else:
        own = pl.BlockSpec((None, tr, cols), lambda l, r, s_ref: (l, s_ref[0] * nr + r, 0))
    grid_spec = pltpu.PrefetchScalarGridSpec(
        num_scalar_prefetch=1, grid=(lh, nr),
        in_specs=[own, pl.BlockSpec((3, None, tr, cols), lambda l, r, s_ref: (0, l, r, 0))],
        out_specs=pl.BlockSpec((None, tr, cols), lambda l, r, s_ref: (l, r, 0)))
    return pl.pallas_call(body, name=name, grid_spec=grid_spec, out_shape=jax.ShapeDtypeStruct((lh, rows, cols), F32),
                          compiler_params=_cp(("parallel", "parallel")))(shard, partial, got)


def _pair_share(halves):
    lh = halves[0].shape[0]
    na = len(halves)

    def body(*refs):
        src, dst = refs[:na], refs[na:2 * na]
        send_sems, recv_sems, local_sems = refs[2 * na:]
        x, y, c, _ = _place()
        local = [pltpu.make_async_copy(src[a], dst[a].at[pl.ds(c * lh, lh)], local_sems.at[a]) for a in range(na)]
        for cp in local:
            cp.start()
        cps = [pltpu.make_async_remote_copy(src_ref=src[a], dst_ref=dst[a].at[pl.ds(c * lh, lh)], send_sem=send_sems.at[a],
                                            recv_sem=recv_sems.at[a], device_id=(x, y, 1 - c), device_id_type=MESH) for a in range(na)]
        for cp in cps:
            cp.start()
        for a, cp in enumerate(cps):
            cp.wait_send()
            pltpu.make_async_remote_copy(src_ref=src[a], dst_ref=dst[a].at[pl.ds((1 - c) * lh, lh)], send_sem=send_sems.at[a],
                                         recv_sem=recv_sems.at[a], device_id=(x, y, 1 - c), device_id_type=MESH).wait_recv()
        for cp in local:
            cp.wait()

    return pl.pallas_call(
        body, name="pair_share", in_specs=[ANY] * na, out_specs=[ANY] * na,
        out_shape=[jax.ShapeDtypeStruct((2 * lh,) + h.shape[1:], h.dtype) for h in halves],
        scratch_shapes=[pltpu.SemaphoreType.DMA((na,)), pltpu.SemaphoreType.DMA((na,)), pltpu.SemaphoreType.DMA((na,))],
    )(*halves)


def _small_step(g_part, w, m, v):
    r = g_part.shape[0]

    def body(g_ref, w_ref, m_ref, v_ref, go_ref, d_ref, nm_ref, nv_ref, all_ref, send_sems, recv_sems):
        x, y, c, _ = _place()
        me = 4 * x + 2 * y + c
        all_ref[me] = g_ref[...]
        cps = []
        for k in range(1, N_DEV):
            px, py, pc = (x + (k >> 2)) % 2, (y + ((k >> 1) & 1)) % 2, (c + (k & 1)) % 2
            cps.append(pltpu.make_async_remote_copy(src_ref=g_ref, dst_ref=all_ref.at[me], send_sem=send_sems.at[k - 1],
                                                    recv_sem=recv_sems.at[k - 1], device_id=(px, py, pc), device_id_type=MESH))
        for cp in cps:
            cp.start()
        for cp in cps:
            cp.wait()
        g = all_ref[0]
        for k in range(1, N_DEV):
            g = g + all_ref[k]
        go_ref[...] = g
        d_ref[...], nm_ref[...], nv_ref[...] = _adamw_math(w_ref[...], g, m_ref[...], v_ref[...])

    vm = pl.BlockSpec(memory_space=pltpu.VMEM)
    return pl.pallas_call(
        body, name="small_step", in_specs=[vm] * 4, out_specs=[vm] * 4, out_shape=[jax.ShapeDtypeStruct((r, 128), F32)] * 4,
        scratch_shapes=[pltpu.VMEM((N_DEV, r, 128), F32), pltpu.SemaphoreType.DMA((N_DEV - 1,)), pltpu.SemaphoreType.DMA((N_DEV - 1,))],
    )(g_part, w, m, v)


def _onehot_mm(name, a, onehot):
    def body(a_ref, oh_ref, o_ref):
        v = a_ref[...]
        oh = oh_ref[...]
        hi, lo = _split_bf16(v)
        lo2 = (v - hi.astype(F32) - lo.astype(F32)).astype(BF16)
        o_ref[...] = _dot(hi, oh) + _dot(lo, oh) + _dot(lo2, oh)

    return pl.pallas_call(body, name=name, out_shape=jax.ShapeDtypeStruct((a.shape[0], onehot.shape[1]), F32))(a, onehot)


def _pack_small(parts):
    flat = jnp.concatenate([p.reshape(-1) for p in parts])
    n = flat.shape[0]
    rows = -(-n // 128)
    rows = -(-rows // 8) * 8
    return jnp.pad(flat, (0, rows * 128 - n)).reshape(rows, 128)


def _unpack_small(packed, like):
    flat = packed.reshape(-1)
    out, off = [], 0
    for p in like:
        out.append(flat[off:off + p.size].reshape(p.shape))
        off += p.size
    return out


def kernel(x, mix_norm, w_qkv, w_o, q_norm, k_norm, rel_bias, ffn_norm, w_up, w_down, loss_target, m_mix_norm, m_w_qkv, m_w_o, m_q_norm, m_k_norm, m_rel_bias, m_ffn_norm, m_w_up, m_w_down, v_mix_norm, v_w_qkv, v_w_o, v_q_norm, v_k_norm, v_rel_bias, v_ffn_norm, v_w_up, v_w_down):
    n_layers, d = mix_norm.shape
    t = x.shape[1]
    ff = w_down.shape[1] * N_CHIPS
    heads = d // HEAD_DIM
    cx, cy, cc = lax.axis_index("x"), lax.axis_index("y"), lax.axis_index("c")
    shard = (2 * cx + cy).astype(jnp.int32).reshape(1)
    core = cc.astype(jnp.int32).reshape(1)

    big = [w_qkv, w_o, w_up, w_down]
    kinds = ["cols", "rows", "cols", "rows"]
    widths = [w_qkv.shape[2], w_o.shape[1], w_up.shape[2], w_down.shape[1]]
    wq, wo, wu, wd = _gather_weights([w.astype(BF16) for w in big], kinds)

    rel_pad = -(-N_REL // PAIR) * PAIR
    ext_hot = _ca_ext_index()[:, None] == np.arange(rel_pad)[None, :]
    fold_hot, spread_hot = jnp.asarray(ext_hot, BF16), jnp.asarray(ext_hot.T, BF16)
    rel_tab = jnp.pad(rel_bias, ((0, 0), (0, 0), (0, rel_pad - N_REL)))
    exts = [_onehot_mm("relbias_spread", rel_tab[i], spread_hot).reshape(heads // 2, 2, CA_EXT) for i in range(n_layers // 2)]

    xs, h1s, qkvs, os_, o32s, x2s, h2s, ss, us, q32s = [], [], [], [], [], [], [], [], [], []
    xc = x[0]
    h = _rms_first(xc, mix_norm[0:1])
    for layer in range(n_layers):
        xs.append(xc)
        h1s.append(h)
        if layer % 2 == 0:
            qkv = _qkv_proj(h, wq, layer, BF16)
            o, o32 = _sb_fwd(qkv)
            q32s.append(None)
        else:
            idx = layer // 2
            q32 = _qkv_proj(h, wq, layer, F32)
            gq = jnp.tile(q_norm[idx], 2).reshape(1, PAIR)
            gk = jnp.tile(k_norm[idx], 2).reshape(1, PAIR)
            qkv = _ca_prep(q32, gq, gk)
            o, o32 = _ca_fwd(qkv, exts[idx]), None
            q32s.append(q32)
        qkvs.append(qkv)
        os_.append(o)
        o32s.append(o32)
        x2, h2 = _out_proj("attn_out", o, wo, layer, xc, ffn_norm[layer:layer + 1])
        s, u = _up_proj(h2, wu, layer)
        nxt = mix_norm[layer + 1:layer + 2] if layer + 1 < n_layers else mix_norm[0:1]
        xc, h = _out_proj("mlp_out", u, wd, layer, x2, nxt)
        x2s.append(x2)
        h2s.append(h2)
        ss.append(s)
        us.append(u)

    dx, loss_part = _loss_head(xc, loss_target[0])
    loss = lax.psum(loss_part[0, 0], ("x", "y", "c"))

    g_qkv = g_o = g_up = g_down = None
    d_mix, d_ffn = [None] * n_layers, [None] * n_layers
    d_qn, d_kn, d_rb = [], [], []
    for layer in reversed(range(n_layers)):
        du = _down_bwd(dx, wd, layer, ss[layer])
        g_down = _wgrad("wgrad_down", us[layer], _mat_spec, dx, _mat_spec_b, t, ff, d, layer, n_layers, g_down)
        g_up = _wgrad("wgrad_up", h2s[layer], _mat_spec, du, _mat_spec_b, t, d, ff, layer, n_layers, g_up)
        dx2, d_ffn[layer] = _norm_bwd_proj(
            "up_bwd", du, _rows2(ff), [(None, slice(None), slice(None))], wu, layer, x2s[layer], ffn_norm[layer:layer + 1], dx)
        do = _plain_nt("attn_out_bwd", dx2, wo, layer, BF16)
        g_o = _wgrad("wgrad_o", os_[layer], _mat_spec, dx2, _mat_spec_b, t, d, d, layer, n_layers, g_o)
        if layer % 2 == 0:
            dqkv = _sb_bwd(qkvs[layer], o32s[layer], do)
        else:
            idx = layer // 2
            gq = jnp.tile(q_norm[idx], 2).reshape(1, PAIR)
            gk = jnp.tile(k_norm[idx], 2).reshape(1, PAIR)
            dqn, dkv, dext = _ca_bwd(qkvs[layer], exts[idx], do)
            dqkv, dgq, dgk = _ca_unprep(dqn, dkv, q32s[layer], gq, gk)
            d_qn.insert(0, dgq[0, :HEAD_DIM])
            d_kn.insert(0, dgk[0, :HEAD_DIM])
            d_rb.insert(0, _onehot_mm("relbias_fold", dext.reshape(heads, CA_EXT), fold_hot)[:, :N_REL])
        g_qkv = _wgrad("wgrad_qkv", h1s[layer], _mat_spec, dqkv, _plane_spec_b(d), t, d, 3 * d, layer, n_layers, g_qkv,
                       tno=_tile(d, 1024))
        dx, d_mix[layer] = _norm_bwd_proj(
            "qkv_bwd", dqkv, lambda tm: pl.BlockSpec((3, tm, d), lambda m: (0, m, 0)),
            [(p, slice(None), slice(p * d, (p + 1) * d)) for p in range(3)], wq, layer, xs[layer], mix_norm[layer:layer + 1], dx2)
    grad_x = dx.reshape(x.shape)

    grads = [g_qkv, g_o, g_up, g_down]
    other = _pair_exchange(grads)
    partial = [_add_half("add_pair", g, o_, core) for g, o_ in zip(grads, other)]
    got = _chip_exchange(partial, kinds, widths)
    halves = [_add_chips("add_chips", p, r_, k, w_, shard) for p, r_, k, w_ in zip(partial, got, kinds, widths)]
    g_big = _pair_share(halves)

    upd = [_adamw("adamw", w_, g_, m_, v_) for w_, g_, m_, v_ in
           zip(big, g_big, [m_w_qkv, m_w_o, m_w_up, m_w_down], [v_w_qkv, v_w_o, v_w_up, v_w_down])]

    small_w = [mix_norm, q_norm, k_norm, rel_bias, ffn_norm]
    small_g = [jnp.concatenate(d_mix, 0), jnp.stack(d_qn), jnp.stack(d_kn), jnp.stack(d_rb), jnp.concatenate(d_ffn, 0)]
    packed = _small_step(_pack_small(small_g), _pack_small(small_w),
                         _pack_small([m_mix_norm, m_q_norm, m_k_norm, m_rel_bias, m_ffn_norm]),
                         _pack_small([v_mix_norm, v_q_norm, v_k_norm, v_rel_bias, v_ffn_norm]))
    sg, sd, sm, sv = [_unpack_small(p, small_w) for p in packed]

    def order(small, bigs):
        return [small[0], bigs[0], bigs[1], small[1], small[2], small[3], small[4], bigs[2], bigs[3]]

    return (loss, grad_x, *order(sg, g_big), *order(sd, [u_[0] for u_ in upd]),
            *order(sm, [u_[1] for u_ in upd]), *order(sv, [u_[2] for u_ in upd]))
```

```python
import functools

import numpy as np
import jax
import jax.numpy as jnp
from jax import lax
from jax.experimental import pallas as pl
from jax.experimental.pallas import tpu as pltpu

F32, BF16 = jnp.float32, jnp.bfloat16
MESH = pl.DeviceIdType.MESH

HEAD_DIM = 64
PAIR = 128
CHUNK = 64
LEFT_CHUNKS = 8
MAX_REL = 256
N_REL = 2 * MAX_REL + 1
CA_TQ = 128
CA_SUB = 2
CA_PAD = LEFT_CHUNKS * CHUNK
CA_WIN = CA_PAD + CA_TQ
CA_EXT = CA_WIN + CA_TQ
SB_T = 128
SB_WIN = 3
SB_SUB = 2
RMS_EPS = 1e-6
SB_EXIT = -104.0
NEG_BIG = -1e30
ADAM_LR, ADAM_B1, ADAM_B2, ADAM_EPS, ADAM_WD, ADAM_STEP = 0.001, 0.9, 0.999, 1e-08, 0.01, 10
VMEM_LIMIT = 56 << 20
N_CHIPS = 4
N_DEV = 8
ANY = pl.BlockSpec(memory_space=pl.ANY)


def _cp(sem=None, **kw):
    return pltpu.CompilerParams(dimension_semantics=sem, vmem_limit_bytes=VMEM_LIMIT, **kw)


def _tile(n, want):
    t = min(n, want)
    assert n % t == 0, (n, t)
    return t


def _split_bf16(v):
    hi = v.astype(BF16)
    lo = (v - hi.astype(F32)).astype(BF16)
    return hi, lo


def _dot(a, b):
    return jnp.dot(a, b, preferred_element_type=F32)


def _dot_nt(a, b):
    return lax.dot_general(a, b, (((1,), (1,)), ((), ())), preferred_element_type=F32)


def _dot_tn(a, b):
    return lax.dot_general(a, b, (((0,), (0,)), ((), ())), preferred_element_type=F32)


TM = 512
TN = 1024


def _row_matmul(name, a, a_spec, w, layer, nt, chunks, extra, extra_specs, out_shape, out_specs, epilogue, t,
                sem=("parallel",)):
    tm = _tile(t, TM)
    n_extra = len(extra)

    def body(a_ref, w_ref, *rest):
        ins, outs = rest[:n_extra], rest[n_extra:]
        for n, terms in enumerate(chunks):
            acc = None
            for plane, rows, cols in terms:
                av = (a_ref[...] if plane is None else a_ref[plane]).astype(BF16)
                part = _dot_nt(av, w_ref[rows, cols]) if nt else _dot(av, w_ref[rows, cols])
                acc = part if acc is None else acc + part
            epilogue(n, acc, ins, outs)

    w_spec = pl.BlockSpec((None,) + w.shape[1:], lambda m: (layer, 0, 0))
    return pl.pallas_call(
        body, name=name, grid=(t // tm,), in_specs=[a_spec(tm), w_spec, *extra_specs], out_specs=out_specs, out_shape=out_shape,
        compiler_params=_cp(sem),
    )(a, w, *extra)


def _rows2(width):
    return lambda tm: pl.BlockSpec((tm, width), lambda m: (m, 0))


def _col_chunks(n, width=None):
    tn = _tile(n, TN)
    return [[(None, slice(None), slice(c * tn, (c + 1) * tn))] for c in range(n // tn)], tn


def _rms_rows(x, gain):
    r = lax.rsqrt(jnp.mean(x * x, axis=-1, keepdims=True) + RMS_EPS)
    return x * r * gain


def _rms_bwd_rows(dh, x, gain, dres):
    r = lax.rsqrt(jnp.mean(x * x, axis=-1, keepdims=True) + RMS_EPS)
    xhat = x * r
    dxn = dh * gain
    dx = r * (dxn - xhat * jnp.mean(dxn * xhat, axis=-1, keepdims=True))
    return dx + dres, jnp.sum(dh * xhat, axis=0, keepdims=True)


def _rms_first(x, gain):
    t, d = x.shape
    tm = _tile(t, 512)

    def body(x_ref, g_ref, h_ref):
        h_ref[...] = _rms_rows(x_ref[...], g_ref[...]).astype(BF16)

    return pl.pallas_call(
        body, name="rms_first", grid=(t // tm,),
        in_specs=[pl.BlockSpec((tm, d), lambda m: (m, 0)), pl.BlockSpec((1, d), lambda m: (0, 0))],
        out_specs=pl.BlockSpec((tm, d), lambda m: (m, 0)), out_shape=jax.ShapeDtypeStruct((t, d), BF16),
        compiler_params=_cp(("parallel",)),
    )(x, gain)


def _qkv_proj(h, w, layer, out_dtype):
    t, d = h.shape
    chunks = [[(None, slice(None), slice(p * d, (p + 1) * d))] for p in range(3)]

    def epi(n, acc, ins, outs):
        outs[0][n] = acc.astype(out_dtype)

    return _row_matmul(
        "qkv_proj", h, _rows2(d), w, layer, False, chunks, (), (), (jax.ShapeDtypeStruct((3, t, d), out_dtype),),
        (pl.BlockSpec((3, _tile(t, TM), d), lambda m: (0, m, 0)),), epi, t)[0]


def _out_proj(name, a, w, layer, res, gain):
    t, kk = a.shape
    d = res.shape[1]

    def epi(n, acc, ins, outs):
        xn = ins[0][...] + acc
        outs[0][...] = xn
        outs[1][...] = _rms_rows(xn, ins[1][...]).astype(BF16)

    row = _rows2(d)(_tile(t, TM))
    return _row_matmul(
        name, a, _rows2(kk), w, layer, False, [[(None, slice(None), slice(None))]], (res, gain),
        (row, pl.BlockSpec((1, d), lambda m: (0, 0))),
        (jax.ShapeDtypeStruct((t, d), F32), jax.ShapeDtypeStruct((t, d), BF16)), (row, row), epi, t)


def _up_proj(h, w, layer):
    t, d = h.shape
    ff = w.shape[2]
    chunks, tn = _col_chunks(ff)

    def epi(n, acc, ins, outs):
        s = jnp.maximum(acc, 0.0)
        outs[0][:, n * tn:(n + 1) * tn] = s.astype(BF16)
        outs[1][:, n * tn:(n + 1) * tn] = (s * s).astype(BF16)

    o = _rows2(ff)(_tile(t, TM))
    return _row_matmul("up_proj", h, _rows2(d), w, layer, False, chunks, (), (), (jax.ShapeDtypeStruct((t, ff), BF16),) * 2, (o, o), epi, t)


def _down_bwd(dx, w, layer, s):
    t, d = dx.shape
    ff = w.shape[1]
    tn = _tile(ff, TN)
    chunks = [[(None, slice(c * tn, (c + 1) * tn), slice(None))] for c in range(ff // tn)]

    def epi(n, acc, ins, outs):
        cols = slice(n * tn, (n + 1) * tn)
        outs[0][:, cols] = (acc * (2.0 * ins[0][:, cols].astype(F32))).astype(BF16)

    o = _rows2(ff)(_tile(t, TM))
    return _row_matmul("down_bwd", dx, _rows2(d), w, layer, True, chunks, (s,), (o,), (jax.ShapeDtypeStruct((t, ff), BF16),), (o,), epi, t)[0]


def _norm_bwd_proj(name, a, a_spec, terms, w, layer, x, gain, dres):
    t, d = x.shape

    def epi(n, acc, ins, outs):
        dx, dg = _rms_bwd_rows(acc, ins[0][...], ins[1][...], ins[2][...])
        outs[0][...] = dx

        @pl.when(pl.program_id(0) == 0)
        def _():
            outs[1][...] = dg

        @pl.when(pl.program_id(0) > 0)
        def _():
            outs[1][...] += dg

    row = _rows2(d)(_tile(t, TM))
    vec = pl.BlockSpec((1, d), lambda m: (0, 0))
    return _row_matmul(
        name, a, a_spec, w, layer, True, [terms], (x, gain, dres), (row, vec, row),
        (jax.ShapeDtypeStruct((t, d), F32), jax.ShapeDtypeStruct((1, d), F32)), (row, vec), epi, t, sem=("arbitrary",))


def _plain_nt(name, a, w, layer, out_dtype):
    t, n = a.shape
    m_out = w.shape[1]

    def epi(n_, acc, ins, outs):
        outs[0][...] = acc.astype(out_dtype)

    return _row_matmul(
        name, a, _rows2(n), w, layer, True, [[(None, slice(None), slice(None))]], (), (),
        (jax.ShapeDtypeStruct((t, m_out), out_dtype),), (_rows2(m_out)(_tile(t, TM)),), epi, t)[0]


def _wgrad(name, a, a_spec_fn, b, b_spec_fn, t, mo, no, layer, n_layers, prev, tno=1024):
    tmo, tno, tk = _tile(mo, 1024), _tile(no, tno), _tile(t, 2048)
    nk = t // tk

    def body(a_ref, b_ref, *rest):
        o_ref = rest[-1]
        part = _dot_tn(a_ref[...].astype(BF16), b_ref[...].astype(BF16))

        @pl.when(pl.program_id(2) == 0)
        def _():
            o_ref[...] = part

        @pl.when(pl.program_id(2) > 0)
        def _():
            o_ref[...] += part

    ins = [a, b] + ([] if prev is None else [prev])
    in_specs = [a_spec_fn(tk, tmo), b_spec_fn(tk, tno)] + ([] if prev is None else [ANY])
    return pl.pallas_call(
        body, name=name, grid=(mo // tmo, no // tno, nk), in_specs=in_specs,
        out_specs=pl.BlockSpec((None, tmo, tno), lambda m, n, k: (layer, m, n)),
        out_shape=jax.ShapeDtypeStruct((n_layers, mo, no), F32),
        input_output_aliases={} if prev is None else {2: 0},
        compiler_params=_cp(("parallel", "parallel", "arbitrary")),
    )(*ins)


def _mat_spec(tk, tw):
    return pl.BlockSpec((tk, tw), lambda m, n, k: (k, m))


def _mat_spec_b(tk, tw):
    return pl.BlockSpec((tk, tw), lambda m, n, k: (k, n))


def _plane_spec_b(d):
    def fn(tk, tw):
        npp = d // tw
        return pl.BlockSpec((None, tk, tw), lambda m, n, k: (n // npp, k, n % npp))
    return fn


def _sb_masks(t):
    row = lax.broadcasted_iota(jnp.int32, (t, t), 0)
    col = lax.broadcasted_iota(jnp.int32, (t, t), 1)
    lane = lax.broadcasted_iota(jnp.int32, (1, PAIR), 1)
    return row, col, [(lane // HEAD_DIM) == h for h in (0, 1)]


def _suffix_sums(v, tri, tail, tt):
    hi, lo = _split_bf16(v)
    parts = []
    for b in reversed(range(v.shape[1] // tt)):
        cols = slice(b * tt, (b + 1) * tt)
        parts.insert(0, _dot(hi[:, cols], tri) + _dot(lo[:, cols], tri) + tail)
        tail = tail + jnp.sum(v[:, cols], axis=1, keepdims=True)
    return (parts[0] if len(parts) == 1 else jnp.concatenate(parts, axis=1)), tail


def _sb_tile(qh, kb, valid, after, carry, tt):
    z = _dot_nt(qh, kb)
    lb = -(jnp.maximum(z, 0.0) + jnp.log(1.0 + jnp.exp(-jnp.abs(z))))
    if valid is not None:
        lb = jnp.where(valid, lb, 0.0)
    between, carry = _suffix_sums(lb, after, carry, tt)
    a = jnp.exp(z + lb + between)
    if valid is not None:
        a = jnp.where(valid, a, 0.0)
    return lb, a, carry


def _sb_window_valid(i, first, tt):
    t_pos = i * tt + lax.broadcasted_iota(jnp.int32, (tt, SB_WIN * tt), 0)
    s_pos = first * tt + lax.broadcasted_iota(jnp.int32, (tt, SB_WIN * tt), 1)
    return s_pos < t_pos


def _sb_walk_on(st):
    return jnp.logical_and(st[0] >= 0, jnp.maximum(jnp.max(st[1]), jnp.max(st[2])) > SB_EXIT)


def _sb_fwd(qkv):
    _, t, d = qkv.shape
    hp, tt = d // PAIR, _tile(t, SB_T)

    def body(q_ref, k_ref, v_ref, o_ref, o32_ref):
        row, col, head = _sb_masks(tt)
        after = (row > col).astype(BF16)

        def tile(kb, vb, valid, carry, acc, qh):
            for h in (0, 1):
                _, a, carry[h] = _sb_tile(qh[h], kb, valid, after, carry[h], tt)
                acc = acc + _dot(a.astype(BF16), jnp.where(head[h], vb, jnp.zeros_like(vb)))
            return carry, acc

        states = []
        for sub in range(SB_SUB):
            i = pl.program_id(1) * SB_SUB + sub
            q = q_ref[sub * tt:(sub + 1) * tt, :]
            qh = [jnp.where(head[h], q, jnp.zeros_like(q)) * jnp.asarray(HEAD_DIM ** -0.5, BF16) for h in (0, 1)]
            first = jnp.maximum(i - (SB_WIN - 1), 0)
            rows_w = pl.ds(pl.multiple_of(first * tt, tt), SB_WIN * tt)
            zc = jnp.zeros((tt, 1), F32)
            carry, acc = tile(k_ref[rows_w, :], v_ref[rows_w, :], _sb_window_valid(i, first, tt), [zc, zc],
                              jnp.zeros((tt, PAIR), F32), qh)
            states.append((first, carry, acc, qh))

        for sub, (first, carry, acc, qh) in enumerate(states):
            def step(st, qh=qh):
                rows = pl.ds(pl.multiple_of(st[0] * tt, tt), tt)
                carry, acc = tile(k_ref[rows, :], v_ref[rows, :], None, [st[1], st[2]], st[3], qh)
                return st[0] - 1, carry[0], carry[1], acc

            o = lax.while_loop(_sb_walk_on, step, (first - 1, carry[0], carry[1], acc))[3]
            o_ref[sub * tt:(sub + 1) * tt, :] = o.astype(BF16)
            o32_ref[sub * tt:(sub + 1) * tt, :] = o

    tq = SB_SUB * tt
    qs = pl.BlockSpec((None, tq, PAIR), lambda p, i: (0, i, p))
    ks = pl.BlockSpec((None, t, PAIR), lambda p, i: (1, 0, p))
    vs = pl.BlockSpec((None, t, PAIR), lambda p, i: (2, 0, p))
    os_ = pl.BlockSpec((tq, PAIR), lambda p, i: (i, p))
    return pl.pallas_call(
        body, name="sb_fwd", grid=(hp, t // tq), in_specs=[qs, ks, vs], out_specs=(os_, os_),
        out_shape=(jax.ShapeDtypeStruct((t, d), BF16), jax.ShapeDtypeStruct((t, d), F32)),
        compiler_params=_cp(("parallel", "arbitrary")),
    )(qkv, qkv, qkv)


def _sb_bwd(qkv, o32, do):
    _, t, d = qkv.shape
    hp, tt = d // PAIR, _tile(t, SB_T)
    nq = t // (SB_SUB * tt)
    scale = HEAD_DIM ** -0.5

    def body(q_ref, k_ref, v_ref, o32_ref, do_ref, dqkv_ref, dk_acc, dv_acc):
        i = pl.program_id(1)
        row, col, head = _sb_masks(tt)
        after = (row > col).astype(BF16)
        from_s = (row >= col).astype(BF16)

        @pl.when(i == 0)
        def _():
            dk_acc[...] = jnp.zeros_like(dk_acc)
            dv_acc[...] = jnp.zeros_like(dv_acc)

        def tile(rows, valid, carry, seen, dq, qh, doh, tot):
            kb, vb = k_ref[rows, :], v_ref[rows, :]
            dk_t = dv_t = None
            for h in (0, 1):
                lb, a, carry[h] = _sb_tile(qh[h], kb, valid, after, carry[h], tt)
                ab = a.astype(BF16)
                g = ab.astype(F32) * _dot_nt(doh[h], vb)
                g_from, seen[h] = _suffix_sums(g, from_s, seen[h], tt)
                e = jnp.exp(lb)
                dz = g * e - (1.0 - e) * (tot[h] - g_from)
                if valid is not None:
                    dz = jnp.where(valid, dz, 0.0)
                dz = dz.astype(BF16)
                dq = dq + _dot(dz, jnp.where(head[h], kb, jnp.zeros_like(kb)))
                dk_h, dv_h = _dot_tn(dz, qh[h]), _dot_tn(ab, doh[h])
                dk_t = dk_h if dk_t is None else dk_t + dk_h
                dv_t = dv_h if dv_t is None else dv_t + dv_h
            dk_acc[rows, :] += dk_t
            dv_acc[rows, :] += dv_t
            return carry, seen, dq

        states = []
        for sub in range(SB_SUB):
            ii = i * SB_SUB + sub
            q = q_ref[sub * tt:(sub + 1) * tt, :]
            dob = do_ref[sub * tt:(sub + 1) * tt, :]
            prod = dob.astype(F32) * o32_ref[sub * tt:(sub + 1) * tt, :]
            zb = jnp.zeros_like(q)
            qh = [jnp.where(head[h], q, zb) * jnp.asarray(scale, BF16) for h in (0, 1)]
            doh = [jnp.where(head[h], dob, zb) for h in (0, 1)]
            tot = [jnp.sum(jnp.where(head[h], prod, 0.0), axis=1, keepdims=True) for h in (0, 1)]
            first = jnp.maximum(ii - (SB_WIN - 1), 0)
            zc = jnp.zeros((tt, 1), F32)
            carry, seen, dq = tile(pl.ds(pl.multiple_of(first * tt, tt), SB_WIN * tt), _sb_window_valid(ii, first, tt),
                                   [zc, zc], [zc, zc], jnp.zeros((tt, PAIR), F32), qh, doh, tot)
            states.append((first, carry, seen, dq, qh, doh, tot))

        for sub, (first, carry, seen, dq, qh, doh, tot) in enumerate(states):
            def step(st, qh=qh, doh=doh, tot=tot):
                carry, seen, dq = tile(pl.ds(pl.multiple_of(st[0] * tt, tt), tt), None, [st[1], st[2]], [st[3], st[4]], st[5],
                                       qh, doh, tot)
                return st[0] - 1, carry[0], carry[1], seen[0], seen[1], dq

            dq = lax.while_loop(_sb_walk_on, step, (first - 1, carry[0], carry[1], seen[0], seen[1], dq))[5]
            dqkv_ref[0, pl.ds(pl.multiple_of((i * SB_SUB + sub) * tt, tt), tt), :] = (dq * scale).astype(BF16)

        @pl.when(i == nq - 1)
        def _():
            dqkv_ref[1, :, :] = dk_acc[...].astype(BF16)
            dqkv_ref[2, :, :] = dv_acc[...].astype(BF16)

    qs = pl.BlockSpec((None, SB_SUB * tt, PAIR), lambda p, i: (0, i, p))
    ks = pl.BlockSpec((None, t, PAIR), lambda p, i: (1, 0, p))
    vs = pl.BlockSpec((None, t, PAIR), lambda p, i: (2, 0, p))
    ts = pl.BlockSpec((SB_SUB * tt, PAIR), lambda p, i: (i, p))
    return pl.pallas_call(
        body, name="sb_bwd", grid=(hp, nq), in_specs=[qs, ks, vs, ts, ts],
        out_specs=pl.BlockSpec((3, t, PAIR), lambda p, i: (0, 0, p)), out_shape=jax.ShapeDtypeStruct((3, t, d), BF16),
        scratch_shapes=[pltpu.VMEM((t, PAIR), F32), pltpu.VMEM((t, PAIR), F32)],
        compiler_params=_cp(("parallel", "arbitrary")),
    )(qkv, qkv, qkv, o32, do)


def _pair_sum_matrix():
    r = lax.broadcasted_iota(jnp.int32, (PAIR, PAIR), 0) // HEAD_DIM
    c = lax.broadcasted_iota(jnp.int32, (PAIR, PAIR), 1) // HEAD_DIM
    return (r == c).astype(BF16)


def _head_mean(v, ones):
    hi, lo = _split_bf16(v)
    return (_dot(hi, ones) + _dot(lo, ones)) * (1.0 / HEAD_DIM)


def _ca_prep(qkv32, gq, gk):
    _, t, d = qkv32.shape
    tr = _tile(t, CA_PAD)
    assert CA_PAD % tr == 0
    npad = CA_PAD // tr

    def body(x_ref, gq_ref, gk_ref, o_ref):
        p, r = pl.program_id(0), pl.program_id(1)

        @pl.when(r < npad)
        def _():
            o_ref[...] = jnp.zeros_like(o_ref)

        @pl.when(jnp.logical_and(r >= npad, p == 2))
        def _():
            o_ref[...] = x_ref[...].astype(BF16)

        @pl.when(jnp.logical_and(r >= npad, p < 2))
        def _():
            ones = _pair_sum_matrix()
            g = jnp.where(p == 0, gq_ref[...], gk_ref[...])
            for c in range(d // PAIR):
                x = x_ref[:, c * PAIR:(c + 1) * PAIR]
                rs = lax.rsqrt(_head_mean(x * x, ones) + RMS_EPS)
                o_ref[:, c * PAIR:(c + 1) * PAIR] = (x * rs * g).astype(BF16)

    vec = pl.BlockSpec((1, PAIR), lambda p, r: (0, 0))
    return pl.pallas_call(
        body, name="ca_prep", grid=(3, npad + t // tr),
        in_specs=[pl.BlockSpec((None, tr, d), lambda p, r: (p, jnp.maximum(r - npad, 0), 0)), vec, vec],
        out_specs=pl.BlockSpec((None, tr, d), lambda p, r: (p, r, 0)),
        out_shape=jax.ShapeDtypeStruct((3, CA_PAD + t, d), BF16), compiler_params=_cp(("parallel", "parallel")),
    )(qkv32, gq, gk)


def _ca_unprep(dq, dkv, qkv32, gq, gk):
    _, t, d = qkv32.shape
    tr = _tile(t, 512)
    nr = t // tr

    def body(dq_ref, dkv_ref, x_ref, gq_ref, gk_ref, o_ref, dgq_ref, dgk_ref):
        p, r = pl.program_id(0), pl.program_id(1)

        @pl.when(jnp.logical_and(p == 0, r == 0))
        def _():
            dgq_ref[...] = jnp.zeros_like(dgq_ref)
            dgk_ref[...] = jnp.zeros_like(dgk_ref)

        @pl.when(p == 2)
        def _():
            o_ref[...] = dkv_ref[...].astype(BF16)

        @pl.when(p < 2)
        def _():
            ones = _pair_sum_matrix()
            g = jnp.where(p == 0, gq_ref[...], gk_ref[...])
            dg = jnp.zeros((1, PAIR), F32)
            for c in range(d // PAIR):
                cols = slice(c * PAIR, (c + 1) * PAIR)
                x = x_ref[:, cols]
                dy = jnp.where(p == 0, dq_ref[:, cols], dkv_ref[:, cols])
                rs = lax.rsqrt(_head_mean(x * x, ones) + RMS_EPS)
                xhat = x * rs
                dxn = dy * g
                o_ref[:, cols] = (rs * (dxn - xhat * _head_mean(dxn * xhat, ones))).astype(BF16)
                dg = dg + jnp.sum(dy * xhat, axis=0, keepdims=True)
            dg = dg + pltpu.roll(dg, HEAD_DIM, 1)

            @pl.when(p == 0)
            def _():
                dgq_ref[...] += dg

            @pl.when(p == 1)
            def _():
                dgk_ref[...] += dg

    vec = pl.BlockSpec((1, PAIR), lambda p, r: (0, 0))
    return pl.pallas_call(
        body, name="ca_unprep", grid=(3, nr),
        in_specs=[pl.BlockSpec((tr, d), lambda p, r: (r, 0)),
                  pl.BlockSpec((None, tr, d), lambda p, r: (jnp.maximum(p - 1, 0), r, 0)),
                  pl.BlockSpec((None, tr, d), lambda p, r: (p, r, 0)), vec, vec],
        out_specs=(pl.BlockSpec((None, tr, d), lambda p, r: (p, r, 0)), vec, vec),
        out_shape=(jax.ShapeDtypeStruct((3, t, d), BF16), jax.ShapeDtypeStruct((1, PAIR), F32), jax.ShapeDtypeStruct((1, PAIR), F32)),
        compiler_params=_cp(("arbitrary", "arbitrary")),
    )(dq, dkv, qkv32, gq, gk)


def _ca_ext_index():
    m = np.arange(CA_EXT)
    return np.where(m <= CA_WIN, np.clip(CA_PAD - m, -MAX_REL, MAX_REL) + MAX_REL, 2 * MAX_REL).astype(np.int32)


def _ca_valid(i):
    a = lax.broadcasted_iota(jnp.int32, (CA_TQ, CA_WIN), 0)
    b = lax.broadcasted_iota(jnp.int32, (CA_TQ, CA_WIN), 1)
    ca, cb = a // CHUNK, b // CHUNK
    return jnp.logical_and(jnp.logical_and(cb >= ca, cb <= ca + LEFT_CHUNKS), b >= CA_PAD - CA_TQ * i)


def _skew(x, sign):
    row = lax.broadcasted_iota(jnp.int32, (CA_TQ, 1), 0)
    for bit in range(CA_TQ.bit_length() - 1):
        amount = (1 << bit) if sign > 0 else CA_EXT - (1 << bit)
        x = jnp.where(((row >> bit) & 1) == 1, pltpu.roll(x, amount, 1), x)
    return x


def _ca_bias_tiles(ext_ref, bias_ref):
    for h in (0, 1):
        bias_ref[h] = _skew(jnp.broadcast_to(ext_ref[pl.ds(h, 1), :], (CA_TQ, CA_EXT)), 1)


def _ca_probs(qh, kw, bias, valid):
    z = jnp.where(valid, _dot_nt(qh, kw) + bias, NEG_BIG)
    p = jnp.exp(z - jnp.max(z, axis=1, keepdims=True))
    return p / jnp.sum(p, axis=1, keepdims=True)


def _ca_fwd(qkvn, ext):
    _, tp, d = qkvn.shape
    t = tp - CA_PAD
    hp, tq = d // PAIR, CA_SUB * CA_TQ
    npad = CA_PAD // tq

    def body(q_ref, k_ref, v_ref, ext_ref, o_ref, bias_ref):
        i = pl.program_id(1)

        @pl.when(i == 0)
        def _():
            _ca_bias_tiles(ext_ref, bias_ref)

        lane = lax.broadcasted_iota(jnp.int32, (1, PAIR), 1)
        for sub in range(CA_SUB):
            ii = i * CA_SUB + sub
            valid = _ca_valid(ii)
            win = pl.ds(pl.multiple_of(ii * CA_TQ, CA_TQ), CA_WIN)
            kw, vw, q = k_ref[win, :], v_ref[win, :], q_ref[sub * CA_TQ:(sub + 1) * CA_TQ, :]
            o = jnp.zeros((CA_TQ, PAIR), F32)
            for h in (0, 1):
                hm = (lane // HEAD_DIM) == h
                qh = jnp.where(hm, q, jnp.zeros_like(q)) * jnp.asarray(HEAD_DIM ** -0.5, BF16)
                p = _ca_probs(qh, kw, bias_ref[h, :, :CA_WIN], valid)
                o = o + _dot(p.astype(BF16), jnp.where(hm, vw, jnp.zeros_like(vw)))
            o_ref[sub * CA_TQ:(sub + 1) * CA_TQ, :] = o.astype(BF16)

    return pl.pallas_call(
        body, name="ca_fwd", grid=(hp, t // tq),
        in_specs=[pl.BlockSpec((None, tq, PAIR), lambda p, i: (0, i + npad, p)),
                  pl.BlockSpec((None, tp, PAIR), lambda p, i: (1, 0, p)),
                  pl.BlockSpec((None, tp, PAIR), lambda p, i: (2, 0, p)),
                  pl.BlockSpec((None, 2, CA_EXT), lambda p, i: (p, 0, 0))],
        out_specs=pl.BlockSpec((tq, PAIR), lambda p, i: (i, p)), out_shape=jax.ShapeDtypeStruct((t, d), BF16),
        scratch_shapes=[pltpu.VMEM((2, CA_TQ, CA_EXT), F32)], compiler_params=_cp(("parallel", "arbitrary")),
    )(qkvn, qkvn, qkvn, ext)


def _ca_bwd(qkvn, ext, do):
    _, tp, d = qkvn.shape
    t = tp - CA_PAD
    hp, tq = d // PAIR, CA_SUB * CA_TQ
    npad, nq = CA_PAD // tq, t // tq
    scale = HEAD_DIM ** -0.5

    def body(q_ref, k_ref, v_ref, ext_ref, do_ref, dq_ref, dkv_ref, dext_ref, bias_ref, dbias_ref, dk_acc, dv_acc):
        i = pl.program_id(1)

        @pl.when(i == 0)
        def _():
            _ca_bias_tiles(ext_ref, bias_ref)
            dbias_ref[...] = jnp.zeros_like(dbias_ref)
            dk_acc[...] = jnp.zeros_like(dk_acc)
            dv_acc[...] = jnp.zeros_like(dv_acc)

        lane = lax.broadcasted_iota(jnp.int32, (1, PAIR), 1)
        dbias = [None, None]
        dk_u = dv_u = None

        def spread(v, sub):
            parts = [jnp.zeros((sub * CA_TQ, PAIR), F32)] * (sub > 0) + [v] + \
                    [jnp.zeros(((CA_SUB - 1 - sub) * CA_TQ, PAIR), F32)] * (sub < CA_SUB - 1)
            return jnp.concatenate(parts, axis=0) if len(parts) > 1 else v

        for sub in range(CA_SUB):
            ii = i * CA_SUB + sub
            valid = _ca_valid(ii)
            win = pl.ds(pl.multiple_of(ii * CA_TQ, CA_TQ), CA_WIN)
            rows = slice(sub * CA_TQ, (sub + 1) * CA_TQ)
            kw, vw, q, dob = k_ref[win, :], v_ref[win, :], q_ref[rows, :], do_ref[rows, :]
            dq = jnp.zeros((CA_TQ, PAIR), F32)
            dk_t = jnp.zeros((CA_WIN, PAIR), F32)
            dv_t = jnp.zeros((CA_WIN, PAIR), F32)
            for h in (0, 1):
                hm = (lane // HEAD_DIM) == h
                qh = jnp.where(hm, q, jnp.zeros_like(q)) * jnp.asarray(scale, BF16)
                doh = jnp.where(hm, dob, jnp.zeros_like(dob))
                p = _ca_probs(qh, kw, bias_ref[h, :, :CA_WIN], valid)
                dp = _dot_nt(doh, vw)
                ds = p * (dp - jnp.sum(p * dp, axis=1, keepdims=True))
                dbias[h] = ds if dbias[h] is None else dbias[h] + ds
                dsb = ds.astype(BF16)
                dq = dq + _dot(dsb, jnp.where(hm, kw, jnp.zeros_like(kw)))
                dk_t = dk_t + _dot_tn(dsb, qh)
                dv_t = dv_t + _dot_tn(p.astype(BF16), doh)
            dq_ref[rows, :] = dq * scale
            dk_u = spread(dk_t, sub) if dk_u is None else dk_u + spread(dk_t, sub)
            dv_u = spread(dv_t, sub) if dv_u is None else dv_u + spread(dv_t, sub)
        union = pl.ds(pl.multiple_of(i * tq, CA_TQ), CA_WIN + (CA_SUB - 1) * CA_TQ)
        dk_acc[union, :] += dk_u
        dv_acc[union, :] += dv_u
        for h in (0, 1):
            dbias_ref[h, :, :CA_WIN] += dbias[h]

        @pl.when(i == nq - 1)
        def _():
            dkv_ref[0, :, :] = dk_acc[CA_PAD:, :]
            dkv_ref[1, :, :] = dv_acc[CA_PAD:, :]
            for h in (0, 1):
                dext_ref[pl.ds(h, 1), :] = jnp.sum(_skew(dbias_ref[h], -1), axis=0, keepdims=True)

    es = pl.BlockSpec((None, 2, CA_EXT), lambda p, i: (p, 0, 0))
    ts = pl.BlockSpec((tq, PAIR), lambda p, i: (i, p))
    return pl.pallas_call(
        body, name="ca_bwd", grid=(hp, nq),
        in_specs=[pl.BlockSpec((None, tq, PAIR), lambda p, i: (0, i + npad, p)),
                  pl.BlockSpec((None, tp, PAIR), lambda p, i: (1, 0, p)),
                  pl.BlockSpec((None, tp, PAIR), lambda p, i: (2, 0, p)), es, ts],
        out_specs=(ts, pl.BlockSpec((2, t, PAIR), lambda p, i: (0, 0, p)), es),
        out_shape=(jax.ShapeDtypeStruct((t, d), F32), jax.ShapeDtypeStruct((2, t, d), F32), jax.ShapeDtypeStruct((hp, 2, CA_EXT), F32)),
        scratch_shapes=[pltpu.VMEM((2, CA_TQ, CA_EXT), F32), pltpu.VMEM((2, CA_TQ, CA_EXT), F32),
                        pltpu.VMEM((tp, PAIR), F32), pltpu.VMEM((tp, PAIR), F32)],
        compiler_params=_cp(("parallel", "arbitrary")),
    )(qkvn, qkvn, qkvn, ext, do)


def _loss_head(y, target):
    t, d = y.shape
    tm = _tile(t, 512)

    def body(y_ref, t_ref, dy_ref, loss_ref):
        diff = y_ref[...] - t_ref[...]
        dy_ref[...] = diff * (1.0 / d)
        part = 0.5 * jnp.sum(jnp.mean(diff * diff, axis=-1, keepdims=True), axis=0, keepdims=True)

        @pl.when(pl.program_id(0) == 0)
        def _():
            loss_ref[...] = jnp.zeros_like(loss_ref)

        loss_ref[...] += jnp.broadcast_to(part, loss_ref.shape)

    row = pl.BlockSpec((tm, d), lambda m: (m, 0))
    return pl.pallas_call(
        body, name="loss_head", grid=(t // tm,), in_specs=[row, row],
        out_specs=(row, pl.BlockSpec((8, 128), lambda m: (0, 0))),
        out_shape=(jax.ShapeDtypeStruct((t, d), F32), jax.ShapeDtypeStruct((8, 128), F32)),
        compiler_params=_cp(("arbitrary",)),
    )(y, target)


def _adamw_math(w, g, m, v):
    m = ADAM_B1 * m + (1.0 - ADAM_B1) * g
    v = ADAM_B2 * v + (1.0 - ADAM_B2) * (g * g)
    m_hat = m / (1.0 - ADAM_B1 ** ADAM_STEP)
    v_hat = v / (1.0 - ADAM_B2 ** ADAM_STEP)
    delta = -ADAM_LR * (m_hat / (jnp.sqrt(v_hat) + ADAM_EPS) + ADAM_WD * w)
    return delta, m, v


def _adamw(name, w, g, m, v):
    shape = w.shape
    cols = shape[-1]
    rows = int(np.prod(shape[:-1]))
    tr = _tile(rows, 512)
    flat = [a.reshape(rows, cols) for a in (w, g, m, v)]

    def body(w_ref, g_ref, m_ref, v_ref, d_ref, nm_ref, nv_ref):
        d_ref[...], nm_ref[...], nv_ref[...] = _adamw_math(w_ref[...], g_ref[...], m_ref[...], v_ref[...])

    blk = pl.BlockSpec((tr, cols), lambda r: (r, 0))
    outs = pl.pallas_call(
        body, name=name, grid=(rows // tr,), in_specs=[blk] * 4, out_specs=(blk,) * 3,
        out_shape=(jax.ShapeDtypeStruct((rows, cols), F32),) * 3, compiler_params=_cp(("parallel",)),
    )(*flat)
    return [o.reshape(shape) for o in outs]


def _place():
    x, y, c = lax.axis_index("x"), lax.axis_index("y"), lax.axis_index("c")
    chips = [(1 - x, y), (x, 1 - y), (1 - x, 1 - y)]
    return x, y, c, chips


def _shard_slab(ref, kind, layer0, n_layers, shard, width):
    lay = pl.ds(layer0, n_layers)
    if kind == "cols":
        return ref.at[lay, :, pl.ds(shard * width, width)]
    return ref.at[lay, pl.ds(shard * width, width), :]


def _cast_into_place(w, kind, shard):
    n_layers, rows, cols = w.shape
    tr = _tile(rows, 512)
    nr = rows // tr

    def body(s_ref, w_ref, o_ref):
        o_ref[...] = w_ref[...].astype(BF16)

    if kind == "cols":
        full, out = (n_layers, rows, cols * N_CHIPS), pl.BlockSpec((None, tr, cols), lambda l, r, s_ref: (l, r, s_ref[0]))
    else:
        full, out = (n_layers, rows * N_CHIPS, cols), pl.BlockSpec((None, tr, cols), lambda l, r, s_ref: (l, s_ref[0] * nr + r, 0))
    grid_spec = pltpu.PrefetchScalarGridSpec(
        num_scalar_prefetch=1, grid=(n_layers, nr),
        in_specs=[pl.BlockSpec((None, tr, cols), lambda l, r, s_ref: (l, r, 0))], out_specs=out)
    return pl.pallas_call(body, name="cast_into_place", grid_spec=grid_spec, out_shape=jax.ShapeDtypeStruct(full, BF16),
                          compiler_params=_cp(("parallel", "parallel")))(shard, w)


def _gather_weights(fulls, kinds, widths):
    lh = fulls[0].shape[0] // 2
    na = len(fulls)

    def body(*refs):
        dst = refs[na:2 * na]
        send_sems, recv_sems = refs[2 * na:]
        x, y, c, chips = _place()
        mine = 2 * x + y
        sibling = (x, y, 1 - c)

        def slab(a, half, shard):
            return _shard_slab(dst[a], kinds[a], half * lh, lh, shard, widths[a])

        def remote(k, a, half, shard, to):
            return pltpu.make_async_remote_copy(
                src_ref=slab(a, half, shard), dst_ref=slab(a, half, shard),
                send_sem=send_sems.at[k], recv_sem=recv_sems.at[k], device_id=to, device_id_type=MESH)

        sends = []
        for j, chip in enumerate(chips):
            for a in range(na):
                sends.append(remote(j * na + a, a, c, mine, (*chip, c)))
        for cp in sends:
            cp.start()
        passed = []
        for j, (cx, cy) in enumerate(chips):
            for a in range(na):
                k = j * na + a
                remote(k, a, c, 2 * cx + cy, (x, y, c)).wait_recv()
                fwd = remote(3 * na + k, a, c, 2 * cx + cy, sibling)
                fwd.start()
                passed.append(fwd)
        for j, (cx, cy) in enumerate(chips):
            for a in range(na):
                remote(3 * na + j * na + a, a, 1 - c, 2 * cx + cy, (x, y, c)).wait_recv()
        for cp in sends + passed:
            cp.wait_send()

    n_sem = 6 * na
    return pl.pallas_call(
        body, name="gather_weights", in_specs=[ANY] * na, out_specs=[ANY] * na,
        out_shape=[jax.ShapeDtypeStruct(f.shape, f.dtype) for f in fulls], input_output_aliases={a: a for a in range(na)},
        scratch_shapes=[pltpu.SemaphoreType.DMA((n_sem,)), pltpu.SemaphoreType.DMA((n_sem,))],
    )(*fulls)


def _pair_exchange(grads):
    n_layers = grads[0].shape[0]
    lh = n_layers // 2
    na = len(grads)

    def body(*refs):
        src, dst = refs[:na], refs[na:2 * na]
        send_sems, recv_sems = refs[2 * na:]
        x, y, c, _ = _place()
        cps = [pltpu.make_async_remote_copy(src_ref=src[a].at[pl.ds((1 - c) * lh, lh)], dst_ref=dst[a], send_sem=send_sems.at[a],
                                            recv_sem=recv_sems.at[a], device_id=(x, y, 1 - c), device_id_type=MESH) for a in range(na)]
        for cp in cps:
            cp.start()
        for cp in cps:
            cp.wait()

    return pl.pallas_call(
        body, name="pair_exchange", in_specs=[ANY] * na, out_specs=[ANY] * na,
        out_shape=[jax.ShapeDtypeStruct((lh,) + g.shape[1:], g.dtype) for g in grads],
        scratch_shapes=[pltpu.SemaphoreType.DMA((na,)), pltpu.SemaphoreType.DMA((na,))],
    )(*grads)


def _add_half(name, full, other, c):
    n_layers, rows, cols = full.shape
    lh = n_layers // 2
    tr = _tile(rows, 512)
    nr = rows // tr

    def body(c_ref, a_ref, b_ref, o_ref):
        o_ref[...] = (a_ref[...] + b_ref[...]).astype(BF16)

    grid_spec = pltpu.PrefetchScalarGridSpec(
        num_scalar_prefetch=1, grid=(lh, nr),
        in_specs=[pl.BlockSpec((None, tr, cols), lambda l, r, c_ref: (c_ref[0] * lh + l, r, 0)),
                  pl.BlockSpec((None, tr, cols), lambda l, r, c_ref: (l, r, 0))],
        out_specs=pl.BlockSpec((None, tr, cols), lambda l, r, c_ref: (l, r, 0)))
    return pl.pallas_call(body, name=name, grid_spec=grid_spec, out_shape=jax.ShapeDtypeStruct((lh, rows, cols), BF16),
                          compiler_params=_cp(("parallel", "parallel")))(c, full, other)


def _chip_exchange(partials, kinds, widths):
    lh = partials[0].shape[0]
    na = len(partials)
    out_shapes = []
    for p, k, w in zip(partials, kinds, widths):
        shp = (lh, p.shape[1], w) if k == "cols" else (lh, w, p.shape[2])
        out_shapes.append(jax.ShapeDtypeStruct((3,) + shp, p.dtype))

    def body(*refs):
        src, dst = refs[:na], refs[na:2 * na]
        send_sems, recv_sems = refs[2 * na:]
        x, y, c, chips = _place()
        cps = []
        for j, (cx, cy) in enumerate(chips):
            for a in range(na):
                k = j * na + a
                cps.append(pltpu.make_async_remote_copy(
                    src_ref=_shard_slab(src[a], kinds[a], 0, lh, 2 * cx + cy, widths[a]), dst_ref=dst[a].at[j],
                    send_sem=send_sems.at[k], recv_sem=recv_sems.at[k], device_id=(cx, cy, c), device_id_type=MESH))
        for cp in cps:
            cp.start()
        for cp in cps:
            cp.wait()

    return pl.pallas_call(
        body, name="chip_exchange", in_specs=[ANY] * na, out_specs=[ANY] * na, out_shape=out_shapes,
        scratch_shapes=[pltpu.SemaphoreType.DMA((3 * na,)), pltpu.SemaphoreType.DMA((3 * na,))],
    )(*partials)


def _add_chips(name, partial, got, kind, width, shard_core):
    lh = partial.shape[0]
    _, _, rows, cols = got.shape
    tr = _tile(rows, 512)
    nr = rows // tr

    def body(s_ref, a_ref, b_ref, o_ref):
        o_ref[...] = ((a_ref[...].astype(F32) + b_ref[0].astype(F32)) + b_ref[1].astype(F32)) + b_ref[2].astype(F32)

    if kind == "cols":
        own = pl.BlockSpec((None, tr, width), lambda l, r, s_ref: (l, r, s_ref[0]))
    else:
        own = pl.BlockSpec((None, tr, cols), lambda l, r, s_ref: (l, s_ref[0] * nr + r, 0))
    grid_spec = pltpu.PrefetchScalarGridSpec(
        num_scalar_prefetch=1, grid=(lh, nr),
        in_specs=[own, pl.BlockSpec((3, None, tr, cols), lambda l, r, s_ref: (0, l, r, 0))],
        out_specs=pl.BlockSpec((None, tr, cols), lambda l, r, s_ref: (s_ref[1] * lh + l, r, 0)))
    return pl.pallas_call(body, name=name, grid_spec=grid_spec, out_shape=jax.ShapeDtypeStruct((2 * lh, rows, cols), F32),
                          compiler_params=_cp(("parallel", "parallel")))(shard_core, partial, got)


def _pair_share(sums):
    lh = sums[0].shape[0] // 2
    na = len(sums)

    def body(*refs):
        dst = refs[na:2 * na]
        send_sems, recv_sems = refs[2 * na:]
        x, y, c, _ = _place()

        def swap(a, half):
            return pltpu.make_async_remote_copy(
                src_ref=dst[a].at[pl.ds(half * lh, lh)], dst_ref=dst[a].at[pl.ds(half * lh, lh)], send_sem=send_sems.at[a],
                recv_sem=recv_sems.at[a], device_id=(x, y, 1 - c), device_id_type=MESH)

        for a in range(na):
            swap(a, c).start()
        for a in range(na):
            swap(a, c).wait_send()
            swap(a, 1 - c).wait_recv()

    return pl.pallas_call(
        body, name="pair_share", in_specs=[ANY] * na, out_specs=[ANY] * na,
        out_shape=[jax.ShapeDtypeStruct(s.shape, s.dtype) for s in sums], input_output_aliases={a: a for a in range(na)},
        scratch_shapes=[pltpu.SemaphoreType.DMA((na,)), pltpu.SemaphoreType.DMA((na,))],
    )(*sums)


def _small_step(g_part, w, m, v):
    r = g_part.shape[0]

    def body(g_ref, w_ref, m_ref, v_ref, go_ref, d_ref, nm_ref, nv_ref, all_ref, send_sems, recv_sems):
        x, y, c, _ = _place()
        me = 4 * x + 2 * y + c
        all_ref[me] = g_ref[...]
        cps = []
        for k in range(1, N_DEV):
            px, py, pc = (x + (k >> 2)) % 2, (y + ((k >> 1) & 1)) % 2, (c + (k & 1)) % 2
            cps.append(pltpu.make_async_remote_copy(src_ref=g_ref, dst_ref=all_ref.at[me], send_sem=send_sems.at[k - 1],
                                                    recv_sem=recv_sems.at[k - 1], device_id=(px, py, pc), device_id_type=MESH))
        for cp in cps:
            cp.start()
        for cp in cps:
            cp.wait()
        g = all_ref[0]
        for k in range(1, N_DEV):
            g = g + all_ref[k]
        go_ref[...] = g
        d_ref[...], nm_ref[...], nv_ref[...] = _adamw_math(w_ref[...], g, m_ref[...], v_ref[...])

    vm = pl.BlockSpec(memory_space=pltpu.VMEM)
    return pl.pallas_call(
        body, name="small_step", in_specs=[vm] * 4, out_specs=[vm] * 4, out_shape=[jax.ShapeDtypeStruct((r, 128), F32)] * 4,
        scratch_shapes=[pltpu.VMEM((N_DEV, r, 128), F32), pltpu.SemaphoreType.DMA((N_DEV - 1,)), pltpu.SemaphoreType.DMA((N_DEV - 1,))],
    )(g_part, w, m, v)


def _onehot_mm(name, a, onehot):
    def body(a_ref, oh_ref, o_ref):
        v = a_ref[...]
        oh = oh_ref[...]
        hi, lo = _split_bf16(v)
        lo2 = (v - hi.astype(F32) - lo.astype(F32)).astype(BF16)
        o_ref[...] = _dot(hi, oh) + _dot(lo, oh) + _dot(lo2, oh)

    return pl.pallas_call(body, name=name, out_shape=jax.ShapeDtypeStruct((a.shape[0], onehot.shape[1]), F32))(a, onehot)


def _pack_small(parts):
    flat = jnp.concatenate([p.reshape(-1) for p in parts])
    n = flat.shape[0]
    rows = -(-n // 128)
    rows = -(-rows // 8) * 8
    return jnp.pad(flat, (0, rows * 128 - n)).reshape(rows, 128)


def _unpack_small(packed, like):
    flat = packed.reshape(-1)
    out, off = [], 0
    for p in like:
        out.append(flat[off:off + p.size].reshape(p.shape))
        off += p.size
    return out


def kernel(x, mix_norm, w_qkv, w_o, q_norm, k_norm, rel_bias, ffn_norm, w_up, w_down, loss_target, m_mix_norm, m_w_qkv, m_w_o, m_q_norm, m_k_norm, m_rel_bias, m_ffn_norm, m_w_up, m_w_down, v_mix_norm, v_w_qkv, v_w_o, v_q_norm, v_k_norm, v_rel_bias, v_ffn_norm, v_w_up, v_w_down):
    n_layers, d = mix_norm.shape
    t = x.shape[1]
    ff = w_down.shape[1] * N_CHIPS
    heads = d // HEAD_DIM
    cx, cy, cc = lax.axis_index("x"), lax.axis_index("y"), lax.axis_index("c")
    shard = (2 * cx + cy).astype(jnp.int32).reshape(1)
    core = cc.astype(jnp.int32).reshape(1)

    big = [w_qkv, w_o, w_up, w_down]
    kinds = ["cols", "rows", "cols", "rows"]
    widths = [w_qkv.shape[2], w_o.shape[1], w_up.shape[2], w_down.shape[1]]
    wq, wo, wu, wd = _gather_weights([_cast_into_place(w, k, shard) for w, k in zip(big, kinds)], kinds, widths)

    rel_pad = -(-N_REL // PAIR) * PAIR
    ext_hot = _ca_ext_index()[:, None] == np.arange(rel_pad)[None, :]
    fold_hot, spread_hot = jnp.asarray(ext_hot, BF16), jnp.asarray(ext_hot.T, BF16)
    rel_tab = jnp.pad(rel_bias, ((0, 0), (0, 0), (0, rel_pad - N_REL)))
    exts = [_onehot_mm("relbias_spread", rel_tab[i], spread_hot).reshape(heads // 2, 2, CA_EXT) for i in range(n_layers // 2)]

    xs, h1s, qkvs, os_, o32s, x2s, h2s, ss, us, q32s = [], [], [], [], [], [], [], [], [], []
    xc = x[0]
    h = _rms_first(xc, mix_norm[0:1])
    for layer in range(n_layers):
        xs.append(xc)
        h1s.append(h)
        if layer % 2 == 0:
            qkv = _qkv_proj(h, wq, layer, BF16)
            o, o32 = _sb_fwd(qkv)
            q32s.append(None)
        else:
            idx = layer // 2
            q32 = _qkv_proj(h, wq, layer, F32)
            gq = jnp.tile(q_norm[idx], 2).reshape(1, PAIR)
            gk = jnp.tile(k_norm[idx], 2).reshape(1, PAIR)
            qkv = _ca_prep(q32, gq, gk)
            o, o32 = _ca_fwd(qkv, exts[idx]), None
            q32s.append(q32)
        qkvs.append(qkv)
        os_.append(o)
        o32s.append(o32)
        x2, h2 = _out_proj("attn_out", o, wo, layer, xc, ffn_norm[layer:layer + 1])
        s, u = _up_proj(h2, wu, layer)
        nxt = mix_norm[layer + 1:layer + 2] if layer + 1 < n_layers else mix_norm[0:1]
        xc, h = _out_proj("mlp_out", u, wd, layer, x2, nxt)
        x2s.append(x2)
        h2s.append(h2)
        ss.append(s)
        us.append(u)

    dx, loss_part = _loss_head(xc, loss_target[0])
    loss = lax.psum(loss_part[0, 0], ("x", "y", "c"))

    g_qkv = g_o = g_up = g_down = None
    d_mix, d_ffn = [None] * n_layers, [None] * n_layers
    d_qn, d_kn, d_rb = [], [], []
    for layer in reversed(range(n_layers)):
        du = _down_bwd(dx, wd, layer, ss[layer])
        g_down = _wgrad("wgrad_down", us[layer], _mat_spec, dx, _mat_spec_b, t, ff, d, layer, n_layers, g_down)
        g_up = _wgrad("wgrad_up", h2s[layer], _mat_spec, du, _mat_spec_b, t, d, ff, layer, n_layers, g_up)
        dx2, d_ffn[layer] = _norm_bwd_proj(
            "up_bwd", du, _rows2(ff), [(None, slice(None), slice(None))], wu, layer, x2s[layer], ffn_norm[layer:layer + 1], dx)
        do = _plain_nt("attn_out_bwd", dx2, wo, layer, BF16)
        g_o = _wgrad("wgrad_o", os_[layer], _mat_spec, dx2, _mat_spec_b, t, d, d, layer, n_layers, g_o)
        if layer % 2 == 0:
            dqkv = _sb_bwd(qkvs[layer], o32s[layer], do)
        else:
            idx = layer // 2
            gq = jnp.tile(q_norm[idx], 2).reshape(1, PAIR)
            gk = jnp.tile(k_norm[idx], 2).reshape(1, PAIR)
            dqn, dkv, dext = _ca_bwd(qkvs[layer], exts[idx], do)
            dqkv, dgq, dgk = _ca_unprep(dqn, dkv, q32s[layer], gq, gk)
            d_qn.insert(0, dgq[0, :HEAD_DIM])
            d_kn.insert(0, dgk[0, :HEAD_DIM])
            d_rb.insert(0, _onehot_mm("relbias_fold", dext.reshape(heads, CA_EXT), fold_hot)[:, :N_REL])
        g_qkv = _wgrad("wgrad_qkv", h1s[layer], _mat_spec, dqkv, _plane_spec_b(d), t, d, 3 * d, layer, n_layers, g_qkv,
                       tno=_tile(d, 1024))
        dx, d_mix[layer] = _norm_bwd_proj(
            "qkv_bwd", dqkv, lambda tm: pl.BlockSpec((3, tm, d), lambda m: (0, m, 0)),
            [(p, slice(None), slice(p * d, (p + 1) * d)) for p in range(3)], wq, layer, xs[layer], mix_norm[layer:layer + 1], dx2)
    grad_x = dx.reshape(x.shape)

    grads = [g_qkv, g_o, g_up, g_down]
    other = _pair_exchange(grads)
    partial = [_add_half("add_pair", g, o_, core) for g, o_ in zip(grads, other)]
    got = _chip_exchange(partial, kinds, widths)
    shard_core = jnp.concatenate([shard, core])
    g_big = _pair_share([_add_chips("add_chips", p, r_, k, w_, shard_core) for p, r_, k, w_ in zip(partial, got, kinds, widths)])

    upd = [_adamw("adamw", w_, g_, m_, v_) for w_, g_, m_, v_ in
           zip(big, g_big, [m_w_qkv, m_w_o, m_w_up, m_w_down], [v_w_qkv, v_w_o, v_w_up, v_w_down])]

    small_w = [mix_norm, q_norm, k_norm, rel_bias, ffn_norm]
    small_g = [jnp.concatenate(d_mix, 0), jnp.stack(d_qn), jnp.stack(d_kn), jnp.stack(d_rb), jnp.concatenate(d_ffn, 0)]
    packed = _small_step(_pack_small(small_g), _pack_small(small_w),
                         _pack_small([m_mix_norm, m_q_norm, m_k_norm, m_rel_bias, m_ffn_norm]),
                         _pack_small([v_mix_norm, v_q_norm, v_k_norm, v_rel_bias, v_ffn_norm]))
    sg, sd, sm, sv = [_unpack_small(p, small_w) for p in packed]

    def order(small, bigs):
        return [small[0], bigs[0], bigs[1], small[1], small[2], small[3], small[4], bigs[2], bigs[3]]

    return (loss, grad_x, *order(sg, g_big), *order(sd, [u_[0] for u_ in upd]),
            *order(sm, [u_[1] for u_ in upd]), *order(sv, [u_[2] for u_ in upd]))
```

```python
import functools

import numpy as np
import jax
import jax.numpy as jnp
from jax import lax
from jax.experimental import pallas as pl
from jax.experimental.pallas import tpu as pltpu

F32, BF16 = jnp.float32, jnp.bfloat16
MESH = pl.DeviceIdType.MESH

HEAD_DIM = 64
PAIR = 128
CHUNK = 64
LEFT_CHUNKS = 8
MAX_REL = 256
N_REL = 2 * MAX_REL + 1
CA_TQ = 128
CA_SUB = 2
CA_PAD = LEFT_CHUNKS * CHUNK
CA_WIN = CA_PAD + CA_TQ
CA_EXT = CA_WIN + CA_TQ
SB_T = 128
SB_WIN = 3
SB_SUB = 2
RMS_EPS = 1e-6
SB_EXIT = -104.0
NEG_BIG = -1e30
ADAM_LR, ADAM_B1, ADAM_B2, ADAM_EPS, ADAM_WD, ADAM_STEP = 0.001, 0.9, 0.999, 1e-08, 0.01, 10
VMEM_LIMIT = 56 << 20
N_CHIPS = 4
N_DEV = 8
ANY = pl.BlockSpec(memory_space=pl.ANY)


def _cp(sem=None, **kw):
    return pltpu.CompilerParams(dimension_semantics=sem, vmem_limit_bytes=VMEM_LIMIT, **kw)


def _tile(n, want):
    t = min(n, want)
    assert n % t == 0, (n, t)
    return t


class _Side:
    def __init__(self, ins, aliased, new_outs, n_send, n_recv, start, mid, finish):
        self.ins, self.aliased, self.new_outs = list(ins), list(aliased), list(new_outs)
        self.n_send, self.n_recv, self.start, self.mid, self.finish = n_send, n_recv, start, mid, finish

    def out_shapes(self):
        return [jax.ShapeDtypeStruct(a.shape, a.dtype) for a, al in zip(self.ins, self.aliased) if al] + self.new_outs


def _side_call(name, body, grid, in_specs, out_specs, out_shape, scratch_shapes, args, side, sem):
    out_shape, out_specs = list(out_shape), list(out_specs)
    if side is None:
        outs = pl.pallas_call(body, name=name, grid=grid, in_specs=in_specs, out_specs=out_specs, out_shape=out_shape,
                              scratch_shapes=scratch_shapes, compiler_params=_cp(sem))(*args)
        return list(outs), []
    n_in, n_out, n_sc, n_sin = len(in_specs), len(out_shape), len(scratch_shapes), len(side.ins)
    s_outs = side.out_shapes()

    def wrapped(*refs):
        ins, s_in = refs[:n_in], refs[n_in:n_in + n_sin]
        outs = refs[n_in + n_sin:n_in + n_sin + n_out]
        s_out = refs[n_in + n_sin + n_out:n_in + n_sin + n_out + len(s_outs)]
        scr = refs[n_in + n_sin + n_out + len(s_outs):]
        send, recv = scr[n_sc], scr[n_sc + 1]
        p, i = pl.program_id(0), pl.program_id(1)

        @pl.when(jnp.logical_and(p == 0, i == 0))
        def _():
            side.start(s_in, s_out, send, recv)

        body(*ins, *outs, *scr[:n_sc])

        @pl.when(jnp.logical_and(p == grid[0] // 2, i == 0))
        def _():
            side.mid(s_in, s_out, send, recv)

        @pl.when(jnp.logical_and(p == grid[0] - 1, i == grid[1] - 1))
        def _():
            side.finish(s_in, s_out, send, recv)

    aliases, pos = {}, n_out
    for k, al in enumerate(side.aliased):
        if al:
            aliases[n_in + k] = pos
            pos += 1
    outs = pl.pallas_call(
        wrapped, name=name, grid=grid, in_specs=list(in_specs) + [ANY] * n_sin, out_specs=out_specs + [ANY] * len(s_outs),
        out_shape=out_shape + s_outs, input_output_aliases=aliases,
        scratch_shapes=list(scratch_shapes) + [pltpu.SemaphoreType.DMA((side.n_send,)), pltpu.SemaphoreType.DMA((side.n_recv,))],
        compiler_params=_cp(("arbitrary", "arbitrary")),
    )(*args, *side.ins)
    return list(outs[:n_out]), list(outs[n_out:])


def _blocking(name, side):
    s_outs = side.out_shapes()
    n_sin = len(side.ins)

    def body(*refs):
        s_in, s_out = refs[:n_sin], refs[n_sin:n_sin + len(s_outs)]
        send, recv = refs[n_sin + len(s_outs):]
        side.start(s_in, s_out, send, recv)
        side.mid(s_in, s_out, send, recv)
        side.finish(s_in, s_out, send, recv)

    aliases, pos = {}, 0
    for k, al in enumerate(side.aliased):
        if al:
            aliases[k] = pos
            pos += 1
    return list(pl.pallas_call(
        body, name=name, in_specs=[ANY] * n_sin, out_specs=[ANY] * len(s_outs), out_shape=s_outs, input_output_aliases=aliases,
        scratch_shapes=[pltpu.SemaphoreType.DMA((side.n_send,)), pltpu.SemaphoreType.DMA((side.n_recv,))],
    )(*side.ins))


def _split_bf16(v):
    hi = v.astype(BF16)
    lo = (v - hi.astype(F32)).astype(BF16)
    return hi, lo


def _dot(a, b):
    return jnp.dot(a, b, preferred_element_type=F32)


def _dot_nt(a, b):
    return lax.dot_general(a, b, (((1,), (1,)), ((), ())), preferred_element_type=F32)


def _dot_tn(a, b):
    return lax.dot_general(a, b, (((0,), (0,)), ((), ())), preferred_element_type=F32)


TM = 512
TN = 1024


def _row_matmul(name, a, a_spec, w, layer, nt, chunks, extra, extra_specs, out_shape, out_specs, epilogue, t,
                sem=("parallel",)):
    tm = _tile(t, TM)
    n_extra = len(extra)

    def body(a_ref, w_ref, *rest):
        ins, outs = rest[:n_extra], rest[n_extra:]
        for n, terms in enumerate(chunks):
            acc = None
            for plane, rows, cols in terms:
                av = (a_ref[...] if plane is None else a_ref[plane]).astype(BF16)
                part = _dot_nt(av, w_ref[rows, cols]) if nt else _dot(av, w_ref[rows, cols])
                acc = part if acc is None else acc + part
            epilogue(n, acc, ins, outs)

    w_spec = pl.BlockSpec((None,) + w.shape[1:], lambda m: (layer, 0, 0))
    return pl.pallas_call(
        body, name=name, grid=(t // tm,), in_specs=[a_spec(tm), w_spec, *extra_specs], out_specs=out_specs, out_shape=out_shape,
        compiler_params=_cp(sem),
    )(a, w, *extra)


def _rows2(width):
    return lambda tm: pl.BlockSpec((tm, width), lambda m: (m, 0))


def _col_chunks(n, width=None):
    tn = _tile(n, TN)
    return [[(None, slice(None), slice(c * tn, (c + 1) * tn))] for c in range(n // tn)], tn


def _rms_rows(x, gain):
    r = lax.rsqrt(jnp.mean(x * x, axis=-1, keepdims=True) + RMS_EPS)
    return x * r * gain


def _rms_bwd_rows(dh, x, gain, dres):
    r = lax.rsqrt(jnp.mean(x * x, axis=-1, keepdims=True) + RMS_EPS)
    xhat = x * r
    dxn = dh * gain
    dx = r * (dxn - xhat * jnp.mean(dxn * xhat, axis=-1, keepdims=True))
    return dx + dres, jnp.sum(dh * xhat, axis=0, keepdims=True)


def _rms_first(x, gain):
    t, d = x.shape
    tm = _tile(t, 512)

    def body(x_ref, g_ref, h_ref):
        h_ref[...] = _rms_rows(x_ref[...], g_ref[...]).astype(BF16)

    return pl.pallas_call(
        body, name="rms_first", grid=(t // tm,),
        in_specs=[pl.BlockSpec((tm, d), lambda m: (m, 0)), pl.BlockSpec((1, d), lambda m: (0, 0))],
        out_specs=pl.BlockSpec((tm, d), lambda m: (m, 0)), out_shape=jax.ShapeDtypeStruct((t, d), BF16),
        compiler_params=_cp(("parallel",)),
    )(x, gain)


def _qkv_proj(h, w, layer, out_dtype):
    t, d = h.shape
    chunks = [[(None, slice(None), slice(p * d, (p + 1) * d))] for p in range(3)]

    def epi(n, acc, ins, outs):
        outs[0][n] = acc.astype(out_dtype)

    return _row_matmul(
        "qkv_proj", h, _rows2(d), w, layer, False, chunks, (), (), (jax.ShapeDtypeStruct((3, t, d), out_dtype),),
        (pl.BlockSpec((3, _tile(t, TM), d), lambda m: (0, m, 0)),), epi, t)[0]


def _out_proj(name, a, w, layer, res, gain):
    t, kk = a.shape
    d = res.shape[1]

    def epi(n, acc, ins, outs):
        xn = ins[0][...] + acc
        outs[0][...] = xn
        outs[1][...] = _rms_rows(xn, ins[1][...]).astype(BF16)

    row = _rows2(d)(_tile(t, TM))
    return _row_matmul(
        name, a, _rows2(kk), w, layer, False, [[(None, slice(None), slice(None))]], (res, gain),
        (row, pl.BlockSpec((1, d), lambda m: (0, 0))),
        (jax.ShapeDtypeStruct((t, d), F32), jax.ShapeDtypeStruct((t, d), BF16)), (row, row), epi, t)


def _up_proj(h, w, layer):
    t, d = h.shape
    ff = w.shape[2]
    chunks, tn = _col_chunks(ff)

    def epi(n, acc, ins, outs):
        s = jnp.maximum(acc, 0.0)
        outs[0][:, n * tn:(n + 1) * tn] = s.astype(BF16)
        outs[1][:, n * tn:(n + 1) * tn] = (s * s).astype(BF16)

    o = _rows2(ff)(_tile(t, TM))
    return _row_matmul("up_proj", h, _rows2(d), w, layer, False, chunks, (), (), (jax.ShapeDtypeStruct((t, ff), BF16),) * 2, (o, o), epi, t)


def _down_bwd(dx, w, layer, s):
    t, d = dx.shape
    ff = w.shape[1]
    tn = _tile(ff, TN)
    chunks = [[(None, slice(c * tn, (c + 1) * tn), slice(None))] for c in range(ff // tn)]

    def epi(n, acc, ins, outs):
        cols = slice(n * tn, (n + 1) * tn)
        outs[0][:, cols] = (acc * (2.0 * ins[0][:, cols].astype(F32))).astype(BF16)

    o = _rows2(ff)(_tile(t, TM))
    return _row_matmul("down_bwd", dx, _rows2(d), w, layer, True, chunks, (s,), (o,), (jax.ShapeDtypeStruct((t, ff), BF16),), (o,), epi, t)[0]


def _norm_bwd_proj(name, a, a_spec, terms, w, layer, x, gain, dres):
    t, d = x.shape

    def epi(n, acc, ins, outs):
        dx, dg = _rms_bwd_rows(acc, ins[0][...], ins[1][...], ins[2][...])
        outs[0][...] = dx

        @pl.when(pl.program_id(0) == 0)
        def _():
            outs[1][...] = dg

        @pl.when(pl.program_id(0) > 0)
        def _():
            outs[1][...] += dg

    row = _rows2(d)(_tile(t, TM))
    vec = pl.BlockSpec((1, d), lambda m: (0, 0))
    return _row_matmul(
        name, a, a_spec, w, layer, True, [terms], (x, gain, dres), (row, vec, row),
        (jax.ShapeDtypeStruct((t, d), F32), jax.ShapeDtypeStruct((1, d), F32)), (row, vec), epi, t, sem=("arbitrary",))


def _plain_nt(name, a, w, layer, out_dtype):
    t, n = a.shape
    m_out = w.shape[1]

    def epi(n_, acc, ins, outs):
        outs[0][...] = acc.astype(out_dtype)

    return _row_matmul(
        name, a, _rows2(n), w, layer, True, [[(None, slice(None), slice(None))]], (), (),
        (jax.ShapeDtypeStruct((t, m_out), out_dtype),), (_rows2(m_out)(_tile(t, TM)),), epi, t)[0]


def _wgrad(name, a, a_spec_fn, b, b_spec_fn, t, mo, no, tno=1024):
    tmo, tno, tk = _tile(mo, 1024), _tile(no, tno), _tile(t, 2048)
    nk = t // tk

    def body(a_ref, b_ref, o_ref, acc_ref):
        part = _dot_tn(a_ref[...].astype(BF16), b_ref[...].astype(BF16))
        k = pl.program_id(2)

        @pl.when(k == 0)
        def _():
            acc_ref[...] = part

        @pl.when(k > 0)
        def _():
            acc_ref[...] += part

        @pl.when(k == nk - 1)
        def _():
            o_ref[...] = acc_ref[...].astype(BF16)

    return pl.pallas_call(
        body, name=name, grid=(mo // tmo, no // tno, nk), in_specs=[a_spec_fn(tk, tmo), b_spec_fn(tk, tno)],
        out_specs=pl.BlockSpec((tmo, tno), lambda m, n, k: (m, n)), out_shape=jax.ShapeDtypeStruct((mo, no), BF16),
        scratch_shapes=[pltpu.VMEM((tmo, tno), F32)], compiler_params=_cp(("parallel", "parallel", "arbitrary")),
    )(a, b)


def _mat_spec(tk, tw):
    return pl.BlockSpec((tk, tw), lambda m, n, k: (k, m))


def _mat_spec_b(tk, tw):
    return pl.BlockSpec((tk, tw), lambda m, n, k: (k, n))


def _plane_spec_b(d):
    def fn(tk, tw):
        npp = d // tw
        return pl.BlockSpec((None, tk, tw), lambda m, n, k: (n // npp, k, n % npp))
    return fn


def _sb_masks(t):
    row = lax.broadcasted_iota(jnp.int32, (t, t), 0)
    col = lax.broadcasted_iota(jnp.int32, (t, t), 1)
    lane = lax.broadcasted_iota(jnp.int32, (1, PAIR), 1)
    return row, col, [(lane // HEAD_DIM) == h for h in (0, 1)]


def _suffix_sums(v, tri, tail, tt):
    hi, lo = _split_bf16(v)
    parts = []
    for b in reversed(range(v.shape[1] // tt)):
        cols = slice(b * tt, (b + 1) * tt)
        parts.insert(0, _dot(hi[:, cols], tri) + _dot(lo[:, cols], tri) + tail)
        tail = tail + jnp.sum(v[:, cols], axis=1, keepdims=True)
    return (parts[0] if len(parts) == 1 else jnp.concatenate(parts, axis=1)), tail


def _sb_tile(qh, kb, valid, after, carry, tt):
    z = _dot_nt(qh, kb)
    lb = -(jnp.maximum(z, 0.0) + jnp.log(1.0 + jnp.exp(-jnp.abs(z))))
    if valid is not None:
        lb = jnp.where(valid, lb, 0.0)
    between, carry = _suffix_sums(lb, after, carry, tt)
    a = jnp.exp(z + lb + between)
    if valid is not None:
        a = jnp.where(valid, a, 0.0)
    return lb, a, carry


def _sb_window_valid(i, first, tt):
    t_pos = i * tt + lax.broadcasted_iota(jnp.int32, (tt, SB_WIN * tt), 0)
    s_pos = first * tt + lax.broadcasted_iota(jnp.int32, (tt, SB_WIN * tt), 1)
    return s_pos < t_pos


def _sb_walk_on(st):
    return jnp.logical_and(st[0] >= 0, jnp.maximum(jnp.max(st[1]), jnp.max(st[2])) > SB_EXIT)


def _sb_fwd(qkv, side=None):
    _, t, d = qkv.shape
    hp, tt = d // PAIR, _tile(t, SB_T)

    def body(q_ref, k_ref, v_ref, o_ref, o32_ref):
        row, col, head = _sb_masks(tt)
        after = (row > col).astype(BF16)

        def tile(kb, vb, valid, carry, acc, qh):
            for h in (0, 1):
                _, a, carry[h] = _sb_tile(qh[h], kb, valid, after, carry[h], tt)
                acc = acc + _dot(a.astype(BF16), jnp.where(head[h], vb, jnp.zeros_like(vb)))
            return carry, acc

        states = []
        for sub in range(SB_SUB):
            i = pl.program_id(1) * SB_SUB + sub
            q = q_ref[sub * tt:(sub + 1) * tt, :]
            qh = [jnp.where(head[h], q, jnp.zeros_like(q)) * jnp.asarray(HEAD_DIM ** -0.5, BF16) for h in (0, 1)]
            first = jnp.maximum(i - (SB_WIN - 1), 0)
            rows_w = pl.ds(pl.multiple_of(first * tt, tt), SB_WIN * tt)
            zc = jnp.zeros((tt, 1), F32)
            carry, acc = tile(k_ref[rows_w, :], v_ref[rows_w, :], _sb_window_valid(i, first, tt), [zc, zc],
                              jnp.zeros((tt, PAIR), F32), qh)
            states.append((first, carry, acc, qh))

        for sub, (first, carry, acc, qh) in enumerate(states):
            def step(st, qh=qh):
                rows = pl.ds(pl.multiple_of(st[0] * tt, tt), tt)
                carry, acc = tile(k_ref[rows, :], v_ref[rows, :], None, [st[1], st[2]], st[3], qh)
                return st[0] - 1, carry[0], carry[1], acc

            o = lax.while_loop(_sb_walk_on, step, (first - 1, carry[0], carry[1], acc))[3]
            o_ref[sub * tt:(sub + 1) * tt, :] = o.astype(BF16)
            o32_ref[sub * tt:(sub + 1) * tt, :] = o

    tq = SB_SUB * tt
    qs = pl.BlockSpec((None, tq, PAIR), lambda p, i: (0, i, p))
    ks = pl.BlockSpec((None, t, PAIR), lambda p, i: (1, 0, p))
    vs = pl.BlockSpec((None, t, PAIR), lambda p, i: (2, 0, p))
    os_ = pl.BlockSpec((tq, PAIR), lambda p, i: (i, p))
    return _side_call(
        "sb_fwd", body, (hp, t // tq), [qs, ks, vs], (os_, os_),
        (jax.ShapeDtypeStruct((t, d), BF16), jax.ShapeDtypeStruct((t, d), F32)), [], (qkv, qkv, qkv), side,
        ("parallel", "arbitrary"))


def _sb_bwd(qkv, o32, do, side=None):
    _, t, d = qkv.shape
    hp, tt = d // PAIR, _tile(t, SB_T)
    nq = t // (SB_SUB * tt)
    scale = HEAD_DIM ** -0.5

    def body(q_ref, k_ref, v_ref, o32_ref, do_ref, dqkv_ref, dk_acc, dv_acc):
        i = pl.program_id(1)
        row, col, head = _sb_masks(tt)
        after = (row > col).astype(BF16)
        from_s = (row >= col).astype(BF16)

        @pl.when(i == 0)
        def _():
            dk_acc[...] = jnp.zeros_like(dk_acc)
            dv_acc[...] = jnp.zeros_like(dv_acc)

        def tile(rows, valid, carry, seen, dq, qh, doh, tot):
            kb, vb = k_ref[rows, :], v_ref[rows, :]
            dk_t = dv_t = None
            for h in (0, 1):
                lb, a, carry[h] = _sb_tile(qh[h], kb, valid, after, carry[h], tt)
                ab = a.astype(BF16)
                g = ab.astype(F32) * _dot_nt(doh[h], vb)
                g_from, seen[h] = _suffix_sums(g, from_s, seen[h], tt)
                e = jnp.exp(lb)
                dz = g * e - (1.0 - e) * (tot[h] - g_from)
                if valid is not None:
                    dz = jnp.where(valid, dz, 0.0)
                dz = dz.astype(BF16)
                dq = dq + _dot(dz, jnp.where(head[h], kb, jnp.zeros_like(kb)))
                dk_h, dv_h = _dot_tn(dz, qh[h]), _dot_tn(ab, doh[h])
                dk_t = dk_h if dk_t is None else dk_t + dk_h
                dv_t = dv_h if dv_t is None else dv_t + dv_h
            dk_acc[rows, :] += dk_t
            dv_acc[rows, :] += dv_t
            return carry, seen, dq

        states = []
        for sub in range(SB_SUB):
            ii = i * SB_SUB + sub
            q = q_ref[sub * tt:(sub + 1) * tt, :]
            dob = do_ref[sub * tt:(sub + 1) * tt, :]
            prod = dob.astype(F32) * o32_ref[sub * tt:(sub + 1) * tt, :]
            zb = jnp.zeros_like(q)
            qh = [jnp.where(head[h], q, zb) * jnp.asarray(scale, BF16) for h in (0, 1)]
            doh = [jnp.where(head[h], dob, zb) for h in (0, 1)]
            tot = [jnp.sum(jnp.where(head[h], prod, 0.0), axis=1, keepdims=True) for h in (0, 1)]
            first = jnp.maximum(ii - (SB_WIN - 1), 0)
            zc = jnp.zeros((tt, 1), F32)
            carry, seen, dq = tile(pl.ds(pl.multiple_of(first * tt, tt), SB_WIN * tt), _sb_window_valid(ii, first, tt),
                                   [zc, zc], [zc, zc], jnp.zeros((tt, PAIR), F32), qh, doh, tot)
            states.append((first, carry, seen, dq, qh, doh, tot))

        for sub, (first, carry, seen, dq, qh, doh, tot) in enumerate(states):
            def step(st, qh=qh, doh=doh, tot=tot):
                carry, seen, dq = tile(pl.ds(pl.multiple_of(st[0] * tt, tt), tt), None, [st[1], st[2]], [st[3], st[4]], st[5],
                                       qh, doh, tot)
                return st[0] - 1, carry[0], carry[1], seen[0], seen[1], dq

            dq = lax.while_loop(_sb_walk_on, step, (first - 1, carry[0], carry[1], seen[0], seen[1], dq))[5]
            dqkv_ref[0, pl.ds(pl.multiple_of((i * SB_SUB + sub) * tt, tt), tt), :] = (dq * scale).astype(BF16)

        @pl.when(i == nq - 1)
        def _():
            dqkv_ref[1, :, :] = dk_acc[...].astype(BF16)
            dqkv_ref[2, :, :] = dv_acc[...].astype(BF16)

    qs = pl.BlockSpec((None, SB_SUB * tt, PAIR), lambda p, i: (0, i, p))
    ks = pl.BlockSpec((None, t, PAIR), lambda p, i: (1, 0, p))
    vs = pl.BlockSpec((None, t, PAIR), lambda p, i: (2, 0, p))
    ts = pl.BlockSpec((SB_SUB * tt, PAIR), lambda p, i: (i, p))
    return _side_call(
        "sb_bwd", body, (hp, nq), [qs, ks, vs, ts, ts], [pl.BlockSpec((3, t, PAIR), lambda p, i: (0, 0, p))],
        [jax.ShapeDtypeStruct((3, t, d), BF16)], [pltpu.VMEM((t, PAIR), F32), pltpu.VMEM((t, PAIR), F32)],
        (qkv, qkv, qkv, o32, do), side, ("parallel", "arbitrary"))


def _pair_sum_matrix():
    r = lax.broadcasted_iota(jnp.int32, (PAIR, PAIR), 0) // HEAD_DIM
    c = lax.broadcasted_iota(jnp.int32, (PAIR, PAIR), 1) // HEAD_DIM
    return (r == c).astype(BF16)


def _head_mean(v, ones):
    hi, lo = _split_bf16(v)
    return (_dot(hi, ones) + _dot(lo, ones)) * (1.0 / HEAD_DIM)


def _ca_prep(qkv32, gq, gk):
    _, t, d = qkv32.shape
    tr = _tile(t, CA_PAD)
    assert CA_PAD % tr == 0
    npad = CA_PAD // tr

    def body(x_ref, gq_ref, gk_ref, o_ref):
        p, r = pl.program_id(0), pl.program_id(1)

        @pl.when(r < npad)
        def _():
            o_ref[...] = jnp.zeros_like(o_ref)

        @pl.when(jnp.logical_and(r >= npad, p == 2))
        def _():
            o_ref[...] = x_ref[...].astype(BF16)

        @pl.when(jnp.logical_and(r >= npad, p < 2))
        def _():
            ones = _pair_sum_matrix()
            g = jnp.where(p == 0, gq_ref[...], gk_ref[...])
            for c in range(d // PAIR):
                x = x_ref[:, c * PAIR:(c + 1) * PAIR]
                rs = lax.rsqrt(_head_mean(x * x, ones) + RMS_EPS)
                o_ref[:, c * PAIR:(c + 1) * PAIR] = (x * rs * g).astype(BF16)

    vec = pl.BlockSpec((1, PAIR), lambda p, r: (0, 0))
    return pl.pallas_call(
        body, name="ca_prep", grid=(3, npad + t // tr),
        in_specs=[pl.BlockSpec((None, tr, d), lambda p, r: (p, jnp.maximum(r - npad, 0), 0)), vec, vec],
        out_specs=pl.BlockSpec((None, tr, d), lambda p, r: (p, r, 0)),
        out_shape=jax.ShapeDtypeStruct((3, CA_PAD + t, d), BF16), compiler_params=_cp(("parallel", "parallel")),
    )(qkv32, gq, gk)


def _ca_unprep(dq, dkv, qkv32, gq, gk):
    _, t, d = qkv32.shape
    tr = _tile(t, 512)
    nr = t // tr

    def body(dq_ref, dkv_ref, x_ref, gq_ref, gk_ref, o_ref, dgq_ref, dgk_ref):
        p, r = pl.program_id(0), pl.program_id(1)

        @pl.when(jnp.logical_and(p == 0, r == 0))
        def _():
            dgq_ref[...] = jnp.zeros_like(dgq_ref)
            dgk_ref[...] = jnp.zeros_like(dgk_ref)

        @pl.when(p == 2)
        def _():
            o_ref[...] = dkv_ref[...].astype(BF16)

        @pl.when(p < 2)
        def _():
            ones = _pair_sum_matrix()
            g = jnp.where(p == 0, gq_ref[...], gk_ref[...])
            dg = jnp.zeros((1, PAIR), F32)
            for c in range(d // PAIR):
                cols = slice(c * PAIR, (c + 1) * PAIR)
                x = x_ref[:, cols]
                dy = jnp.where(p == 0, dq_ref[:, cols], dkv_ref[:, cols])
                rs = lax.rsqrt(_head_mean(x * x, ones) + RMS_EPS)
                xhat = x * rs
                dxn = dy * g
                o_ref[:, cols] = (rs * (dxn - xhat * _head_mean(dxn * xhat, ones))).astype(BF16)
                dg = dg + jnp.sum(dy * xhat, axis=0, keepdims=True)
            dg = dg + pltpu.roll(dg, HEAD_DIM, 1)

            @pl.when(p == 0)
            def _():
                dgq_ref[...] += dg

            @pl.when(p == 1)
            def _():
                dgk_ref[...] += dg

    vec = pl.BlockSpec((1, PAIR), lambda p, r: (0, 0))
    return pl.pallas_call(
        body, name="ca_unprep", grid=(3, nr),
        in_specs=[pl.BlockSpec((tr, d), lambda p, r: (r, 0)),
                  pl.BlockSpec((None, tr, d), lambda p, r: (jnp.maximum(p - 1, 0), r, 0)),
                  pl.BlockSpec((None, tr, d), lambda p, r: (p, r, 0)), vec, vec],
        out_specs=(pl.BlockSpec((None, tr, d), lambda p, r: (p, r, 0)), vec, vec),
        out_shape=(jax.ShapeDtypeStruct((3, t, d), BF16), jax.ShapeDtypeStruct((1, PAIR), F32), jax.ShapeDtypeStruct((1, PAIR), F32)),
        compiler_params=_cp(("arbitrary", "arbitrary")),
    )(dq, dkv, qkv32, gq, gk)


def _ca_ext_index():
    m = np.arange(CA_EXT)
    return np.where(m <= CA_WIN, np.clip(CA_PAD - m, -MAX_REL, MAX_REL) + MAX_REL, 2 * MAX_REL).astype(np.int32)


def _ca_valid(i):
    a = lax.broadcasted_iota(jnp.int32, (CA_TQ, CA_WIN), 0)
    b = lax.broadcasted_iota(jnp.int32, (CA_TQ, CA_WIN), 1)
    ca, cb = a // CHUNK, b // CHUNK
    return jnp.logical_and(jnp.logical_and(cb >= ca, cb <= ca + LEFT_CHUNKS), b >= CA_PAD - CA_TQ * i)


def _skew(x, sign):
    row = lax.broadcasted_iota(jnp.int32, (CA_TQ, 1), 0)
    for bit in range(CA_TQ.bit_length() - 1):
        amount = (1 << bit) if sign > 0 else CA_EXT - (1 << bit)
        x = jnp.where(((row >> bit) & 1) == 1, pltpu.roll(x, amount, 1), x)
    return x


def _ca_bias_tiles(ext_ref, bias_ref):
    for h in (0, 1):
        bias_ref[h] = _skew(jnp.broadcast_to(ext_ref[pl.ds(h, 1), :], (CA_TQ, CA_EXT)), 1)


def _ca_probs(qh, kw, bias, valid):
    z = jnp.where(valid, _dot_nt(qh, kw) + bias, NEG_BIG)
    p = jnp.exp(z - jnp.max(z, axis=1, keepdims=True))
    return p / jnp.sum(p, axis=1, keepdims=True)


def _ca_fwd(qkvn, ext, side=None):
    _, tp, d = qkvn.shape
    t = tp - CA_PAD
    hp, tq = d // PAIR, CA_SUB * CA_TQ
    npad = CA_PAD // tq

    def body(q_ref, k_ref, v_ref, ext_ref, o_ref, bias_ref):
        i = pl.program_id(1)

        @pl.when(i == 0)
        def _():
            _ca_bias_tiles(ext_ref, bias_ref)

        lane = lax.broadcasted_iota(jnp.int32, (1, PAIR), 1)
        for sub in range(CA_SUB):
            ii = i * CA_SUB + sub
            valid = _ca_valid(ii)
            win = pl.ds(pl.multiple_of(ii * CA_TQ, CA_TQ), CA_WIN)
            kw, vw, q = k_ref[win, :], v_ref[win, :], q_ref[sub * CA_TQ:(sub + 1) * CA_TQ, :]
            o = jnp.zeros((CA_TQ, PAIR), F32)
            for h in (0, 1):
                hm = (lane // HEAD_DIM) == h
                qh = jnp.where(hm, q, jnp.zeros_like(q)) * jnp.asarray(HEAD_DIM ** -0.5, BF16)
                p = _ca_probs(qh, kw, bias_ref[h, :, :CA_WIN], valid)
                o = o + _dot(p.astype(BF16), jnp.where(hm, vw, jnp.zeros_like(vw)))
            o_ref[sub * CA_TQ:(sub + 1) * CA_TQ, :] = o.astype(BF16)

    return _side_call(
        "ca_fwd", body, (hp, t // tq),
        [pl.BlockSpec((None, tq, PAIR), lambda p, i: (0, i + npad, p)),
         pl.BlockSpec((None, tp, PAIR), lambda p, i: (1, 0, p)),
         pl.BlockSpec((None, tp, PAIR), lambda p, i: (2, 0, p)),
         pl.BlockSpec((None, 2, CA_EXT), lambda p, i: (p, 0, 0))],
        [pl.BlockSpec((tq, PAIR), lambda p, i: (i, p))], [jax.ShapeDtypeStruct((t, d), BF16)],
        [pltpu.VMEM((2, CA_TQ, CA_EXT), F32)], (qkvn, qkvn, qkvn, ext), side, ("parallel", "arbitrary"))


def _ca_bwd(qkvn, ext, do, side=None):
    _, tp, d = qkvn.shape
    t = tp - CA_PAD
    hp, tq = d // PAIR, CA_SUB * CA_TQ
    npad, nq = CA_PAD // tq, t // tq
    scale = HEAD_DIM ** -0.5

    def body(q_ref, k_ref, v_ref, ext_ref, do_ref, dq_ref, dkv_ref, dext_ref, bias_ref, dbias_ref, dk_acc, dv_acc):
        i = pl.program_id(1)

        @pl.when(i == 0)
        def _():
            _ca_bias_tiles(ext_ref, bias_ref)
            dbias_ref[...] = jnp.zeros_like(dbias_ref)
            dk_acc[...] = jnp.zeros_like(dk_acc)
            dv_acc[...] = jnp.zeros_like(dv_acc)

        lane = lax.broadcasted_iota(jnp.int32, (1, PAIR), 1)
        dbias = [None, None]
        dk_u = dv_u = None

        def spread(v, sub):
            parts = [jnp.zeros((sub * CA_TQ, PAIR), F32)] * (sub > 0) + [v] + \
                    [jnp.zeros(((CA_SUB - 1 - sub) * CA_TQ, PAIR), F32)] * (sub < CA_SUB - 1)
            return jnp.concatenate(parts, axis=0) if len(parts) > 1 else v

        for sub in range(CA_SUB):
            ii = i * CA_SUB + sub
            valid = _ca_valid(ii)
            win = pl.ds(pl.multiple_of(ii * CA_TQ, CA_TQ), CA_WIN)
            rows = slice(sub * CA_TQ, (sub + 1) * CA_TQ)
            kw, vw, q, dob = k_ref[win, :], v_ref[win, :], q_ref[rows, :], do_ref[rows, :]
            dq = jnp.zeros((CA_TQ, PAIR), F32)
            dk_t = jnp.zeros((CA_WIN, PAIR), F32)
            dv_t = jnp.zeros((CA_WIN, PAIR), F32)
            for h in (0, 1):
                hm = (lane // HEAD_DIM) == h
                qh = jnp.where(hm, q, jnp.zeros_like(q)) * jnp.asarray(scale, BF16)
                doh = jnp.where(hm, dob, jnp.zeros_like(dob))
                p = _ca_probs(qh, kw, bias_ref[h, :, :CA_WIN], valid)
                dp = _dot_nt(doh, vw)
                ds = p * (dp - jnp.sum(p * dp, axis=1, keepdims=True))
                dbias[h] = ds if dbias[h] is None else dbias[h] + ds
                dsb = ds.astype(BF16)
                dq = dq + _dot(dsb, jnp.where(hm, kw, jnp.zeros_like(kw)))
                dk_t = dk_t + _dot_tn(dsb, qh)
                dv_t = dv_t + _dot_tn(p.astype(BF16), doh)
            dq_ref[rows, :] = dq * scale
            dk_u = spread(dk_t, sub) if dk_u is None else dk_u + spread(dk_t, sub)
            dv_u = spread(dv_t, sub) if dv_u is None else dv_u + spread(dv_t, sub)
        union = pl.ds(pl.multiple_of(i * tq, CA_TQ), CA_WIN + (CA_SUB - 1) * CA_TQ)
        dk_acc[union, :] += dk_u
        dv_acc[union, :] += dv_u
        for h in (0, 1):
            dbias_ref[h, :, :CA_WIN] += dbias[h]

        @pl.when(i == nq - 1)
        def _():
            dkv_ref[0, :, :] = dk_acc[CA_PAD:, :]
            dkv_ref[1, :, :] = dv_acc[CA_PAD:, :]
            for h in (0, 1):
                dext_ref[pl.ds(h, 1), :] = jnp.sum(_skew(dbias_ref[h], -1), axis=0, keepdims=True)

    es = pl.BlockSpec((None, 2, CA_EXT), lambda p, i: (p, 0, 0))
    ts = pl.BlockSpec((tq, PAIR), lambda p, i: (i, p))
    return _side_call(
        "ca_bwd", body, (hp, nq),
        [pl.BlockSpec((None, tq, PAIR), lambda p, i: (0, i + npad, p)),
         pl.BlockSpec((None, tp, PAIR), lambda p, i: (1, 0, p)),
         pl.BlockSpec((None, tp, PAIR), lambda p, i: (2, 0, p)), es, ts],
        (ts, pl.BlockSpec((2, t, PAIR), lambda p, i: (0, 0, p)), es),
        (jax.ShapeDtypeStruct((t, d), F32), jax.ShapeDtypeStruct((2, t, d), F32), jax.ShapeDtypeStruct((hp, 2, CA_EXT), F32)),
        [pltpu.VMEM((2, CA_TQ, CA_EXT), F32), pltpu.VMEM((2, CA_TQ, CA_EXT), F32),
         pltpu.VMEM((tp, PAIR), F32), pltpu.VMEM((tp, PAIR), F32)],
        (qkvn, qkvn, qkvn, ext, do), side, ("parallel", "arbitrary"))


def _loss_head(y, target):
    t, d = y.shape
    tm = _tile(t, 512)

    def body(y_ref, t_ref, dy_ref, loss_ref):
        diff = y_ref[...] - t_ref[...]
        dy_ref[...] = diff * (1.0 / d)
        part = 0.5 * jnp.sum(jnp.mean(diff * diff, axis=-1, keepdims=True), axis=0, keepdims=True)

        @pl.when(pl.program_id(0) == 0)
        def _():
            loss_ref[...] = jnp.zeros_like(loss_ref)

        loss_ref[...] += jnp.broadcast_to(part, loss_ref.shape)

    row = pl.BlockSpec((tm, d), lambda m: (m, 0))
    return pl.pallas_call(
        body, name="loss_head", grid=(t // tm,), in_specs=[row, row],
        out_specs=(row, pl.BlockSpec((8, 128), lambda m: (0, 0))),
        out_shape=(jax.ShapeDtypeStruct((t, d), F32), jax.ShapeDtypeStruct((8, 128), F32)),
        compiler_params=_cp(("arbitrary",)),
    )(y, target)


def _adamw_math(w, g, m, v):
    m = ADAM_B1 * m + (1.0 - ADAM_B1) * g
    v = ADAM_B2 * v + (1.0 - ADAM_B2) * (g * g)
    m_hat = m / (1.0 - ADAM_B1 ** ADAM_STEP)
    v_hat = v / (1.0 - ADAM_B2 ** ADAM_STEP)
    delta = -ADAM_LR * (m_hat / (jnp.sqrt(v_hat) + ADAM_EPS) + ADAM_WD * w)
    return delta, m, v


def _adamw(name, w, g, m, v):
    shape = w.shape
    cols = shape[-1]
    rows = int(np.prod(shape[:-1]))
    tr = _tile(rows, 512)
    flat = [a.reshape(rows, cols) for a in (w, g, m, v)]

    def body(w_ref, g_ref, m_ref, v_ref, d_ref, nm_ref, nv_ref):
        d_ref[...], nm_ref[...], nv_ref[...] = _adamw_math(w_ref[...], g_ref[...], m_ref[...], v_ref[...])

    blk = pl.BlockSpec((tr, cols), lambda r: (r, 0))
    outs = pl.pallas_call(
        body, name=name, grid=(rows // tr,), in_specs=[blk] * 4, out_specs=(blk,) * 3,
        out_shape=(jax.ShapeDtypeStruct((rows, cols), F32),) * 3, compiler_params=_cp(("parallel",)),
    )(*flat)
    return [o.reshape(shape) for o in outs]


def _place():
    x, y, c = lax.axis_index("x"), lax.axis_index("y"), lax.axis_index("c")
    chips = [(1 - x, y), (x, 1 - y), (1 - x, 1 - y)]
    return x, y, c, chips


def _shard_slab(ref, kind, layer0, n_layers, shard, width):
    lay = pl.ds(layer0, n_layers)
    if kind == "cols":
        return ref.at[lay, :, pl.ds(shard * width, width)]
    return ref.at[lay, pl.ds(shard * width, width), :]


def _cast_into_place(w, kind, shard):
    n_layers, rows, cols = w.shape
    tr = _tile(rows, 512)
    nr = rows // tr

    def body(s_ref, w_ref, o_ref):
        o_ref[...] = w_ref[...].astype(BF16)

    if kind == "cols":
        full, out = (n_layers, rows, cols * N_CHIPS), pl.BlockSpec((None, tr, cols), lambda l, r, s_ref: (l, r, s_ref[0]))
    else:
        full, out = (n_layers, rows * N_CHIPS, cols), pl.BlockSpec((None, tr, cols), lambda l, r, s_ref: (l, s_ref[0] * nr + r, 0))
    grid_spec = pltpu.PrefetchScalarGridSpec(
        num_scalar_prefetch=1, grid=(n_layers, nr),
        in_specs=[pl.BlockSpec((None, tr, cols), lambda l, r, s_ref: (l, r, 0))], out_specs=out)
    return pl.pallas_call(body, name="cast_into_place", grid_spec=grid_spec, out_shape=jax.ShapeDtypeStruct(full, BF16),
                          compiler_params=_cp(("parallel", "parallel")))(shard, w)


def _gather_side(layer, fulls, kinds, widths):
    na = len(fulls)
    owner = layer % 2

    def parts(w, send, recv):
        x, y, c, chips = _place()

        def copy(k, a, shard, to):
            slab = _shard_slab(w[a], kinds[a], layer, 1, shard, widths[a])
            return pltpu.make_async_remote_copy(src_ref=slab, dst_ref=slab, send_sem=send.at[k], recv_sem=recv.at[k],
                                                device_id=to, device_id_type=MESH)
        return x, y, c, chips, copy

    def start(_, w, send, recv):
        x, y, c, chips, copy = parts(w, send, recv)

        @pl.when(c == owner)
        def _():
            for j, chip in enumerate(chips):
                for a in range(na):
                    copy(j * na + a, a, 2 * x + y, (*chip, c)).start()

    def mid(_, w, send, recv):
        x, y, c, chips, copy = parts(w, send, recv)

        @pl.when(c == owner)
        def _():
            for j, (cx, cy) in enumerate(chips):
                for a in range(na):
                    copy(j * na + a, a, 2 * cx + cy, (x, y, c)).wait_recv()
                    copy(3 * na + j * na + a, a, 2 * cx + cy, (x, y, 1 - c)).start()

    def finish(_, w, send, recv):
        x, y, c, chips, copy = parts(w, send, recv)

        @pl.when(c == owner)
        def _():
            for j, (cx, cy) in enumerate(chips):
                for a in range(na):
                    copy(j * na + a, a, 2 * x + y, (cx, cy, c)).wait_send()
                    copy(3 * na + j * na + a, a, 2 * cx + cy, (x, y, 1 - c)).wait_send()

        @pl.when(c != owner)
        def _():
            for j, (cx, cy) in enumerate(chips):
                for a in range(na):
                    copy(3 * na + j * na + a, a, 2 * cx + cy, (x, y, c)).wait_recv()

    return _Side(fulls, [True] * na, [], 6 * na, 6 * na, start, mid, finish)


def _shard_piece(ref, kind, shard, width):
    return ref.at[:, shard * width:(shard + 1) * width] if kind == "cols" else ref.at[shard * width:(shard + 1) * width, :]


def _reduce_side(layer, grads, kinds, widths):
    na = len(grads)
    owner = layer % 2
    new_outs = []
    for g, k, w in zip(grads, kinds, widths):
        new_outs.append(jax.ShapeDtypeStruct((N_DEV,) + ((g.shape[0], w) if k == "cols" else (w, g.shape[1])), g.dtype))

    def sends(g, got, send, recv, act):
        x, y, c, _ = _place()
        me = 4 * x + 2 * y + c
        for b in range(N_CHIPS):
            bx, by = b >> 1, b & 1
            to_me = jnp.logical_and(jnp.logical_and(x == bx, y == by), c == owner)

            @pl.when(jnp.logical_not(to_me))
            def _(b=b, bx=bx, by=by):
                for a in range(na):
                    cp = pltpu.make_async_remote_copy(
                        src_ref=_shard_piece(g[a], kinds[a], b, widths[a]), dst_ref=got[a].at[me], send_sem=send.at[b * na + a],
                        recv_sem=recv.at[me * na + a], device_id=(bx, by, owner), device_id_type=MESH)
                    cp.start() if act == "start" else cp.wait_send()

    def start(g, got, send, recv):
        sends(g, got, send, recv, "start")

    def mid(g, got, send, recv):
        pass

    def finish(g, got, send, recv):
        sends(g, got, send, recv, "wait")
        x, y, c, _ = _place()
        me = 4 * x + 2 * y + c

        @pl.when(c == owner)
        def _():
            for s in range(N_DEV):
                @pl.when(me != s)
                def _(s=s):
                    for a in range(na):
                        pltpu.make_async_remote_copy(
                            src_ref=_shard_piece(g[a], kinds[a], 0, widths[a]), dst_ref=got[a].at[s], send_sem=send.at[0],
                            recv_sem=recv.at[s * na + a], device_id=(x, y, c), device_id_type=MESH).wait_recv()

    return _Side(grads, [False] * na, new_outs, N_CHIPS * na, N_DEV * na, start, mid, finish)


def _reduce_sum(layer, n_layers, g, got, kind, width, prev, flags):
    _, rows, cols = got.shape
    tr = _tile(rows, 256)
    nr = rows // tr

    def body(f_ref, g_ref, got_ref, *rest):
        o_ref = rest[-1]

        @pl.when(f_ref[2] == 1)
        def _():
            acc = None
            for s in range(N_DEV):
                term = jnp.where(f_ref[1] == s, g_ref[...], got_ref[s]).astype(F32)
                acc = term if acc is None else acc + term
            o_ref[...] = acc

    if kind == "cols":
        own = pl.BlockSpec((tr, width), lambda r, f_ref: (r, f_ref[0]))
    else:
        own = pl.BlockSpec((tr, cols), lambda r, f_ref: (f_ref[0] * nr + r, 0))
    grid_spec = pltpu.PrefetchScalarGridSpec(
        num_scalar_prefetch=1, grid=(nr,),
        in_specs=[own, pl.BlockSpec((N_DEV, tr, cols), lambda r, f_ref: (0, r, 0))] + ([] if prev is None else [ANY]),
        out_specs=pl.BlockSpec((None, tr, cols), lambda r, f_ref: (layer, r, 0)))
    return pl.pallas_call(
        body, name="reduce_sum", grid_spec=grid_spec, out_shape=jax.ShapeDtypeStruct((n_layers, rows, cols), F32),
        input_output_aliases={} if prev is None else {3: 0}, compiler_params=_cp(("arbitrary",)),
    )(flags, g, got, *([] if prev is None else [prev]))


def _pair_share(sums):
    n_layers = sums[0].shape[0]
    na = len(sums)

    def body(*refs):
        dst = refs[na:2 * na]
        send_sems, recv_sems = refs[2 * na:]
        x, y, c, _ = _place()

        def swap(l, a):
            return pltpu.make_async_remote_copy(
                src_ref=dst[a].at[l], dst_ref=dst[a].at[l], send_sem=send_sems.at[l * na + a],
                recv_sem=recv_sems.at[l * na + a], device_id=(x, y, 1 - c), device_id_type=MESH)

        for l in range(n_layers):
            @pl.when(c == l % 2)
            def _(l=l):
                for a in range(na):
                    swap(l, a).start()
        for l in range(n_layers):
            @pl.when(c == l % 2)
            def _(l=l):
                for a in range(na):
                    swap(l, a).wait_send()

            @pl.when(c != l % 2)
            def _(l=l):
                for a in range(na):
                    swap(l, a).wait_recv()

    return pl.pallas_call(
        body, name="pair_share", in_specs=[ANY] * na, out_specs=[ANY] * na,
        out_shape=[jax.ShapeDtypeStruct(s.shape, s.dtype) for s in sums], input_output_aliases={a: a for a in range(na)},
        scratch_shapes=[pltpu.SemaphoreType.DMA((n_layers * na,)), pltpu.SemaphoreType.DMA((n_layers * na,))],
    )(*sums)


def _small_step(g_part, w, m, v):
    r = g_part.shape[0]

    def body(g_ref, w_ref, m_ref, v_ref, go_ref, d_ref, nm_ref, nv_ref, all_ref, send_sems, recv_sems):
        x, y, c, _ = _place()
        me = 4 * x + 2 * y + c
        all_ref[me] = g_ref[...]
        cps = []
        for k in range(1, N_DEV):
            px, py, pc = (x + (k >> 2)) % 2, (y + ((k >> 1) & 1)) % 2, (c + (k & 1)) % 2
            cps.append(pltpu.make_async_remote_copy(src_ref=g_ref, dst_ref=all_ref.at[me], send_sem=send_sems.at[k - 1],
                                                    recv_sem=recv_sems.at[k - 1], device_id=(px, py, pc), device_id_type=MESH))
        for cp in cps:
            cp.start()
        for cp in cps:
            cp.wait()
        g = all_ref[0]
        for k in range(1, N_DEV):
            g = g + all_ref[k]
        go_ref[...] = g
        d_ref[...], nm_ref[...], nv_ref[...] = _adamw_math(w_ref[...], g, m_ref[...], v_ref[...])

    vm = pl.BlockSpec(memory_space=pltpu.VMEM)
    return pl.pallas_call(
        body, name="small_step", in_specs=[vm] * 4, out_specs=[vm] * 4, out_shape=[jax.ShapeDtypeStruct((r, 128), F32)] * 4,
        scratch_shapes=[pltpu.VMEM((N_DEV, r, 128), F32), pltpu.SemaphoreType.DMA((N_DEV - 1,)), pltpu.SemaphoreType.DMA((N_DEV - 1,))],
    )(g_part, w, m, v)


def _onehot_mm(name, a, onehot):
    def body(a_ref, oh_ref, o_ref):
        v = a_ref[...]
        oh = oh_ref[...]
        hi, lo = _split_bf16(v)
        lo2 = (v - hi.astype(F32) - lo.astype(F32)).astype(BF16)
        o_ref[...] = _dot(hi, oh) + _dot(lo, oh) + _dot(lo2, oh)

    return pl.pallas_call(body, name=name, out_shape=jax.ShapeDtypeStruct((a.shape[0], onehot.shape[1]), F32))(a, onehot)


def _pack_small(parts):
    flat = jnp.concatenate([p.reshape(-1) for p in parts])
    n = flat.shape[0]
    rows = -(-n // 128)
    rows = -(-rows // 8) * 8
    return jnp.pad(flat, (0, rows * 128 - n)).reshape(rows, 128)


def _unpack_small(packed, like):
    flat = packed.reshape(-1)
    out, off = [], 0
    for p in like:
        out.append(flat[off:off + p.size].reshape(p.shape))
        off += p.size
    return out


def kernel(x, mix_norm, w_qkv, w_o, q_norm, k_norm, rel_bias, ffn_norm, w_up, w_down, loss_target, m_mix_norm, m_w_qkv, m_w_o, m_q_norm, m_k_norm, m_rel_bias, m_ffn_norm, m_w_up, m_w_down, v_mix_norm, v_w_qkv, v_w_o, v_q_norm, v_k_norm, v_rel_bias, v_ffn_norm, v_w_up, v_w_down):
    n_layers, d = mix_norm.shape
    t = x.shape[1]
    ff = w_down.shape[1] * N_CHIPS
    heads = d // HEAD_DIM
    cx, cy, cc = lax.axis_index("x"), lax.axis_index("y"), lax.axis_index("c")
    shard = (2 * cx + cy).astype(jnp.int32).reshape(1)
    core = cc.astype(jnp.int32).reshape(1)

    big = [w_qkv, w_o, w_up, w_down]
    kinds = ["cols", "rows", "cols", "rows"]
    widths = [w_qkv.shape[2], w_o.shape[1], w_up.shape[2], w_down.shape[1]]
    fulls = _blocking("gather_first", _gather_side(0, [_cast_into_place(w, k, shard) for w, k in zip(big, kinds)], kinds, widths))

    rel_pad = -(-N_REL // PAIR) * PAIR
    ext_hot = _ca_ext_index()[:, None] == np.arange(rel_pad)[None, :]
    fold_hot, spread_hot = jnp.asarray(ext_hot, BF16), jnp.asarray(ext_hot.T, BF16)
    rel_tab = jnp.pad(rel_bias, ((0, 0), (0, 0), (0, rel_pad - N_REL)))
    exts = [_onehot_mm("relbias_spread", rel_tab[i], spread_hot).reshape(heads // 2, 2, CA_EXT) for i in range(n_layers // 2)]

    xs, h1s, qkvs, os_, o32s, x2s, h2s, ss, us, q32s = [], [], [], [], [], [], [], [], [], []
    xc = x[0]
    h = _rms_first(xc, mix_norm[0:1])
    for layer in range(n_layers):
        xs.append(xc)
        h1s.append(h)
        side = _gather_side(layer + 1, fulls, kinds, widths) if layer + 1 < n_layers else None
        if layer % 2 == 0:
            qkv = _qkv_proj(h, fulls[0], layer, BF16)
            (o, o32), gathered = _sb_fwd(qkv, side)
            q32s.append(None)
        else:
            idx = layer // 2
            q32 = _qkv_proj(h, fulls[0], layer, F32)
            gq = jnp.tile(q_norm[idx], 2).reshape(1, PAIR)
            gk = jnp.tile(k_norm[idx], 2).reshape(1, PAIR)
            qkv = _ca_prep(q32, gq, gk)
            (o,), gathered = _ca_fwd(qkv, exts[idx], side)
            o32 = None
            q32s.append(q32)
        fulls = gathered if side is not None else fulls
        wq, wo, wu, wd = fulls
        qkvs.append(qkv)
        os_.append(o)
        o32s.append(o32)
        x2, h2 = _out_proj("attn_out", o, wo, layer, xc, ffn_norm[layer:layer + 1])
        s, u = _up_proj(h2, wu, layer)
        nxt = mix_norm[layer + 1:layer + 2] if layer + 1 < n_layers else mix_norm[0:1]
        xc, h = _out_proj("mlp_out", u, wd, layer, x2, nxt)
        x2s.append(x2)
        h2s.append(h2)
        ss.append(s)
        us.append(u)

    dx, loss_part = _loss_head(xc, loss_target[0])
    loss = lax.psum(loss_part[0, 0], ("x", "y", "c"))

    d_mix, d_ffn = [None] * n_layers, [None] * n_layers
    d_qn, d_kn, d_rb = [], [], []
    device = (4 * cx + 2 * cy + cc).astype(jnp.int32).reshape(1)
    sums = [None] * len(big)

    def reduce_sums(of_layer, grads, got):
        flags = jnp.concatenate([shard, device, (core == of_layer % 2).astype(jnp.int32)])
        return [_reduce_sum(of_layer, n_layers, g, r_, k, w_, s_, flags)
                for g, r_, k, w_, s_ in zip(grads, got, kinds, widths, sums)]

    waiting = None
    for layer in reversed(range(n_layers)):
        du = _down_bwd(dx, wd, layer, ss[layer])
        g_down = _wgrad("wgrad_down", us[layer], _mat_spec, dx, _mat_spec_b, t, ff, d)
        g_up = _wgrad("wgrad_up", h2s[layer], _mat_spec, du, _mat_spec_b, t, d, ff)
        dx2, d_ffn[layer] = _norm_bwd_proj(
            "up_bwd", du, _rows2(ff), [(None, slice(None), slice(None))], wu, layer, x2s[layer], ffn_norm[layer:layer + 1], dx)
        do = _plain_nt("attn_out_bwd", dx2, wo, layer, BF16)
        g_o = _wgrad("wgrad_o", os_[layer], _mat_spec, dx2, _mat_spec_b, t, d, d)
        side = _reduce_side(waiting[0], waiting[1], kinds, widths) if waiting is not None else None
        if layer % 2 == 0:
            (dqkv,), got = _sb_bwd(qkvs[layer], o32s[layer], do, side)
        else:
            idx = layer // 2
            gq = jnp.tile(q_norm[idx], 2).reshape(1, PAIR)
            gk = jnp.tile(k_norm[idx], 2).reshape(1, PAIR)
            (dqn, dkv, dext), got = _ca_bwd(qkvs[layer], exts[idx], do, side)
            dqkv, dgq, dgk = _ca_unprep(dqn, dkv, q32s[layer], gq, gk)
            d_qn.insert(0, dgq[0, :HEAD_DIM])
            d_kn.insert(0, dgk[0, :HEAD_DIM])
            d_rb.insert(0, _onehot_mm("relbias_fold", dext.reshape(heads, CA_EXT), fold_hot)[:, :N_REL])
        if waiting is not None:
            sums = reduce_sums(waiting[0], waiting[1], got)
        g_qkv = _wgrad("wgrad_qkv", h1s[layer], _mat_spec, dqkv, _plane_spec_b(d), t, d, 3 * d, tno=_tile(d, 1024))
        waiting = (layer, [g_qkv, g_o, g_up, g_down])
        dx, d_mix[layer] = _norm_bwd_proj(
            "qkv_bwd", dqkv, lambda tm: pl.BlockSpec((3, tm, d), lambda m: (0, m, 0)),
            [(p, slice(None), slice(p * d, (p + 1) * d)) for p in range(3)], wq, layer, xs[layer], mix_norm[layer:layer + 1], dx2)
    grad_x = dx.reshape(x.shape)
    sums = reduce_sums(waiting[0], waiting[1], _blocking("reduce_last", _reduce_side(waiting[0], waiting[1], kinds, widths)))
    g_big = _pair_share(sums)

    upd = [_adamw("adamw", w_, g_, m_, v_) for w_, g_, m_, v_ in
           zip(big, g_big, [m_w_qkv, m_w_o, m_w_up, m_w_down], [v_w_qkv, v_w_o, v_w_up, v_w_down])]

    small_w = [mix_norm, q_norm, k_norm, rel_bias, ffn_norm]
    small_g = [jnp.concatenate(d_mix, 0), jnp.stack(d_qn), jnp.stack(d_kn), jnp.stack(d_rb), jnp.concatenate(d_ffn, 0)]
    packed = _small_step(_pack_small(small_g), _pack_small(small_w),
                         _pack_small([m_mix_norm, m_q_norm, m_k_norm, m_rel_bias, m_ffn_norm]),
                         _pack_small([v_mix_norm, v_q_norm, v_k_norm, v_rel_bias, v_ffn_norm]))
    sg, sd, sm, sv = [_unpack_small(p, small_w) for p in packed]

    def order(small, bigs):
        return [small[0], bigs[0], bigs[1], small[1], small[2], small[3], small[4], bigs[2], bigs[3]]

    return (loss, grad_x, *order(sg, g_big), *order(sd, [u_[0] for u_ in upd]),
            *order(sm, [u_[1] for u_ in upd]), *order(sv, [u_[2] for u_ in upd]))
```

```python
import functools

import numpy as np
import jax
import jax.numpy as jnp
from jax import lax
from jax.experimental import pallas as pl
from jax.experimental.pallas import tpu as pltpu

F32, BF16 = jnp.float32, jnp.bfloat16
MESH = pl.DeviceIdType.MESH

HEAD_DIM = 64
PAIR = 128
CHUNK = 64
LEFT_CHUNKS = 8
MAX_REL = 256
N_REL = 2 * MAX_REL + 1
CA_TQ = 128
CA_SUB = 2
CA_PAD = LEFT_CHUNKS * CHUNK
CA_WIN = CA_PAD + CA_TQ
CA_EXT = CA_WIN + CA_TQ
SB_T = 128
SB_WIN = 3
SB_SUB = 2
RMS_EPS = 1e-6
SB_EXIT = -104.0
NEG_BIG = -1e30
ADAM_LR, ADAM_B1, ADAM_B2, ADAM_EPS, ADAM_WD, ADAM_STEP = 0.001, 0.9, 0.999, 1e-08, 0.01, 10
VMEM_LIMIT = 56 << 20
N_CHIPS = 4
N_DEV = 8
ANY = pl.BlockSpec(memory_space=pl.ANY)


def _cp(sem=None, **kw):
    return pltpu.CompilerParams(dimension_semantics=sem, vmem_limit_bytes=VMEM_LIMIT, **kw)


def _tile(n, want):
    t = min(n, want)
    assert n % t == 0, (n, t)
    return t


class _Side:
    def __init__(self, ins, aliased, new_outs, n_send, n_recv, start, mid, finish):
        self.ins, self.aliased, self.new_outs = list(ins), list(aliased), list(new_outs)
        self.n_send, self.n_recv, self.start, self.mid, self.finish = n_send, n_recv, start, mid, finish

    def out_shapes(self):
        return [jax.ShapeDtypeStruct(a.shape, a.dtype) for a, al in zip(self.ins, self.aliased) if al] + self.new_outs


def _side_call(name, body, grid, in_specs, out_specs, out_shape, scratch_shapes, args, side, sem):
    out_shape, out_specs = list(out_shape), list(out_specs)
    if side is None:
        outs = pl.pallas_call(body, name=name, grid=grid, in_specs=in_specs, out_specs=out_specs, out_shape=out_shape,
                              scratch_shapes=scratch_shapes, compiler_params=_cp(sem))(*args)
        return list(outs), []
    n_in, n_out, n_sc, n_sin = len(in_specs), len(out_shape), len(scratch_shapes), len(side.ins)
    s_outs = side.out_shapes()

    def wrapped(*refs):
        ins, s_in = refs[:n_in], refs[n_in:n_in + n_sin]
        outs = refs[n_in + n_sin:n_in + n_sin + n_out]
        s_out = refs[n_in + n_sin + n_out:n_in + n_sin + n_out + len(s_outs)]
        scr = refs[n_in + n_sin + n_out + len(s_outs):]
        send, recv = scr[n_sc], scr[n_sc + 1]
        p, i = pl.program_id(0), pl.program_id(1)

        @pl.when(jnp.logical_and(p == 0, i == 0))
        def _():
            side.start(s_in, s_out, send, recv)

        body(*ins, *outs, *scr[:n_sc])

        @pl.when(jnp.logical_and(p == grid[0] // 2, i == 0))
        def _():
            side.mid(s_in, s_out, send, recv)

        @pl.when(jnp.logical_and(p == grid[0] - 1, i == grid[1] - 1))
        def _():
            side.finish(s_in, s_out, send, recv)

    aliases, pos = {}, n_out
    for k, al in enumerate(side.aliased):
        if al:
            aliases[n_in + k] = pos
            pos += 1
    outs = pl.pallas_call(
        wrapped, name=name, grid=grid, in_specs=list(in_specs) + [ANY] * n_sin, out_specs=out_specs + [ANY] * len(s_outs),
        out_shape=out_shape + s_outs, input_output_aliases=aliases,
        scratch_shapes=list(scratch_shapes) + [pltpu.SemaphoreType.DMA((side.n_send,)), pltpu.SemaphoreType.DMA((side.n_recv,))],
        compiler_params=_cp(("arbitrary", "arbitrary")),
    )(*args, *side.ins)
    return list(outs[:n_out]), list(outs[n_out:])


def _blocking(name, side):
    s_outs = side.out_shapes()
    n_sin = len(side.ins)

    def body(*refs):
        s_in, s_out = refs[:n_sin], refs[n_sin:n_sin + len(s_outs)]
        send, recv = refs[n_sin + len(s_outs):]
        side.start(s_in, s_out, send, recv)
        side.mid(s_in, s_out, send, recv)
        side.finish(s_in, s_out, send, recv)

    aliases, pos = {}, 0
    for k, al in enumerate(side.aliased):
        if al:
            aliases[k] = pos
            pos += 1
    return list(pl.pallas_call(
        body, name=name, in_specs=[ANY] * n_sin, out_specs=[ANY] * len(s_outs), out_shape=s_outs, input_output_aliases=aliases,
        scratch_shapes=[pltpu.SemaphoreType.DMA((side.n_send,)), pltpu.SemaphoreType.DMA((side.n_recv,))],
    )(*side.ins))


def _split_bf16(v):
    hi = v.astype(BF16)
    lo = (v - hi.astype(F32)).astype(BF16)
    return hi, lo


def _dot(a, b):
    return jnp.dot(a, b, preferred_element_type=F32)


def _dot_nt(a, b):
    return lax.dot_general(a, b, (((1,), (1,)), ((), ())), preferred_element_type=F32)


def _dot_tn(a, b):
    return lax.dot_general(a, b, (((0,), (0,)), ((), ())), preferred_element_type=F32)


TM = 512
TN = 1024


def _row_matmul(name, a, a_spec, w, layer, nt, chunks, extra, extra_specs, out_shape, out_specs, epilogue, t,
                sem=("parallel",)):
    tm = _tile(t, TM)
    n_extra = len(extra)

    def body(a_ref, w_ref, *rest):
        ins, outs = rest[:n_extra], rest[n_extra:]
        for n, terms in enumerate(chunks):
            acc = None
            for plane, rows, cols in terms:
                av = (a_ref[...] if plane is None else a_ref[plane]).astype(BF16)
                part = _dot_nt(av, w_ref[rows, cols]) if nt else _dot(av, w_ref[rows, cols])
                acc = part if acc is None else acc + part
            epilogue(n, acc, ins, outs)

    w_spec = pl.BlockSpec((None,) + w.shape[1:], lambda m: (layer, 0, 0))
    return pl.pallas_call(
        body, name=name, grid=(t // tm,), in_specs=[a_spec(tm), w_spec, *extra_specs], out_specs=out_specs, out_shape=out_shape,
        compiler_params=_cp(sem),
    )(a, w, *extra)


def _rows2(width):
    return lambda tm: pl.BlockSpec((tm, width), lambda m: (m, 0))


def _col_chunks(n, width=None):
    tn = _tile(n, TN)
    return [[(None, slice(None), slice(c * tn, (c + 1) * tn))] for c in range(n // tn)], tn


def _rms_rows(x, gain):
    r = lax.rsqrt(jnp.mean(x * x, axis=-1, keepdims=True) + RMS_EPS)
    return x * r * gain


def _rms_bwd_rows(dh, x, gain, dres):
    r = lax.rsqrt(jnp.mean(x * x, axis=-1, keepdims=True) + RMS_EPS)
    xhat = x * r
    dxn = dh * gain
    dx = r * (dxn - xhat * jnp.mean(dxn * xhat, axis=-1, keepdims=True))
    return dx + dres, jnp.sum(dh * xhat, axis=0, keepdims=True)


def _rms_first(x, gain):
    t, d = x.shape
    tm = _tile(t, 512)

    def body(x_ref, g_ref, h_ref):
        h_ref[...] = _rms_rows(x_ref[...], g_ref[...]).astype(BF16)

    return pl.pallas_call(
        body, name="rms_first", grid=(t // tm,),
        in_specs=[pl.BlockSpec((tm, d), lambda m: (m, 0)), pl.BlockSpec((1, d), lambda m: (0, 0))],
        out_specs=pl.BlockSpec((tm, d), lambda m: (m, 0)), out_shape=jax.ShapeDtypeStruct((t, d), BF16),
        compiler_params=_cp(("parallel",)),
    )(x, gain)


def _qkv_proj(h, w, layer, out_dtype):
    t, d = h.shape
    chunks = [[(None, slice(None), slice(p * d, (p + 1) * d))] for p in range(3)]

    def epi(n, acc, ins, outs):
        outs[0][n] = acc.astype(out_dtype)

    return _row_matmul(
        "qkv_proj", h, _rows2(d), w, layer, False, chunks, (), (), (jax.ShapeDtypeStruct((3, t, d), out_dtype),),
        (pl.BlockSpec((3, _tile(t, TM), d), lambda m: (0, m, 0)),), epi, t)[0]


def _out_proj(name, a, w, layer, res, gain):
    t, kk = a.shape
    d = res.shape[1]

    def epi(n, acc, ins, outs):
        xn = ins[0][...] + acc
        outs[0][...] = xn
        outs[1][...] = _rms_rows(xn, ins[1][...]).astype(BF16)

    row = _rows2(d)(_tile(t, TM))
    return _row_matmul(
        name, a, _rows2(kk), w, layer, False, [[(None, slice(None), slice(None))]], (res, gain),
        (row, pl.BlockSpec((1, d), lambda m: (0, 0))),
        (jax.ShapeDtypeStruct((t, d), F32), jax.ShapeDtypeStruct((t, d), BF16)), (row, row), epi, t)


def _up_proj(h, w, layer):
    t, d = h.shape
    ff = w.shape[2]
    chunks, tn = _col_chunks(ff)

    def epi(n, acc, ins, outs):
        s = jnp.maximum(acc, 0.0)
        outs[0][:, n * tn:(n + 1) * tn] = s.astype(BF16)
        outs[1][:, n * tn:(n + 1) * tn] = (s * s).astype(BF16)

    o = _rows2(ff)(_tile(t, TM))
    return _row_matmul("up_proj", h, _rows2(d), w, layer, False, chunks, (), (), (jax.ShapeDtypeStruct((t, ff), BF16),) * 2, (o, o), epi, t)


def _down_bwd(dx, w, layer, s):
    t, d = dx.shape
    ff = w.shape[1]
    tn = _tile(ff, TN)
    chunks = [[(None, slice(c * tn, (c + 1) * tn), slice(None))] for c in range(ff // tn)]

    def epi(n, acc, ins, outs):
        cols = slice(n * tn, (n + 1) * tn)
        outs[0][:, cols] = (acc * (2.0 * ins[0][:, cols].astype(F32))).astype(BF16)

    o = _rows2(ff)(_tile(t, TM))
    return _row_matmul("down_bwd", dx, _rows2(d), w, layer, True, chunks, (s,), (o,), (jax.ShapeDtypeStruct((t, ff), BF16),), (o,), epi, t)[0]


def _norm_bwd_proj(name, a, a_spec, terms, w, layer, x, gain, dres):
    t, d = x.shape

    def epi(n, acc, ins, outs):
        dx, dg = _rms_bwd_rows(acc, ins[0][...], ins[1][...], ins[2][...])
        outs[0][...] = dx

        @pl.when(pl.program_id(0) == 0)
        def _():
            outs[1][...] = dg

        @pl.when(pl.program_id(0) > 0)
        def _():
            outs[1][...] += dg

    row = _rows2(d)(_tile(t, TM))
    vec = pl.BlockSpec((1, d), lambda m: (0, 0))
    return _row_matmul(
        name, a, a_spec, w, layer, True, [terms], (x, gain, dres), (row, vec, row),
        (jax.ShapeDtypeStruct((t, d), F32), jax.ShapeDtypeStruct((1, d), F32)), (row, vec), epi, t, sem=("arbitrary",))


def _plain_nt(name, a, w, layer, out_dtype):
    t, n = a.shape
    m_out = w.shape[1]

    def epi(n_, acc, ins, outs):
        outs[0][...] = acc.astype(out_dtype)

    return _row_matmul(
        name, a, _rows2(n), w, layer, True, [[(None, slice(None), slice(None))]], (), (),
        (jax.ShapeDtypeStruct((t, m_out), out_dtype),), (_rows2(m_out)(_tile(t, TM)),), epi, t)[0]


def _wgrad(name, a, a_spec_fn, b, b_spec_fn, t, mo, no, tno=1024):
    tmo, tno, tk = _tile(mo, 1024), _tile(no, tno), _tile(t, 2048)
    nk = t // tk

    def body(a_ref, b_ref, o_ref, acc_ref):
        part = _dot_tn(a_ref[...].astype(BF16), b_ref[...].astype(BF16))
        k = pl.program_id(2)

        @pl.when(k == 0)
        def _():
            acc_ref[...] = part

        @pl.when(k > 0)
        def _():
            acc_ref[...] += part

        @pl.when(k == nk - 1)
        def _():
            o_ref[...] = acc_ref[...].astype(BF16)

    return pl.pallas_call(
        body, name=name, grid=(mo // tmo, no // tno, nk), in_specs=[a_spec_fn(tk, tmo), b_spec_fn(tk, tno)],
        out_specs=pl.BlockSpec((tmo, tno), lambda m, n, k: (m, n)), out_shape=jax.ShapeDtypeStruct((mo, no), BF16),
        scratch_shapes=[pltpu.VMEM((tmo, tno), F32)], compiler_params=_cp(("parallel", "parallel", "arbitrary")),
    )(a, b)


def _mat_spec(tk, tw):
    return pl.BlockSpec((tk, tw), lambda m, n, k: (k, m))


def _mat_spec_b(tk, tw):
    return pl.BlockSpec((tk, tw), lambda m, n, k: (k, n))


def _plane_spec_b(d):
    def fn(tk, tw):
        npp = d // tw
        return pl.BlockSpec((None, tk, tw), lambda m, n, k: (n // npp, k, n % npp))
    return fn


def _sb_masks(t):
    row = lax.broadcasted_iota(jnp.int32, (t, t), 0)
    col = lax.broadcasted_iota(jnp.int32, (t, t), 1)
    lane = lax.broadcasted_iota(jnp.int32, (1, PAIR), 1)
    return row, col, [(lane // HEAD_DIM) == h for h in (0, 1)]


def _suffix_sums(v, tri, tail, tt):
    hi, lo = _split_bf16(v)
    parts = []
    for b in reversed(range(v.shape[1] // tt)):
        cols = slice(b * tt, (b + 1) * tt)
        parts.insert(0, _dot(hi[:, cols], tri) + _dot(lo[:, cols], tri) + tail)
        tail = tail + jnp.sum(v[:, cols], axis=1, keepdims=True)
    return (parts[0] if len(parts) == 1 else jnp.concatenate(parts, axis=1)), tail


def _sb_tile(qh, kb, valid, after, carry, tt):
    z = _dot_nt(qh, kb)
    lb = -(jnp.maximum(z, 0.0) + jnp.log(1.0 + jnp.exp(-jnp.abs(z))))
    if valid is not None:
        lb = jnp.where(valid, lb, 0.0)
    between, carry = _suffix_sums(lb, after, carry, tt)
    a = jnp.exp(z + lb + between)
    if valid is not None:
        a = jnp.where(valid, a, 0.0)
    return lb, a, carry


def _sb_window_valid(i, first, tt):
    t_pos = i * tt + lax.broadcasted_iota(jnp.int32, (tt, SB_WIN * tt), 0)
    s_pos = first * tt + lax.broadcasted_iota(jnp.int32, (tt, SB_WIN * tt), 1)
    return s_pos < t_pos


def _sb_walk_on(st):
    return jnp.logical_and(st[0] >= 0, jnp.max(st[1]) > SB_EXIT)


def _sb_fwd(qkv, side=None):
    _, t, d = qkv.shape
    hp, tt = d // PAIR, _tile(t, SB_T)

    def body(q_ref, k_ref, v_ref, o_ref, o32_ref):
        row, col, head = _sb_masks(tt)
        after = (row > col).astype(BF16)

        def tile(kb, vb, valid, carry, acc, q2):
            _, a, carry = _sb_tile(q2, kb, valid, after, carry, tt)
            ab = a.astype(BF16)
            for h in (0, 1):
                acc = acc + _dot(ab[h * tt:(h + 1) * tt], jnp.where(head[h], vb, jnp.zeros_like(vb)))
            return carry, acc

        states = []
        for sub in range(SB_SUB):
            i = pl.program_id(1) * SB_SUB + sub
            q = q_ref[sub * tt:(sub + 1) * tt, :]
            q2 = jnp.concatenate([jnp.where(head[h], q, jnp.zeros_like(q)) for h in (0, 1)], axis=0) * jnp.asarray(HEAD_DIM ** -0.5, BF16)
            first = jnp.maximum(i - (SB_WIN - 1), 0)
            rows_w = pl.ds(pl.multiple_of(first * tt, tt), SB_WIN * tt)
            valid = _sb_window_valid(i, first, tt)
            carry, acc = tile(k_ref[rows_w, :], v_ref[rows_w, :], jnp.concatenate([valid, valid], axis=0),
                              jnp.zeros((2 * tt, 1), F32), jnp.zeros((tt, PAIR), F32), q2)
            states.append((first, carry, acc, q2))

        for sub, (first, carry, acc, q2) in enumerate(states):
            def step(st, q2=q2):
                rows = pl.ds(pl.multiple_of(st[0] * tt, tt), tt)
                carry, acc = tile(k_ref[rows, :], v_ref[rows, :], None, st[1], st[2], q2)
                return st[0] - 1, carry, acc

            o = lax.while_loop(_sb_walk_on, step, (first - 1, carry, acc))[2]
            o_ref[sub * tt:(sub + 1) * tt, :] = o.astype(BF16)
            o32_ref[sub * tt:(sub + 1) * tt, :] = o

    tq = SB_SUB * tt
    qs = pl.BlockSpec((None, tq, PAIR), lambda p, i: (0, i, p))
    ks = pl.BlockSpec((None, t, PAIR), lambda p, i: (1, 0, p))
    vs = pl.BlockSpec((None, t, PAIR), lambda p, i: (2, 0, p))
    os_ = pl.BlockSpec((tq, PAIR), lambda p, i: (i, p))
    return _side_call(
        "sb_fwd", body, (hp, t // tq), [qs, ks, vs], (os_, os_),
        (jax.ShapeDtypeStruct((t, d), BF16), jax.ShapeDtypeStruct((t, d), F32)), [], (qkv, qkv, qkv), side,
        ("parallel", "arbitrary"))


def _sb_bwd(qkv, o32, do, side=None):
    _, t, d = qkv.shape
    hp, tt = d // PAIR, _tile(t, SB_T)
    nq = t // (SB_SUB * tt)
    scale = HEAD_DIM ** -0.5

    def body(q_ref, k_ref, v_ref, o32_ref, do_ref, dqkv_ref, dk_acc, dv_acc):
        i = pl.program_id(1)
        row, col, head = _sb_masks(tt)
        after = (row > col).astype(BF16)
        from_s = (row >= col).astype(BF16)

        @pl.when(i == 0)
        def _():
            dk_acc[...] = jnp.zeros_like(dk_acc)
            dv_acc[...] = jnp.zeros_like(dv_acc)

        def tile(rows, valid, carry, seen, dq, q2, do2, tot):
            kb, vb = k_ref[rows, :], v_ref[rows, :]
            lb, a, carry = _sb_tile(q2, kb, valid, after, carry, tt)
            ab = a.astype(BF16)
            g = ab.astype(F32) * _dot_nt(do2, vb)
            g_from, seen = _suffix_sums(g, from_s, seen, tt)
            e = jnp.exp(lb)
            dz = g * e - (1.0 - e) * (tot - g_from)
            if valid is not None:
                dz = jnp.where(valid, dz, 0.0)
            dz = dz.astype(BF16)
            for h in (0, 1):
                dq = dq + _dot(dz[h * tt:(h + 1) * tt], jnp.where(head[h], kb, jnp.zeros_like(kb)))
            dk_acc[rows, :] += _dot_tn(dz, q2)
            dv_acc[rows, :] += _dot_tn(ab, do2)
            return carry, seen, dq

        states = []
        for sub in range(SB_SUB):
            ii = i * SB_SUB + sub
            q = q_ref[sub * tt:(sub + 1) * tt, :]
            dob = do_ref[sub * tt:(sub + 1) * tt, :]
            prod = dob.astype(F32) * o32_ref[sub * tt:(sub + 1) * tt, :]
            zb = jnp.zeros_like(q)
            q2 = jnp.concatenate([jnp.where(head[h], q, zb) for h in (0, 1)], axis=0) * jnp.asarray(scale, BF16)
            do2 = jnp.concatenate([jnp.where(head[h], dob, zb) for h in (0, 1)], axis=0)
            tot = jnp.concatenate([jnp.sum(jnp.where(head[h], prod, 0.0), axis=1, keepdims=True) for h in (0, 1)], axis=0)
            first = jnp.maximum(ii - (SB_WIN - 1), 0)
            valid = _sb_window_valid(ii, first, tt)
            zc = jnp.zeros((2 * tt, 1), F32)
            carry, seen, dq = tile(pl.ds(pl.multiple_of(first * tt, tt), SB_WIN * tt), jnp.concatenate([valid, valid], axis=0),
                                   zc, zc, jnp.zeros((tt, PAIR), F32), q2, do2, tot)
            states.append((first, carry, seen, dq, q2, do2, tot))

        for sub, (first, carry, seen, dq, q2, do2, tot) in enumerate(states):
            def step(st, q2=q2, do2=do2, tot=tot):
                carry, seen, dq = tile(pl.ds(pl.multiple_of(st[0] * tt, tt), tt), None, st[1], st[2], st[3], q2, do2, tot)
                return st[0] - 1, carry, seen, dq

            dq = lax.while_loop(_sb_walk_on, step, (first - 1, carry, seen, dq))[3]
            dqkv_ref[0, pl.ds(pl.multiple_of((i * SB_SUB + sub) * tt, tt), tt), :] = (dq * scale).astype(BF16)

        @pl.when(i == nq - 1)
        def _():
            dqkv_ref[1, :, :] = dk_acc[...].astype(BF16)
            dqkv_ref[2, :, :] = dv_acc[...].astype(BF16)

    qs = pl.BlockSpec((None, SB_SUB * tt, PAIR), lambda p, i: (0, i, p))
    ks = pl.BlockSpec((None, t, PAIR), lambda p, i: (1, 0, p))
    vs = pl.BlockSpec((None, t, PAIR), lambda p, i: (2, 0, p))
    ts = pl.BlockSpec((SB_SUB * tt, PAIR), lambda p, i: (i, p))
    return _side_call(
        "sb_bwd", body, (hp, nq), [qs, ks, vs, ts, ts], [pl.BlockSpec((3, t, PAIR), lambda p, i: (0, 0, p))],
        [jax.ShapeDtypeStruct((3, t, d), BF16)], [pltpu.VMEM((t, PAIR), F32), pltpu.VMEM((t, PAIR), F32)],
        (qkv, qkv, qkv, o32, do), side, ("parallel", "arbitrary"))


def _pair_sum_matrix():
    r = lax.broadcasted_iota(jnp.int32, (PAIR, PAIR), 0) // HEAD_DIM
    c = lax.broadcasted_iota(jnp.int32, (PAIR, PAIR), 1) // HEAD_DIM
    return (r == c).astype(BF16)


def _head_mean(v, ones):
    hi, lo = _split_bf16(v)
    return (_dot(hi, ones) + _dot(lo, ones)) * (1.0 / HEAD_DIM)


def _ca_prep(qkv32, gq, gk):
    _, t, d = qkv32.shape
    tr = _tile(t, CA_PAD)
    assert CA_PAD % tr == 0
    npad = CA_PAD // tr

    def body(x_ref, gq_ref, gk_ref, o_ref):
        p, r = pl.program_id(0), pl.program_id(1)

        @pl.when(r < npad)
        def _():
            o_ref[...] = jnp.zeros_like(o_ref)

        @pl.when(jnp.logical_and(r >= npad, p == 2))
        def _():
            o_ref[...] = x_ref[...].astype(BF16)

        @pl.when(jnp.logical_and(r >= npad, p < 2))
        def _():
            ones = _pair_sum_matrix()
            g = jnp.where(p == 0, gq_ref[...], gk_ref[...])
            for c in range(d // PAIR):
                x = x_ref[:, c * PAIR:(c + 1) * PAIR]
                rs = lax.rsqrt(_head_mean(x * x, ones) + RMS_EPS)
                o_ref[:, c * PAIR:(c + 1) * PAIR] = (x * rs * g).astype(BF16)

    vec = pl.BlockSpec((1, PAIR), lambda p, r: (0, 0))
    return pl.pallas_call(
        body, name="ca_prep", grid=(3, npad + t // tr),
        in_specs=[pl.BlockSpec((None, tr, d), lambda p, r: (p, jnp.maximum(r - npad, 0), 0)), vec, vec],
        out_specs=pl.BlockSpec((None, tr, d), lambda p, r: (p, r, 0)),
        out_shape=jax.ShapeDtypeStruct((3, CA_PAD + t, d), BF16), compiler_params=_cp(("parallel", "parallel")),
    )(qkv32, gq, gk)


def _ca_unprep(dq, dkv, qkv32, gq, gk):
    _, t, d = qkv32.shape
    tr = _tile(t, 512)
    nr = t // tr

    def body(dq_ref, dkv_ref, x_ref, gq_ref, gk_ref, o_ref, dgq_ref, dgk_ref):
        p, r = pl.program_id(0), pl.program_id(1)

        @pl.when(jnp.logical_and(p == 0, r == 0))
        def _():
            dgq_ref[...] = jnp.zeros_like(dgq_ref)
            dgk_ref[...] = jnp.zeros_like(dgk_ref)

        @pl.when(p == 2)
        def _():
            o_ref[...] = dkv_ref[...].astype(BF16)

        @pl.when(p < 2)
        def _():
            ones = _pair_sum_matrix()
            g = jnp.where(p == 0, gq_ref[...], gk_ref[...])
            dg = jnp.zeros((1, PAIR), F32)
            for c in range(d // PAIR):
                cols = slice(c * PAIR, (c + 1) * PAIR)
                x = x_ref[:, cols]
                dy = jnp.where(p == 0, dq_ref[:, cols], dkv_ref[:, cols])
                rs = lax.rsqrt(_head_mean(x * x, ones) + RMS_EPS)
                xhat = x * rs
                dxn = dy * g
                o_ref[:, cols] = (rs * (dxn - xhat * _head_mean(dxn * xhat, ones))).astype(BF16)
                dg = dg + jnp.sum(dy * xhat, axis=0, keepdims=True)
            dg = dg + pltpu.roll(dg, HEAD_DIM, 1)

            @pl.when(p == 0)
            def _():
                dgq_ref[...] += dg

            @pl.when(p == 1)
            def _():
                dgk_ref[...] += dg

    vec = pl.BlockSpec((1, PAIR), lambda p, r: (0, 0))
    return pl.pallas_call(
        body, name="ca_unprep", grid=(3, nr),
        in_specs=[pl.BlockSpec((tr, d), lambda p, r: (r, 0)),
                  pl.BlockSpec((None, tr, d), lambda p, r: (jnp.maximum(p - 1, 0), r, 0)),
                  pl.BlockSpec((None, tr, d), lambda p, r: (p, r, 0)), vec, vec],
        out_specs=(pl.BlockSpec((None, tr, d), lambda p, r: (p, r, 0)), vec, vec),
        out_shape=(jax.ShapeDtypeStruct((3, t, d), BF16), jax.ShapeDtypeStruct((1, PAIR), F32), jax.ShapeDtypeStruct((1, PAIR), F32)),
        compiler_params=_cp(("arbitrary", "arbitrary")),
    )(dq, dkv, qkv32, gq, gk)


def _ca_ext_index():
    m = np.arange(CA_EXT)
    return np.where(m <= CA_WIN, np.clip(CA_PAD - m, -MAX_REL, MAX_REL) + MAX_REL, 2 * MAX_REL).astype(np.int32)


def _ca_pad_penalty(i):
    b = lax.broadcasted_iota(jnp.int32, (1, CA_WIN), 1)
    return jnp.where(b >= CA_PAD - CA_TQ * i, 0.0, NEG_BIG)


def _skew(x, sign):
    row = lax.broadcasted_iota(jnp.int32, (CA_TQ, 1), 0)
    for bit in range(CA_TQ.bit_length() - 1):
        amount = (1 << bit) if sign > 0 else CA_EXT - (1 << bit)
        x = jnp.where(((row >> bit) & 1) == 1, pltpu.roll(x, amount, 1), x)
    return x


def _ca_bias_tiles(ext_ref, bias_ref):
    a = lax.broadcasted_iota(jnp.int32, (CA_TQ, CA_EXT), 0) // CHUNK
    b = lax.broadcasted_iota(jnp.int32, (CA_TQ, CA_EXT), 1) // CHUNK
    seen = jnp.logical_and(b >= a, b <= a + LEFT_CHUNKS)
    for h in (0, 1):
        bias_ref[h] = jnp.where(seen, _skew(jnp.broadcast_to(ext_ref[pl.ds(h, 1), :], (CA_TQ, CA_EXT)), 1), NEG_BIG)


def _ca_probs(qh, kw, bias, penalty):
    z = _dot_nt(qh, kw) + bias + penalty
    p = jnp.exp(z - jnp.max(z, axis=1, keepdims=True))
    return p * (1.0 / jnp.sum(p, axis=1, keepdims=True))


def _ca_fwd(qkvn, ext, side=None):
    _, tp, d = qkvn.shape
    t = tp - CA_PAD
    hp, tq = d // PAIR, CA_SUB * CA_TQ
    npad = CA_PAD // tq

    def body(q_ref, k_ref, v_ref, ext_ref, o_ref, bias_ref):
        i = pl.program_id(1)

        @pl.when(i == 0)
        def _():
            _ca_bias_tiles(ext_ref, bias_ref)

        lane = lax.broadcasted_iota(jnp.int32, (1, PAIR), 1)
        for sub in range(CA_SUB):
            ii = i * CA_SUB + sub
            penalty = _ca_pad_penalty(ii)
            win = pl.ds(pl.multiple_of(ii * CA_TQ, CA_TQ), CA_WIN)
            kw, vw, q = k_ref[win, :], v_ref[win, :], q_ref[sub * CA_TQ:(sub + 1) * CA_TQ, :]
            head = [(lane // HEAD_DIM) == h for h in (0, 1)]
            q2 = jnp.concatenate([jnp.where(hm, q, jnp.zeros_like(q)) for hm in head], axis=0) * jnp.asarray(HEAD_DIM ** -0.5, BF16)
            bias2 = jnp.concatenate([bias_ref[h, :, :CA_WIN] for h in (0, 1)], axis=0)
            pb = _ca_probs(q2, kw, bias2, penalty).astype(BF16)
            o = jnp.zeros((CA_TQ, PAIR), F32)
            for h in (0, 1):
                o = o + _dot(pb[h * CA_TQ:(h + 1) * CA_TQ], jnp.where(head[h], vw, jnp.zeros_like(vw)))
            o_ref[sub * CA_TQ:(sub + 1) * CA_TQ, :] = o.astype(BF16)

    return _side_call(
        "ca_fwd", body, (hp, t // tq),
        [pl.BlockSpec((None, tq, PAIR), lambda p, i: (0, i + npad, p)),
         pl.BlockSpec((None, tp, PAIR), lambda p, i: (1, 0, p)),
         pl.BlockSpec((None, tp, PAIR), lambda p, i: (2, 0, p)),
         pl.BlockSpec((None, 2, CA_EXT), lambda p, i: (p, 0, 0))],
        [pl.BlockSpec((tq, PAIR), lambda p, i: (i, p))], [jax.ShapeDtypeStruct((t, d), BF16)],
        [pltpu.VMEM((2, CA_TQ, CA_EXT), F32)], (qkvn, qkvn, qkvn, ext), side, ("parallel", "arbitrary"))


def _ca_bwd(qkvn, ext, do, side=None):
    _, tp, d = qkvn.shape
    t = tp - CA_PAD
    hp, tq = d // PAIR, CA_SUB * CA_TQ
    npad, nq = CA_PAD // tq, t // tq
    scale = HEAD_DIM ** -0.5

    def body(q_ref, k_ref, v_ref, ext_ref, do_ref, dq_ref, dkv_ref, dext_ref, bias_ref, dbias_ref, dk_acc, dv_acc):
        i = pl.program_id(1)

        @pl.when(i == 0)
        def _():
            _ca_bias_tiles(ext_ref, bias_ref)
            dbias_ref[...] = jnp.zeros_like(dbias_ref)
            dk_acc[...] = jnp.zeros_like(dk_acc)
            dv_acc[...] = jnp.zeros_like(dv_acc)

        lane = lax.broadcasted_iota(jnp.int32, (1, PAIR), 1)
        dbias = [None, None]
        dk_u = dv_u = None

        def spread(v, sub):
            parts = [jnp.zeros((sub * CA_TQ, PAIR), F32)] * (sub > 0) + [v] + \
                    [jnp.zeros(((CA_SUB - 1 - sub) * CA_TQ, PAIR), F32)] * (sub < CA_SUB - 1)
            return jnp.concatenate(parts, axis=0) if len(parts) > 1 else v

        for sub in range(CA_SUB):
            ii = i * CA_SUB + sub
            penalty = _ca_pad_penalty(ii)
            win = pl.ds(pl.multiple_of(ii * CA_TQ, CA_TQ), CA_WIN)
            rows = slice(sub * CA_TQ, (sub + 1) * CA_TQ)
            kw, vw, q, dob = k_ref[win, :], v_ref[win, :], q_ref[rows, :], do_ref[rows, :]
            head = [(lane // HEAD_DIM) == h for h in (0, 1)]
            zb = jnp.zeros_like(q)
            q2 = jnp.concatenate([jnp.where(hm, q, zb) for hm in head], axis=0) * jnp.asarray(scale, BF16)
            do2 = jnp.concatenate([jnp.where(hm, dob, zb) for hm in head], axis=0)
            p = _ca_probs(q2, kw, jnp.concatenate([bias_ref[h, :, :CA_WIN] for h in (0, 1)], axis=0), penalty)
            dp = _dot_nt(do2, vw)
            ds = p * (dp - jnp.sum(p * dp, axis=1, keepdims=True))
            dsb = ds.astype(BF16)
            dq = jnp.zeros((CA_TQ, PAIR), F32)
            for h in (0, 1):
                ds_h = ds[h * CA_TQ:(h + 1) * CA_TQ]
                dbias[h] = ds_h if dbias[h] is None else dbias[h] + ds_h
                dq = dq + _dot(dsb[h * CA_TQ:(h + 1) * CA_TQ], jnp.where(head[h], kw, jnp.zeros_like(kw)))
            dk_t = _dot_tn(dsb, q2)
            dv_t = _dot_tn(p.astype(BF16), do2)
            dq_ref[rows, :] = dq * scale
            dk_u = spread(dk_t, sub) if dk_u is None else dk_u + spread(dk_t, sub)
            dv_u = spread(dv_t, sub) if dv_u is None else dv_u + spread(dv_t, sub)
        union = pl.ds(pl.multiple_of(i * tq, CA_TQ), CA_WIN + (CA_SUB - 1) * CA_TQ)
        dk_acc[union, :] += dk_u
        dv_acc[union, :] += dv_u
        for h in (0, 1):
            dbias_ref[h, :, :CA_WIN] += dbias[h]

        @pl.when(i == nq - 1)
        def _():
            dkv_ref[0, :, :] = dk_acc[CA_PAD:, :]
            dkv_ref[1, :, :] = dv_acc[CA_PAD:, :]
            for h in (0, 1):
                dext_ref[pl.ds(h, 1), :] = jnp.sum(_skew(dbias_ref[h], -1), axis=0, keepdims=True)

    es = pl.BlockSpec((None, 2, CA_EXT), lambda p, i: (p, 0, 0))
    ts = pl.BlockSpec((tq, PAIR), lambda p, i: (i, p))
    return _side_call(
        "ca_bwd", body, (hp, nq),
        [pl.BlockSpec((None, tq, PAIR), lambda p, i: (0, i + npad, p)),
         pl.BlockSpec((None, tp, PAIR), lambda p, i: (1, 0, p)),
         pl.BlockSpec((None, tp, PAIR), lambda p, i: (2, 0, p)), es, ts],
        (ts, pl.BlockSpec((2, t, PAIR), lambda p, i: (0, 0, p)), es),
        (jax.ShapeDtypeStruct((t, d), F32), jax.ShapeDtypeStruct((2, t, d), F32), jax.ShapeDtypeStruct((hp, 2, CA_EXT), F32)),
        [pltpu.VMEM((2, CA_TQ, CA_EXT), F32), pltpu.VMEM((2, CA_TQ, CA_EXT), F32),
         pltpu.VMEM((tp, PAIR), F32), pltpu.VMEM((tp, PAIR), F32)],
        (qkvn, qkvn, qkvn, ext, do), side, ("parallel", "arbitrary"))


def _loss_head(y, target):
    t, d = y.shape
    tm = _tile(t, 512)

    def body(y_ref, t_ref, dy_ref, loss_ref):
        diff = y_ref[...] - t_ref[...]
        dy_ref[...] = diff * (1.0 / d)
        part = 0.5 * jnp.sum(jnp.mean(diff * diff, axis=-1, keepdims=True), axis=0, keepdims=True)

        @pl.when(pl.program_id(0) == 0)
        def _():
            loss_ref[...] = jnp.zeros_like(loss_ref)

        loss_ref[...] += jnp.broadcast_to(part, loss_ref.shape)

    row = pl.BlockSpec((tm, d), lambda m: (m, 0))
    return pl.pallas_call(
        body, name="loss_head", grid=(t // tm,), in_specs=[row, row],
        out_specs=(row, pl.BlockSpec((8, 128), lambda m: (0, 0))),
        out_shape=(jax.ShapeDtypeStruct((t, d), F32), jax.ShapeDtypeStruct((8, 128), F32)),
        compiler_params=_cp(("arbitrary",)),
    )(y, target)


def _adamw_math(w, g, m, v):
    m = ADAM_B1 * m + (1.0 - ADAM_B1) * g
    v = ADAM_B2 * v + (1.0 - ADAM_B2) * (g * g)
    m_hat = m / (1.0 - ADAM_B1 ** ADAM_STEP)
    v_hat = v / (1.0 - ADAM_B2 ** ADAM_STEP)
    delta = -ADAM_LR * (m_hat / (jnp.sqrt(v_hat) + ADAM_EPS) + ADAM_WD * w)
    return delta, m, v


def _adamw(name, w, g, m, v):
    shape = w.shape
    cols = shape[-1]
    rows = int(np.prod(shape[:-1]))
    tr = _tile(rows, 512)
    flat = [a.reshape(rows, cols) for a in (w, g, m, v)]

    def body(w_ref, g_ref, m_ref, v_ref, d_ref, nm_ref, nv_ref):
        d_ref[...], nm_ref[...], nv_ref[...] = _adamw_math(w_ref[...], g_ref[...], m_ref[...], v_ref[...])

    blk = pl.BlockSpec((tr, cols), lambda r: (r, 0))
    outs = pl.pallas_call(
        body, name=name, grid=(rows // tr,), in_specs=[blk] * 4, out_specs=(blk,) * 3,
        out_shape=(jax.ShapeDtypeStruct((rows, cols), F32),) * 3, compiler_params=_cp(("parallel",)),
    )(*flat)
    return [o.reshape(shape) for o in outs]


def _place():
    x, y, c = lax.axis_index("x"), lax.axis_index("y"), lax.axis_index("c")
    chips = [(1 - x, y), (x, 1 - y), (1 - x, 1 - y)]
    return x, y, c, chips


def _shard_slab(ref, kind, layer0, n_layers, shard, width):
    lay = pl.ds(layer0, n_layers)
    if kind == "cols":
        return ref.at[lay, :, pl.ds(shard * width, width)]
    return ref.at[lay, pl.ds(shard * width, width), :]


def _cast_into_place(w, kind, shard):
    n_layers, rows, cols = w.shape
    tr = _tile(rows, 512)
    nr = rows // tr

    def body(s_ref, w_ref, o_ref):
        o_ref[...] = w_ref[...].astype(BF16)

    if kind == "cols":
        full, out = (n_layers, rows, cols * N_CHIPS), pl.BlockSpec((None, tr, cols), lambda l, r, s_ref: (l, r, s_ref[0]))
    else:
        full, out = (n_layers, rows * N_CHIPS, cols), pl.BlockSpec((None, tr, cols), lambda l, r, s_ref: (l, s_ref[0] * nr + r, 0))
    grid_spec = pltpu.PrefetchScalarGridSpec(
        num_scalar_prefetch=1, grid=(n_layers, nr),
        in_specs=[pl.BlockSpec((None, tr, cols), lambda l, r, s_ref: (l, r, 0))], out_specs=out)
    return pl.pallas_call(body, name="cast_into_place", grid_spec=grid_spec, out_shape=jax.ShapeDtypeStruct(full, BF16),
                          compiler_params=_cp(("parallel", "parallel")))(shard, w)


def _gather_side(layer, fulls, kinds, widths):
    na = len(fulls)
    owner = layer % 2

    def parts(w, send, recv):
        x, y, c, chips = _place()

        def copy(k, a, shard, to):
            slab = _shard_slab(w[a], kinds[a], layer, 1, shard, widths[a])
            return pltpu.make_async_remote_copy(src_ref=slab, dst_ref=slab, send_sem=send.at[k], recv_sem=recv.at[k],
                                                device_id=to, device_id_type=MESH)
        return x, y, c, chips, copy

    def start(_, w, send, recv):
        x, y, c, chips, copy = parts(w, send, recv)

        @pl.when(c == owner)
        def _():
            for j, chip in enumerate(chips):
                for a in range(na):
                    copy(j * na + a, a, 2 * x + y, (*chip, c)).start()

    def mid(_, w, send, recv):
        x, y, c, chips, copy = parts(w, send, recv)

        @pl.when(c == owner)
        def _():
            for j, (cx, cy) in enumerate(chips):
                for a in range(na):
                    copy(j * na + a, a, 2 * cx + cy, (x, y, c)).wait_recv()
                    copy(3 * na + j * na + a, a, 2 * cx + cy, (x, y, 1 - c)).start()

    def finish(_, w, send, recv):
        x, y, c, chips, copy = parts(w, send, recv)

        @pl.when(c == owner)
        def _():
            for j, (cx, cy) in enumerate(chips):
                for a in range(na):
                    copy(j * na + a, a, 2 * x + y, (cx, cy, c)).wait_send()
                    copy(3 * na + j * na + a, a, 2 * cx + cy, (x, y, 1 - c)).wait_send()

        @pl.when(c != owner)
        def _():
            for j, (cx, cy) in enumerate(chips):
                for a in range(na):
                    copy(3 * na + j * na + a, a, 2 * cx + cy, (x, y, c)).wait_recv()

    return _Side(fulls, [True] * na, [], 6 * na, 6 * na, start, mid, finish)


def _shard_piece(ref, kind, shard, width):
    return ref.at[:, shard * width:(shard + 1) * width] if kind == "cols" else ref.at[shard * width:(shard + 1) * width, :]


def _reduce_side(layer, grads, kinds, widths):
    na = len(grads)
    owner = layer % 2
    new_outs = []
    for g, k, w in zip(grads, kinds, widths):
        new_outs.append(jax.ShapeDtypeStruct((N_DEV,) + ((g.shape[0], w) if k == "cols" else (w, g.shape[1])), g.dtype))

    def sends(g, got, send, recv, act):
        x, y, c, _ = _place()
        me = 4 * x + 2 * y + c
        for b in range(N_CHIPS):
            bx, by = b >> 1, b & 1
            to_me = jnp.logical_and(jnp.logical_and(x == bx, y == by), c == owner)

            @pl.when(jnp.logical_not(to_me))
            def _(b=b, bx=bx, by=by):
                for a in range(na):
                    cp = pltpu.make_async_remote_copy(
                        src_ref=_shard_piece(g[a], kinds[a], b, widths[a]), dst_ref=got[a].at[me], send_sem=send.at[b * na + a],
                        recv_sem=recv.at[me * na + a], device_id=(bx, by, owner), device_id_type=MESH)
                    cp.start() if act == "start" else cp.wait_send()

    def start(g, got, send, recv):
        sends(g, got, send, recv, "start")

    def mid(g, got, send, recv):
        pass

    def finish(g, got, send, recv):
        sends(g, got, send, recv, "wait")
        x, y, c, _ = _place()
        me = 4 * x + 2 * y + c

        @pl.when(c == owner)
        def _():
            for s in range(N_DEV):
                @pl.when(me != s)
                def _(s=s):
                    for a in range(na):
                        pltpu.make_async_remote_copy(
                            src_ref=_shard_piece(g[a], kinds[a], 0, widths[a]), dst_ref=got[a].at[s], send_sem=send.at[0],
                            recv_sem=recv.at[s * na + a], device_id=(x, y, c), device_id_type=MESH).wait_recv()

    return _Side(grads, [False] * na, new_outs, N_CHIPS * na, N_DEV * na, start, mid, finish)


def _reduce_sum(layer, n_layers, g, got, kind, width, prev, flags):
    _, rows, cols = got.shape
    tr = _tile(rows, 256)
    nr = rows // tr

    def body(f_ref, g_ref, got_ref, *rest):
        o_ref = rest[-1]

        @pl.when(f_ref[2] == 1)
        def _():
            acc = None
            for s in range(N_DEV):
                term = jnp.where(f_ref[1] == s, g_ref[...], got_ref[s]).astype(F32)
                acc = term if acc is None else acc + term
            o_ref[...] = acc

    if kind == "cols":
        own = pl.BlockSpec((tr, width), lambda r, f_ref: (r, f_ref[0]))
    else:
        own = pl.BlockSpec((tr, cols), lambda r, f_ref: (f_ref[0] * nr + r, 0))
    grid_spec = pltpu.PrefetchScalarGridSpec(
        num_scalar_prefetch=1, grid=(nr,),
        in_specs=[own, pl.BlockSpec((N_DEV, tr, cols), lambda r, f_ref: (0, r, 0))] + ([] if prev is None else [ANY]),
        out_specs=pl.BlockSpec((None, tr, cols), lambda r, f_ref: (layer, r, 0)))
    return pl.pallas_call(
        body, name="reduce_sum", grid_spec=grid_spec, out_shape=jax.ShapeDtypeStruct((n_layers, rows, cols), F32),
        input_output_aliases={} if prev is None else {3: 0}, compiler_params=_cp(("arbitrary",)),
    )(flags, g, got, *([] if prev is None else [prev]))


def _pair_share(sums):
    n_layers = sums[0].shape[0]
    na = len(sums)

    def body(*refs):
        dst = refs[na:2 * na]
        send_sems, recv_sems = refs[2 * na:]
        x, y, c, _ = _place()

        def swap(l, a):
            return pltpu.make_async_remote_copy(
                src_ref=dst[a].at[l], dst_ref=dst[a].at[l], send_sem=send_sems.at[l * na + a],
                recv_sem=recv_sems.at[l * na + a], device_id=(x, y, 1 - c), device_id_type=MESH)

        for l in range(n_layers):
            @pl.when(c == l % 2)
            def _(l=l):
                for a in range(na):
                    swap(l, a).start()
        for l in range(n_layers):
            @pl.when(c == l % 2)
            def _(l=l):
                for a in range(na):
                    swap(l, a).wait_send()

            @pl.when(c != l % 2)
            def _(l=l):
                for a in range(na):
                    swap(l, a).wait_recv()

    return pl.pallas_call(
        body, name="pair_share", in_specs=[ANY] * na, out_specs=[ANY] * na,
        out_shape=[jax.ShapeDtypeStruct(s.shape, s.dtype) for s in sums], input_output_aliases={a: a for a in range(na)},
        scratch_shapes=[pltpu.SemaphoreType.DMA((n_layers * na,)), pltpu.SemaphoreType.DMA((n_layers * na,))],
    )(*sums)


def _small_step(g_part, w, m, v):
    r = g_part.shape[0]

    def body(g_ref, w_ref, m_ref, v_ref, go_ref, d_ref, nm_ref, nv_ref, all_ref, send_sems, recv_sems):
        x, y, c, _ = _place()
        me = 4 * x + 2 * y + c
        all_ref[me] = g_ref[...]
        cps = []
        for k in range(1, N_DEV):
            px, py, pc = (x + (k >> 2)) % 2, (y + ((k >> 1) & 1)) % 2, (c + (k & 1)) % 2
            cps.append(pltpu.make_async_remote_copy(src_ref=g_ref, dst_ref=all_ref.at[me], send_sem=send_sems.at[k - 1],
                                                    recv_sem=recv_sems.at[k - 1], device_id=(px, py, pc), device_id_type=MESH))
        for cp in cps:
            cp.start()
        for cp in cps:
            cp.wait()
        g = all_ref[0]
        for k in range(1, N_DEV):
            g = g + all_ref[k]
        go_ref[...] = g
        d_ref[...], nm_ref[...], nv_ref[...] = _adamw_math(w_ref[...], g, m_ref[...], v_ref[...])

    vm = pl.BlockSpec(memory_space=pltpu.VMEM)
    return pl.pallas_call(
        body, name="small_step", in_specs=[vm] * 4, out_specs=[vm] * 4, out_shape=[jax.ShapeDtypeStruct((r, 128), F32)] * 4,
        scratch_shapes=[pltpu.VMEM((N_DEV, r, 128), F32), pltpu.SemaphoreType.DMA((N_DEV - 1,)), pltpu.SemaphoreType.DMA((N_DEV - 1,))],
    )(g_part, w, m, v)


def _onehot_mm(name, a, onehot):
    def body(a_ref, oh_ref, o_ref):
        v = a_ref[...]
        oh = oh_ref[...]
        hi, lo = _split_bf16(v)
        lo2 = (v - hi.astype(F32) - lo.astype(F32)).astype(BF16)
        o_ref[...] = _dot(hi, oh) + _dot(lo, oh) + _dot(lo2, oh)

    return pl.pallas_call(body, name=name, out_shape=jax.ShapeDtypeStruct((a.shape[0], onehot.shape[1]), F32))(a, onehot)


def _pack_small(parts):
    flat = jnp.concatenate([p.reshape(-1) for p in parts])
    n = flat.shape[0]
    rows = -(-n // 128)
    rows = -(-rows // 8) * 8
    return jnp.pad(flat, (0, rows * 128 - n)).reshape(rows, 128)


def _unpack_small(packed, like):
    flat = packed.reshape(-1)
    out, off = [], 0
    for p in like:
        out.append(flat[off:off + p.size].reshape(p.shape))
        off += p.size
    return out


def kernel(x, mix_norm, w_qkv, w_o, q_norm, k_norm, rel_bias, ffn_norm, w_up, w_down, loss_target, m_mix_norm, m_w_qkv, m_w_o, m_q_norm, m_k_norm, m_rel_bias, m_ffn_norm, m_w_up, m_w_down, v_mix_norm, v_w_qkv, v_w_o, v_q_norm, v_k_norm, v_rel_bias, v_ffn_norm, v_w_up, v_w_down):
    n_layers, d = mix_norm.shape
    t = x.shape[1]
    ff = w_down.shape[1] * N_CHIPS
    heads = d // HEAD_DIM
    cx, cy, cc = lax.axis_index("x"), lax.axis_index("y"), lax.axis_index("c")
    shard = (2 * cx + cy).astype(jnp.int32).reshape(1)
    core = cc.astype(jnp.int32).reshape(1)

    big = [w_qkv, w_o, w_up, w_down]
    kinds = ["cols", "rows", "cols", "rows"]
    widths = [w_qkv.shape[2], w_o.shape[1], w_up.shape[2], w_down.shape[1]]
    fulls = _blocking("gather_first", _gather_side(0, [_cast_into_place(w, k, shard) for w, k in zip(big, kinds)], kinds, widths))

    rel_pad = -(-N_REL // PAIR) * PAIR
    ext_hot = _ca_ext_index()[:, None] == np.arange(rel_pad)[None, :]
    fold_hot, spread_hot = jnp.asarray(ext_hot, BF16), jnp.asarray(ext_hot.T, BF16)
    rel_tab = jnp.pad(rel_bias, ((0, 0), (0, 0), (0, rel_pad - N_REL)))
    exts = [_onehot_mm("relbias_spread", rel_tab[i], spread_hot).reshape(heads // 2, 2, CA_EXT) for i in range(n_layers // 2)]

    xs, h1s, qkvs, os_, o32s, x2s, h2s, ss, us, q32s = [], [], [], [], [], [], [], [], [], []
    xc = x[0]
    h = _rms_first(xc, mix_norm[0:1])
    for layer in range(n_layers):
        xs.append(xc)
        h1s.append(h)
        side = _gather_side(layer + 1, fulls, kinds, widths) if layer + 1 < n_layers else None
        if layer % 2 == 0:
            qkv = _qkv_proj(h, fulls[0], layer, BF16)
            (o, o32), gathered = _sb_fwd(qkv, side)
            q32s.append(None)
        else:
            idx = layer // 2
            q32 = _qkv_proj(h, fulls[0], layer, F32)
            gq = jnp.tile(q_norm[idx], 2).reshape(1, PAIR)
            gk = jnp.tile(k_norm[idx], 2).reshape(1, PAIR)
            qkv = _ca_prep(q32, gq, gk)
            (o,), gathered = _ca_fwd(qkv, exts[idx], side)
            o32 = None
            q32s.append(q32)
        fulls = gathered if side is not None else fulls
        wq, wo, wu, wd = fulls
        qkvs.append(qkv)
        os_.append(o)
        o32s.append(o32)
        x2, h2 = _out_proj("attn_out", o, wo, layer, xc, ffn_norm[layer:layer + 1])
        s, u = _up_proj(h2, wu, layer)
        nxt = mix_norm[layer + 1:layer + 2] if layer + 1 < n_layers else mix_norm[0:1]
        xc, h = _out_proj("mlp_out", u, wd, layer, x2, nxt)
        x2s.append(x2)
        h2s.append(h2)
        ss.append(s)
        us.append(u)

    dx, loss_part = _loss_head(xc, loss_target[0])
    loss = lax.psum(loss_part[0, 0], ("x", "y", "c"))

    d_mix, d_ffn = [None] * n_layers, [None] * n_layers
    d_qn, d_kn, d_rb = [], [], []
    device = (4 * cx + 2 * cy + cc).astype(jnp.int32).reshape(1)
    sums = [None] * len(big)

    def reduce_sums(of_layer, grads, got):
        flags = jnp.concatenate([shard, device, (core == of_layer % 2).astype(jnp.int32)])
        return [_reduce_sum(of_layer, n_layers, g, r_, k, w_, s_, flags)
                for g, r_, k, w_, s_ in zip(grads, got, kinds, widths, sums)]

    waiting = None
    for layer in reversed(range(n_layers)):
        du = _down_bwd(dx, wd, layer, ss[layer])
        g_down = _wgrad("wgrad_down", us[layer], _mat_spec, dx, _mat_spec_b, t, ff, d)
        g_up = _wgrad("wgrad_up", h2s[layer], _mat_spec, du, _mat_spec_b, t, d, ff)
        dx2, d_ffn[layer] = _norm_bwd_proj(
            "up_bwd", du, _rows2(ff), [(None, slice(None), slice(None))], wu, layer, x2s[layer], ffn_norm[layer:layer + 1], dx)
        do = _plain_nt("attn_out_bwd", dx2, wo, layer, BF16)
        g_o = _wgrad("wgrad_o", os_[layer], _mat_spec, dx2, _mat_spec_b, t, d, d)
        side = _reduce_side(waiting[0], waiting[1], kinds, widths) if waiting is not None else None
        if layer % 2 == 0:
            (dqkv,), got = _sb_bwd(qkvs[layer], o32s[layer], do, side)
        else:
            idx = layer // 2
            gq = jnp.tile(q_norm[idx], 2).reshape(1, PAIR)
            gk = jnp.tile(k_norm[idx], 2).reshape(1, PAIR)
            (dqn, dkv, dext), got = _ca_bwd(qkvs[layer], exts[idx], do, side)
            dqkv, dgq, dgk = _ca_unprep(dqn, dkv, q32s[layer], gq, gk)
            d_qn.insert(0, dgq[0, :HEAD_DIM])
            d_kn.insert(0, dgk[0, :HEAD_DIM])
            d_rb.insert(0, _onehot_mm("relbias_fold", dext.reshape(heads, CA_EXT), fold_hot)[:, :N_REL])
        if waiting is not None:
            sums = reduce_sums(waiting[0], waiting[1], got)
        g_qkv = _wgrad("wgrad_qkv", h1s[layer], _mat_spec, dqkv, _plane_spec_b(d), t, d, 3 * d, tno=_tile(d, 1024))
        waiting = (layer, [g_qkv, g_o, g_up, g_down])
        dx, d_mix[layer] = _norm_bwd_proj(
            "qkv_bwd", dqkv, lambda tm: pl.BlockSpec((3, tm, d), lambda m: (0, m, 0)),
            [(p, slice(None), slice(p * d, (p + 1) * d)) for p in range(3)], wq, layer, xs[layer], mix_norm[layer:layer + 1], dx2)
    grad_x = dx.reshape(x.shape)
    sums = reduce_sums(waiting[0], waiting[1], _blocking("reduce_last", _reduce_side(waiting[0], waiting[1], kinds, widths)))
    g_big = _pair_share(sums)

    upd = [_adamw("adamw", w_, g_, m_, v_) for w_, g_, m_, v_ in
           zip(big, g_big, [m_w_qkv, m_w_o, m_w_up, m_w_down], [v_w_qkv, v_w_o, v_w_up, v_w_down])]

    small_w = [mix_norm, q_norm, k_norm, rel_bias, ffn_norm]
    small_g = [jnp.concatenate(d_mix, 0), jnp.stack(d_qn), jnp.stack(d_kn), jnp.stack(d_rb), jnp.concatenate(d_ffn, 0)]
    packed = _small_step(_pack_small(small_g), _pack_small(small_w),
                         _pack_small([m_mix_norm, m_q_norm, m_k_norm, m_rel_bias, m_ffn_norm]),
                         _pack_small([v_mix_norm, v_q_norm, v_k_norm, v_rel_bias, v_ffn_norm]))
    sg, sd, sm, sv = [_unpack_small(p, small_w) for p in packed]

    def order(small, bigs):
        return [small[0], bigs[0], bigs[1], small[1], small[2], small[3], small[4], bigs[2], bigs[3]]

    return (loss, grad_x, *order(sg, g_big), *order(sd, [u_[0] for u_ in upd]),
            *order(sm, [u_[1] for u_ in upd]), *order(sv, [u_[2] for u_ in upd]))
```

```python
import functools

import numpy as np
import jax
import jax.numpy as jnp
from jax import lax
from jax.experimental import pallas as pl
from jax.experimental.pallas import tpu as pltpu

F32, BF16 = jnp.float32, jnp.bfloat16
MESH = pl.DeviceIdType.MESH

HEAD_DIM = 64
PAIR = 128
CHUNK = 64
LEFT_CHUNKS = 8
MAX_REL = 256
N_REL = 2 * MAX_REL + 1
CA_TQ = 128
CA_SUB = 2
CA_PAD = LEFT_CHUNKS * CHUNK
CA_WIN = CA_PAD + CA_TQ
CA_EXT = CA_WIN + CA_TQ
SB_T = 128
SB_WIN = 3
SB_SUB = 2
RMS_EPS = 1e-6
SB_EXIT = -104.0
NEG_BIG = -1e30
ADAM_LR, ADAM_B1, ADAM_B2, ADAM_EPS, ADAM_WD, ADAM_STEP = 0.001, 0.9, 0.999, 1e-08, 0.01, 10
VMEM_LIMIT = 56 << 20
N_CHIPS = 4
N_DEV = 8
ANY = pl.BlockSpec(memory_space=pl.ANY)


def _cp(sem=None, **kw):
    return pltpu.CompilerParams(dimension_semantics=sem, vmem_limit_bytes=VMEM_LIMIT, **kw)


def _tile(n, want):
    t = min(n, want)
    assert n % t == 0, (n, t)
    return t


class _Side:
    def __init__(self, ins, aliased, new_outs, n_send, n_recv, start, mid, finish):
        self.ins, self.aliased, self.new_outs = list(ins), list(aliased), list(new_outs)
        self.n_send, self.n_recv, self.start, self.mid, self.finish = n_send, n_recv, start, mid, finish

    def out_shapes(self):
        return [jax.ShapeDtypeStruct(a.shape, a.dtype) for a, al in zip(self.ins, self.aliased) if al] + self.new_outs


def _side_call(name, body, grid, in_specs, out_specs, out_shape, scratch_shapes, args, side, sem):
    out_shape, out_specs = list(out_shape), list(out_specs)
    if side is None:
        outs = pl.pallas_call(body, name=name, grid=grid, in_specs=in_specs, out_specs=out_specs, out_shape=out_shape,
                              scratch_shapes=scratch_shapes, compiler_params=_cp(sem))(*args)
        return list(outs), []
    n_in, n_out, n_sc, n_sin = len(in_specs), len(out_shape), len(scratch_shapes), len(side.ins)
    s_outs = side.out_shapes()

    def wrapped(*refs):
        ins, s_in = refs[:n_in], refs[n_in:n_in + n_sin]
        outs = refs[n_in + n_sin:n_in + n_sin + n_out]
        s_out = refs[n_in + n_sin + n_out:n_in + n_sin + n_out + len(s_outs)]
        scr = refs[n_in + n_sin + n_out + len(s_outs):]
        send, recv = scr[n_sc], scr[n_sc + 1]
        ids = [pl.program_id(k) for k in range(len(grid))]

        def at(first_axis, rest):
            ok = ids[0] == first_axis
            for k in range(1, len(grid)):
                ok = jnp.logical_and(ok, ids[k] == (grid[k] - 1 if rest == "last" else 0))
            return ok

        @pl.when(at(0, "first"))
        def _():
            side.start(s_in, s_out, send, recv)

        body(*ins, *outs, *scr[:n_sc])

        @pl.when(at(grid[0] // 2, "first"))
        def _():
            side.mid(s_in, s_out, send, recv)

        @pl.when(at(grid[0] - 1, "last"))
        def _():
            side.finish(s_in, s_out, send, recv)

    aliases, pos = {}, n_out
    for k, al in enumerate(side.aliased):
        if al:
            aliases[n_in + k] = pos
            pos += 1
    outs = pl.pallas_call(
        wrapped, name=name, grid=grid, in_specs=list(in_specs) + [ANY] * n_sin, out_specs=out_specs + [ANY] * len(s_outs),
        out_shape=out_shape + s_outs, input_output_aliases=aliases,
        scratch_shapes=list(scratch_shapes) + [pltpu.SemaphoreType.DMA((side.n_send,)), pltpu.SemaphoreType.DMA((side.n_recv,))],
        compiler_params=_cp(("arbitrary",) * len(grid)),
    )(*args, *side.ins)
    return list(outs[:n_out]), list(outs[n_out:])


def _blocking(name, side):
    s_outs = side.out_shapes()
    n_sin = len(side.ins)

    def body(*refs):
        s_in, s_out = refs[:n_sin], refs[n_sin:n_sin + len(s_outs)]
        send, recv = refs[n_sin + len(s_outs):]
        side.start(s_in, s_out, send, recv)
        side.mid(s_in, s_out, send, recv)
        side.finish(s_in, s_out, send, recv)

    aliases, pos = {}, 0
    for k, al in enumerate(side.aliased):
        if al:
            aliases[k] = pos
            pos += 1
    return list(pl.pallas_call(
        body, name=name, in_specs=[ANY] * n_sin, out_specs=[ANY] * len(s_outs), out_shape=s_outs, input_output_aliases=aliases,
        scratch_shapes=[pltpu.SemaphoreType.DMA((side.n_send,)), pltpu.SemaphoreType.DMA((side.n_recv,))],
    )(*side.ins))


def _split_bf16(v):
    hi = v.astype(BF16)
    lo = (v - hi.astype(F32)).astype(BF16)
    return hi, lo


def _dot(a, b):
    return jnp.dot(a, b, preferred_element_type=F32)


def _dot_nt(a, b):
    return lax.dot_general(a, b, (((1,), (1,)), ((), ())), preferred_element_type=F32)


def _dot_tn(a, b):
    return lax.dot_general(a, b, (((0,), (0,)), ((), ())), preferred_element_type=F32)


TM = 512
TN = 1024


def _row_matmul(name, a, a_spec, w, layer, nt, chunks, extra, extra_specs, out_shape, out_specs, epilogue, t,
                sem=("parallel",), side=None):
    tm = _tile(t, TM)
    n_extra = len(extra)

    def body(a_ref, w_ref, *rest):
        ins, outs = rest[:n_extra], rest[n_extra:]
        for n, terms in enumerate(chunks):
            acc = None
            for plane, rows, cols in terms:
                av = (a_ref[...] if plane is None else a_ref[plane]).astype(BF16)
                part = _dot_nt(av, w_ref[rows, cols]) if nt else _dot(av, w_ref[rows, cols])
                acc = part if acc is None else acc + part
            epilogue(n, acc, ins, outs)

    w_spec = pl.BlockSpec((None,) + w.shape[1:], lambda m: (layer, 0, 0))
    outs, side_outs = _side_call(name, body, (t // tm,), [a_spec(tm), w_spec, *extra_specs], out_specs, out_shape, [],
                                 (a, w, *extra), side, sem)
    return outs if side is None else (outs, side_outs)


def _rows2(width):
    return lambda tm: pl.BlockSpec((tm, width), lambda m: (m, 0))


def _col_chunks(n, width=None):
    tn = _tile(n, TN)
    return [[(None, slice(None), slice(c * tn, (c + 1) * tn))] for c in range(n // tn)], tn


def _rms_rows(x, gain):
    r = lax.rsqrt(jnp.mean(x * x, axis=-1, keepdims=True) + RMS_EPS)
    return x * r * gain


def _rms_bwd_rows(dh, x, gain, dres):
    r = lax.rsqrt(jnp.mean(x * x, axis=-1, keepdims=True) + RMS_EPS)
    xhat = x * r
    dxn = dh * gain
    dx = r * (dxn - xhat * jnp.mean(dxn * xhat, axis=-1, keepdims=True))
    return dx + dres, jnp.sum(dh * xhat, axis=0, keepdims=True)


def _rms_first(x, gain):
    t, d = x.shape
    tm = _tile(t, 512)

    def body(x_ref, g_ref, h_ref):
        h_ref[...] = _rms_rows(x_ref[...], g_ref[...]).astype(BF16)

    return pl.pallas_call(
        body, name="rms_first", grid=(t // tm,),
        in_specs=[pl.BlockSpec((tm, d), lambda m: (m, 0)), pl.BlockSpec((1, d), lambda m: (0, 0))],
        out_specs=pl.BlockSpec((tm, d), lambda m: (m, 0)), out_shape=jax.ShapeDtypeStruct((t, d), BF16),
        compiler_params=_cp(("parallel",)),
    )(x, gain)


def _qkv_proj(h, w, layer, out_dtype):
    t, d = h.shape
    chunks = [[(None, slice(None), slice(p * d, (p + 1) * d))] for p in range(3)]

    def epi(n, acc, ins, outs):
        outs[0][n] = acc.astype(out_dtype)

    return _row_matmul(
        "qkv_proj", h, _rows2(d), w, layer, False, chunks, (), (), (jax.ShapeDtypeStruct((3, t, d), out_dtype),),
        (pl.BlockSpec((3, _tile(t, TM), d), lambda m: (0, m, 0)),), epi, t)[0]


def _out_proj(name, a, w, layer, res, gain):
    t, kk = a.shape
    d = res.shape[1]

    def epi(n, acc, ins, outs):
        xn = ins[0][...] + acc
        outs[0][...] = xn
        outs[1][...] = _rms_rows(xn, ins[1][...]).astype(BF16)

    row = _rows2(d)(_tile(t, TM))
    return _row_matmul(
        name, a, _rows2(kk), w, layer, False, [[(None, slice(None), slice(None))]], (res, gain),
        (row, pl.BlockSpec((1, d), lambda m: (0, 0))),
        (jax.ShapeDtypeStruct((t, d), F32), jax.ShapeDtypeStruct((t, d), BF16)), (row, row), epi, t)


def _up_proj(h, w, layer):
    t, d = h.shape
    ff = w.shape[2]
    chunks, tn = _col_chunks(ff)

    def epi(n, acc, ins, outs):
        s = jnp.maximum(acc, 0.0)
        outs[0][:, n * tn:(n + 1) * tn] = s.astype(BF16)
        outs[1][:, n * tn:(n + 1) * tn] = (s * s).astype(BF16)

    o = _rows2(ff)(_tile(t, TM))
    return _row_matmul("up_proj", h, _rows2(d), w, layer, False, chunks, (), (), (jax.ShapeDtypeStruct((t, ff), BF16),) * 2, (o, o), epi, t)


def _down_bwd(dx, w, layer, s):
    t, d = dx.shape
    ff = w.shape[1]
    tn = _tile(ff, TN)
    chunks = [[(None, slice(c * tn, (c + 1) * tn), slice(None))] for c in range(ff // tn)]

    def epi(n, acc, ins, outs):
        cols = slice(n * tn, (n + 1) * tn)
        outs[0][:, cols] = (acc * (2.0 * ins[0][:, cols].astype(F32))).astype(BF16)

    o = _rows2(ff)(_tile(t, TM))
    return _row_matmul("down_bwd", dx, _rows2(d), w, layer, True, chunks, (s,), (o,), (jax.ShapeDtypeStruct((t, ff), BF16),), (o,), epi, t)[0]


def _norm_bwd_proj(name, a, a_spec, terms, w, layer, x, gain, dres, side=None):
    t, d = x.shape

    def epi(n, acc, ins, outs):
        dx, dg = _rms_bwd_rows(acc, ins[0][...], ins[1][...], ins[2][...])
        outs[0][...] = dx

        @pl.when(pl.program_id(0) == 0)
        def _():
            outs[1][...] = dg

        @pl.when(pl.program_id(0) > 0)
        def _():
            outs[1][...] += dg

    row = _rows2(d)(_tile(t, TM))
    vec = pl.BlockSpec((1, d), lambda m: (0, 0))
    return _row_matmul(
        name, a, a_spec, w, layer, True, [terms], (x, gain, dres), (row, vec, row),
        (jax.ShapeDtypeStruct((t, d), F32), jax.ShapeDtypeStruct((1, d), F32)), (row, vec), epi, t, sem=("arbitrary",), side=side)


def _plain_nt(name, a, w, layer, out_dtype):
    t, n = a.shape
    m_out = w.shape[1]

    def epi(n_, acc, ins, outs):
        outs[0][...] = acc.astype(out_dtype)

    return _row_matmul(
        name, a, _rows2(n), w, layer, True, [[(None, slice(None), slice(None))]], (), (),
        (jax.ShapeDtypeStruct((t, m_out), out_dtype),), (_rows2(m_out)(_tile(t, TM)),), epi, t)[0]


def _wgrad(name, a, a_spec_fn, b, b_spec_fn, t, mo, no, tno=1024):
    tmo, tno, tk = _tile(mo, 1024), _tile(no, tno), _tile(t, 2048)
    nk = t // tk

    def body(a_ref, b_ref, o_ref, acc_ref):
        part = _dot_tn(a_ref[...].astype(BF16), b_ref[...].astype(BF16))
        k = pl.program_id(2)

        @pl.when(k == 0)
        def _():
            acc_ref[...] = part

        @pl.when(k > 0)
        def _():
            acc_ref[...] += part

        @pl.when(k == nk - 1)
        def _():
            o_ref[...] = acc_ref[...].astype(BF16)

    return pl.pallas_call(
        body, name=name, grid=(mo // tmo, no // tno, nk), in_specs=[a_spec_fn(tk, tmo), b_spec_fn(tk, tno)],
        out_specs=pl.BlockSpec((tmo, tno), lambda m, n, k: (m, n)), out_shape=jax.ShapeDtypeStruct((mo, no), BF16),
        scratch_shapes=[pltpu.VMEM((tmo, tno), F32)], compiler_params=_cp(("parallel", "parallel", "arbitrary")),
    )(a, b)


def _mat_spec(tk, tw):
    return pl.BlockSpec((tk, tw), lambda m, n, k: (k, m))


def _mat_spec_b(tk, tw):
    return pl.BlockSpec((tk, tw), lambda m, n, k: (k, n))


def _plane_spec_b(d):
    def fn(tk, tw):
        npp = d // tw
        return pl.BlockSpec((None, tk, tw), lambda m, n, k: (n // npp, k, n % npp))
    return fn


def _sb_masks(t):
    row = lax.broadcasted_iota(jnp.int32, (t, t), 0)
    col = lax.broadcasted_iota(jnp.int32, (t, t), 1)
    lane = lax.broadcasted_iota(jnp.int32, (1, PAIR), 1)
    return row, col, [(lane // HEAD_DIM) == h for h in (0, 1)]


def _suffix_sums(v, tri, tail, tt):
    hi, lo = _split_bf16(v)
    parts = []
    for b in reversed(range(v.shape[1] // tt)):
        cols = slice(b * tt, (b + 1) * tt)
        parts.insert(0, _dot(hi[:, cols], tri) + _dot(lo[:, cols], tri) + tail)
        tail = tail + jnp.sum(v[:, cols], axis=1, keepdims=True)
    return (parts[0] if len(parts) == 1 else jnp.concatenate(parts, axis=1)), tail


def _sb_tile(qh, kb, valid, after, carry, tt):
    z = _dot_nt(qh, kb)
    lb = -(jnp.maximum(z, 0.0) + jnp.log(1.0 + jnp.exp(-jnp.abs(z))))
    if valid is not None:
        lb = jnp.where(valid, lb, 0.0)
    between, carry = _suffix_sums(lb, after, carry, tt)
    a = jnp.exp(z + lb + between)
    if valid is not None:
        a = jnp.where(valid, a, 0.0)
    return lb, a, carry


def _sb_window_valid(i, first, tt):
    t_pos = i * tt + lax.broadcasted_iota(jnp.int32, (tt, SB_WIN * tt), 0)
    s_pos = first * tt + lax.broadcasted_iota(jnp.int32, (tt, SB_WIN * tt), 1)
    return s_pos < t_pos


def _sb_walk_on(st):
    return jnp.logical_and(st[0] >= 0, jnp.max(st[1]) > SB_EXIT)


def _sb_fwd(qkv, side=None):
    _, t, d = qkv.shape
    hp, tt = d // PAIR, _tile(t, SB_T)

    def body(q_ref, k_ref, v_ref, o_ref, o32_ref):
        row, col, head = _sb_masks(tt)
        after = (row > col).astype(BF16)

        def tile(kb, vb, valid, carry, acc, q2):
            _, a, carry = _sb_tile(q2, kb, valid, after, carry, tt)
            ab = a.astype(BF16)
            for h in (0, 1):
                acc = acc + _dot(ab[h * tt:(h + 1) * tt], jnp.where(head[h], vb, jnp.zeros_like(vb)))
            return carry, acc

        states = []
        for sub in range(SB_SUB):
            i = pl.program_id(1) * SB_SUB + sub
            q = q_ref[sub * tt:(sub + 1) * tt, :]
            q2 = jnp.concatenate([jnp.where(head[h], q, jnp.zeros_like(q)) for h in (0, 1)], axis=0) * jnp.asarray(HEAD_DIM ** -0.5, BF16)
            first = jnp.maximum(i - (SB_WIN - 1), 0)
            rows_w = pl.ds(pl.multiple_of(first * tt, tt), SB_WIN * tt)
            valid = _sb_window_valid(i, first, tt)
            carry, acc = tile(k_ref[rows_w, :], v_ref[rows_w, :], jnp.concatenate([valid, valid], axis=0),
                              jnp.zeros((2 * tt, 1), F32), jnp.zeros((tt, PAIR), F32), q2)
            states.append((first, carry, acc, q2))

        for sub, (first, carry, acc, q2) in enumerate(states):
            def step(st, q2=q2):
                rows = pl.ds(pl.multiple_of(st[0] * tt, tt), tt)
                carry, acc = tile(k_ref[rows, :], v_ref[rows, :], None, st[1], st[2], q2)
                return st[0] - 1, carry, acc

            o = lax.while_loop(_sb_walk_on, step, (first - 1, carry, acc))[2]
            o_ref[sub * tt:(sub + 1) * tt, :] = o.astype(BF16)
            o32_ref[sub * tt:(sub + 1) * tt, :] = o

    tq = SB_SUB * tt
    qs = pl.BlockSpec((None, tq, PAIR), lambda p, i: (0, i, p))
    ks = pl.BlockSpec((None, t, PAIR), lambda p, i: (1, 0, p))
    vs = pl.BlockSpec((None, t, PAIR), lambda p, i: (2, 0, p))
    os_ = pl.BlockSpec((tq, PAIR), lambda p, i: (i, p))
    return _side_call(
        "sb_fwd", body, (hp, t // tq), [qs, ks, vs], (os_, os_),
        (jax.ShapeDtypeStruct((t, d), BF16), jax.ShapeDtypeStruct((t, d), F32)), [], (qkv, qkv, qkv), side,
        ("parallel", "arbitrary"))


def _sb_bwd(qkv, o32, do, side=None):
    _, t, d = qkv.shape
    hp, tt = d // PAIR, _tile(t, SB_T)
    nq = t // (SB_SUB * tt)
    scale = HEAD_DIM ** -0.5

    def body(q_ref, k_ref, v_ref, o32_ref, do_ref, dqkv_ref, dk_acc, dv_acc):
        i = pl.program_id(1)
        row, col, head = _sb_masks(tt)
        after = (row > col).astype(BF16)
        from_s = (row >= col).astype(BF16)

        @pl.when(i == 0)
        def _():
            dk_acc[...] = jnp.zeros_like(dk_acc)
            dv_acc[...] = jnp.zeros_like(dv_acc)

        def tile(rows, valid, carry, seen, dq, q2, do2, tot):
            kb, vb = k_ref[rows, :], v_ref[rows, :]
            lb, a, carry = _sb_tile(q2, kb, valid, after, carry, tt)
            ab = a.astype(BF16)
            g = ab.astype(F32) * _dot_nt(do2, vb)
            g_from, seen = _suffix_sums(g, from_s, seen, tt)
            e = jnp.exp(lb)
            dz = g * e - (1.0 - e) * (tot - g_from)
            if valid is not None:
                dz = jnp.where(valid, dz, 0.0)
            dz = dz.astype(BF16)
            for h in (0, 1):
                dq = dq + _dot(dz[h * tt:(h + 1) * tt], jnp.where(head[h], kb, jnp.zeros_like(kb)))
            dk_acc[rows, :] += _dot_tn(dz, q2)
            dv_acc[rows, :] += _dot_tn(ab, do2)
            return carry, seen, dq

        states = []
        for sub in range(SB_SUB):
            ii = i * SB_SUB + sub
            q = q_ref[sub * tt:(sub + 1) * tt, :]
            dob = do_ref[sub * tt:(sub + 1) * tt, :]
            prod = dob.astype(F32) * o32_ref[sub * tt:(sub + 1) * tt, :]
            zb = jnp.zeros_like(q)
            q2 = jnp.concatenate([jnp.where(head[h], q, zb) for h in (0, 1)], axis=0) * jnp.asarray(scale, BF16)
            do2 = jnp.concatenate([jnp.where(head[h], dob, zb) for h in (0, 1)], axis=0)
            tot = jnp.concatenate([jnp.sum(jnp.where(head[h], prod, 0.0), axis=1, keepdims=True) for h in (0, 1)], axis=0)
            first = jnp.maximum(ii - (SB_WIN - 1), 0)
            valid = _sb_window_valid(ii, first, tt)
            zc = jnp.zeros((2 * tt, 1), F32)
            carry, seen, dq = tile(pl.ds(pl.multiple_of(first * tt, tt), SB_WIN * tt), jnp.concatenate([valid, valid], axis=0),
                                   zc, zc, jnp.zeros((tt, PAIR), F32), q2, do2, tot)
            states.append((first, carry, seen, dq, q2, do2, tot))

        for sub, (first, carry, seen, dq, q2, do2, tot) in enumerate(states):
            def step(st, q2=q2, do2=do2, tot=tot):
                carry, seen, dq = tile(pl.ds(pl.multiple_of(st[0] * tt, tt), tt), None, st[1], st[2], st[3], q2, do2, tot)
                return st[0] - 1, carry, seen, dq

            dq = lax.while_loop(_sb_walk_on, step, (first - 1, carry, seen, dq))[3]
            dqkv_ref[0, pl.ds(pl.multiple_of((i * SB_SUB + sub) * tt, tt), tt), :] = (dq * scale).astype(BF16)

        @pl.when(i == nq - 1)
        def _():
            dqkv_ref[1, :, :] = dk_acc[...].astype(BF16)
            dqkv_ref[2, :, :] = dv_acc[...].astype(BF16)

    qs = pl.BlockSpec((None, SB_SUB * tt, PAIR), lambda p, i: (0, i, p))
    ks = pl.BlockSpec((None, t, PAIR), lambda p, i: (1, 0, p))
    vs = pl.BlockSpec((None, t, PAIR), lambda p, i: (2, 0, p))
    ts = pl.BlockSpec((SB_SUB * tt, PAIR), lambda p, i: (i, p))
    return _side_call(
        "sb_bwd", body, (hp, nq), [qs, ks, vs, ts, ts], [pl.BlockSpec((3, t, PAIR), lambda p, i: (0, 0, p))],
        [jax.ShapeDtypeStruct((3, t, d), BF16)], [pltpu.VMEM((t, PAIR), F32), pltpu.VMEM((t, PAIR), F32)],
        (qkv, qkv, qkv, o32, do), side, ("parallel", "arbitrary"))


def _pair_sum_matrix():
    r = lax.broadcasted_iota(jnp.int32, (PAIR, PAIR), 0) // HEAD_DIM
    c = lax.broadcasted_iota(jnp.int32, (PAIR, PAIR), 1) // HEAD_DIM
    return (r == c).astype(BF16)


def _head_mean(v, ones):
    hi, lo = _split_bf16(v)
    return (_dot(hi, ones) + _dot(lo, ones)) * (1.0 / HEAD_DIM)


def _ca_prep(qkv32, gq, gk):
    _, t, d = qkv32.shape
    tr = _tile(t, CA_PAD)
    assert CA_PAD % tr == 0
    npad = CA_PAD // tr

    def body(x_ref, gq_ref, gk_ref, o_ref):
        p, r = pl.program_id(0), pl.program_id(1)

        @pl.when(r < npad)
        def _():
            o_ref[...] = jnp.zeros_like(o_ref)

        @pl.when(jnp.logical_and(r >= npad, p == 2))
        def _():
            o_ref[...] = x_ref[...].astype(BF16)

        @pl.when(jnp.logical_and(r >= npad, p < 2))
        def _():
            ones = _pair_sum_matrix()
            g = jnp.where(p == 0, gq_ref[...], gk_ref[...])
            for c in range(d // PAIR):
                x = x_ref[:, c * PAIR:(c + 1) * PAIR]
                rs = lax.rsqrt(_head_mean(x * x, ones) + RMS_EPS)
                o_ref[:, c * PAIR:(c + 1) * PAIR] = (x * rs * g).astype(BF16)

    vec = pl.BlockSpec((1, PAIR), lambda p, r: (0, 0))
    return pl.pallas_call(
        body, name="ca_prep", grid=(3, npad + t // tr),
        in_specs=[pl.BlockSpec((None, tr, d), lambda p, r: (p, jnp.maximum(r - npad, 0), 0)), vec, vec],
        out_specs=pl.BlockSpec((None, tr, d), lambda p, r: (p, r, 0)),
        out_shape=jax.ShapeDtypeStruct((3, CA_PAD + t, d), BF16), compiler_params=_cp(("parallel", "parallel")),
    )(qkv32, gq, gk)


def _ca_unprep(dq, dkv, qkv32, gq, gk):
    _, t, d = qkv32.shape
    tr = _tile(t, 512)
    nr = t // tr

    def body(dq_ref, dkv_ref, x_ref, gq_ref, gk_ref, o_ref, dgq_ref, dgk_ref):
        p, r = pl.program_id(0), pl.program_id(1)

        @pl.when(jnp.logical_and(p == 0, r == 0))
        def _():
            dgq_ref[...] = jnp.zeros_like(dgq_ref)
            dgk_ref[...] = jnp.zeros_like(dgk_ref)

        @pl.when(p == 2)
        def _():
            o_ref[...] = dkv_ref[...].astype(BF16)

        @pl.when(p < 2)
        def _():
            ones = _pair_sum_matrix()
            g = jnp.where(p == 0, gq_ref[...], gk_ref[...])
            dg = jnp.zeros((1, PAIR), F32)
            for c in range(d // PAIR):
                cols = slice(c * PAIR, (c + 1) * PAIR)
                x = x_ref[:, cols]
                dy = jnp.where(p == 0, dq_ref[:, cols], dkv_ref[:, cols])
                rs = lax.rsqrt(_head_mean(x * x, ones) + RMS_EPS)
                xhat = x * rs
                dxn = dy * g
                o_ref[:, cols] = (rs * (dxn - xhat * _head_mean(dxn * xhat, ones))).astype(BF16)
                dg = dg + jnp.sum(dy * xhat, axis=0, keepdims=True)
            dg = dg + pltpu.roll(dg, HEAD_DIM, 1)

            @pl.when(p == 0)
            def _():
                dgq_ref[...] += dg

            @pl.when(p == 1)
            def _():
                dgk_ref[...] += dg

    vec = pl.BlockSpec((1, PAIR), lambda p, r: (0, 0))
    return pl.pallas_call(
        body, name="ca_unprep", grid=(3, nr),
        in_specs=[pl.BlockSpec((tr, d), lambda p, r: (r, 0)),
                  pl.BlockSpec((None, tr, d), lambda p, r: (jnp.maximum(p - 1, 0), r, 0)),
                  pl.BlockSpec((None, tr, d), lambda p, r: (p, r, 0)), vec, vec],
        out_specs=(pl.BlockSpec((None, tr, d), lambda p, r: (p, r, 0)), vec, vec),
        out_shape=(jax.ShapeDtypeStruct((3, t, d), BF16), jax.ShapeDtypeStruct((1, PAIR), F32), jax.ShapeDtypeStruct((1, PAIR), F32)),
        compiler_params=_cp(("arbitrary", "arbitrary")),
    )(dq, dkv, qkv32, gq, gk)


def _ca_ext_index():
    m = np.arange(CA_EXT)
    return np.where(m <= CA_WIN, np.clip(CA_PAD - m, -MAX_REL, MAX_REL) + MAX_REL, 2 * MAX_REL).astype(np.int32)


def _ca_pad_penalty(i):
    b = lax.broadcasted_iota(jnp.int32, (1, CA_WIN), 1)
    return jnp.where(b >= CA_PAD - CA_TQ * i, 0.0, NEG_BIG)


def _skew(x, sign):
    row = lax.broadcasted_iota(jnp.int32, (CA_TQ, 1), 0)
    for bit in range(CA_TQ.bit_length() - 1):
        amount = (1 << bit) if sign > 0 else CA_EXT - (1 << bit)
        x = jnp.where(((row >> bit) & 1) == 1, pltpu.roll(x, amount, 1), x)
    return x


def _ca_bias_tiles(ext_ref, bias_ref):
    a = lax.broadcasted_iota(jnp.int32, (CA_TQ, CA_EXT), 0) // CHUNK
    b = lax.broadcasted_iota(jnp.int32, (CA_TQ, CA_EXT), 1) // CHUNK
    seen = jnp.logical_and(b >= a, b <= a + LEFT_CHUNKS)
    for h in (0, 1):
        bias_ref[h] = jnp.where(seen, _skew(jnp.broadcast_to(ext_ref[pl.ds(h, 1), :], (CA_TQ, CA_EXT)), 1), NEG_BIG)


def _ca_probs(qh, kw, bias, penalty):
    z = _dot_nt(qh, kw) + bias + penalty
    p = jnp.exp(z - jnp.max(z, axis=1, keepdims=True))
    return p * (1.0 / jnp.sum(p, axis=1, keepdims=True))


def _ca_fwd(qkvn, ext, side=None):
    _, tp, d = qkvn.shape
    t = tp - CA_PAD
    hp, tq = d // PAIR, CA_SUB * CA_TQ
    npad = CA_PAD // tq

    def body(q_ref, k_ref, v_ref, ext_ref, o_ref, bias_ref):
        i = pl.program_id(1)

        @pl.when(i == 0)
        def _():
            _ca_bias_tiles(ext_ref, bias_ref)

        lane = lax.broadcasted_iota(jnp.int32, (1, PAIR), 1)
        for sub in range(CA_SUB):
            ii = i * CA_SUB + sub
            penalty = _ca_pad_penalty(ii)
            win = pl.ds(pl.multiple_of(ii * CA_TQ, CA_TQ), CA_WIN)
            kw, vw, q = k_ref[win, :], v_ref[win, :], q_ref[sub * CA_TQ:(sub + 1) * CA_TQ, :]
            head = [(lane // HEAD_DIM) == h for h in (0, 1)]
            q2 = jnp.concatenate([jnp.where(hm, q, jnp.zeros_like(q)) for hm in head], axis=0) * jnp.asarray(HEAD_DIM ** -0.5, BF16)
            bias2 = jnp.concatenate([bias_ref[h, :, :CA_WIN] for h in (0, 1)], axis=0)
            pb = _ca_probs(q2, kw, bias2, penalty).astype(BF16)
            o = jnp.zeros((CA_TQ, PAIR), F32)
            for h in (0, 1):
                o = o + _dot(pb[h * CA_TQ:(h + 1) * CA_TQ], jnp.where(head[h], vw, jnp.zeros_like(vw)))
            o_ref[sub * CA_TQ:(sub + 1) * CA_TQ, :] = o.astype(BF16)

    return _side_call(
        "ca_fwd", body, (hp, t // tq),
        [pl.BlockSpec((None, tq, PAIR), lambda p, i: (0, i + npad, p)),
         pl.BlockSpec((None, tp, PAIR), lambda p, i: (1, 0, p)),
         pl.BlockSpec((None, tp, PAIR), lambda p, i: (2, 0, p)),
         pl.BlockSpec((None, 2, CA_EXT), lambda p, i: (p, 0, 0))],
        [pl.BlockSpec((tq, PAIR), lambda p, i: (i, p))], [jax.ShapeDtypeStruct((t, d), BF16)],
        [pltpu.VMEM((2, CA_TQ, CA_EXT), F32)], (qkvn, qkvn, qkvn, ext), side, ("parallel", "arbitrary"))


def _ca_bwd(qkvn, ext, do, side=None):
    _, tp, d = qkvn.shape
    t = tp - CA_PAD
    hp, tq = d // PAIR, CA_SUB * CA_TQ
    npad, nq = CA_PAD // tq, t // tq
    scale = HEAD_DIM ** -0.5

    def body(q_ref, k_ref, v_ref, ext_ref, do_ref, dq_ref, dkv_ref, dext_ref, bias_ref, dbias_ref, dk_acc, dv_acc):
        i = pl.program_id(1)

        @pl.when(i == 0)
        def _():
            _ca_bias_tiles(ext_ref, bias_ref)
            dbias_ref[...] = jnp.zeros_like(dbias_ref)
            dk_acc[...] = jnp.zeros_like(dk_acc)
            dv_acc[...] = jnp.zeros_like(dv_acc)

        lane = lax.broadcasted_iota(jnp.int32, (1, PAIR), 1)
        dbias = [None, None]
        dk_u = dv_u = None

        def spread(v, sub):
            parts = [jnp.zeros((sub * CA_TQ, PAIR), F32)] * (sub > 0) + [v] + \
                    [jnp.zeros(((CA_SUB - 1 - sub) * CA_TQ, PAIR), F32)] * (sub < CA_SUB - 1)
            return jnp.concatenate(parts, axis=0) if len(parts) > 1 else v

        for sub in range(CA_SUB):
            ii = i * CA_SUB + sub
            penalty = _ca_pad_penalty(ii)
            win = pl.ds(pl.multiple_of(ii * CA_TQ, CA_TQ), CA_WIN)
            rows = slice(sub * CA_TQ, (sub + 1) * CA_TQ)
            kw, vw, q, dob = k_ref[win, :], v_ref[win, :], q_ref[rows, :], do_ref[rows, :]
            head = [(lane // HEAD_DIM) == h for h in (0, 1)]
            zb = jnp.zeros_like(q)
            q2 = jnp.concatenate([jnp.where(hm, q, zb) for hm in head], axis=0) * jnp.asarray(scale, BF16)
            do2 = jnp.concatenate([jnp.where(hm, dob, zb) for hm in head], axis=0)
            p = _ca_probs(q2, kw, jnp.concatenate([bias_ref[h, :, :CA_WIN] for h in (0, 1)], axis=0), penalty)
            dp = _dot_nt(do2, vw)
            ds = p * (dp - jnp.sum(p * dp, axis=1, keepdims=True))
            dsb = ds.astype(BF16)
            dq = jnp.zeros((CA_TQ, PAIR), F32)
            for h in (0, 1):
                ds_h = ds[h * CA_TQ:(h + 1) * CA_TQ]
                dbias[h] = ds_h if dbias[h] is None else dbias[h] + ds_h
                dq = dq + _dot(dsb[h * CA_TQ:(h + 1) * CA_TQ], jnp.where(head[h], kw, jnp.zeros_like(kw)))
            dk_t = _dot_tn(dsb, q2)
            dv_t = _dot_tn(p.astype(BF16), do2)
            dq_ref[rows, :] = dq * scale
            dk_u = spread(dk_t, sub) if dk_u is None else dk_u + spread(dk_t, sub)
            dv_u = spread(dv_t, sub) if dv_u is None else dv_u + spread(dv_t, sub)
        union = pl.ds(pl.multiple_of(i * tq, CA_TQ), CA_WIN + (CA_SUB - 1) * CA_TQ)
        dk_acc[union, :] += dk_u
        dv_acc[union, :] += dv_u
        for h in (0, 1):
            dbias_ref[h, :, :CA_WIN] += dbias[h]

        @pl.when(i == nq - 1)
        def _():
            dkv_ref[0, :, :] = dk_acc[CA_PAD:, :]
            dkv_ref[1, :, :] = dv_acc[CA_PAD:, :]
            for h in (0, 1):
                dext_ref[pl.ds(h, 1), :] = jnp.sum(_skew(dbias_ref[h], -1), axis=0, keepdims=True)

    es = pl.BlockSpec((None, 2, CA_EXT), lambda p, i: (p, 0, 0))
    ts = pl.BlockSpec((tq, PAIR), lambda p, i: (i, p))
    return _side_call(
        "ca_bwd", body, (hp, nq),
        [pl.BlockSpec((None, tq, PAIR), lambda p, i: (0, i + npad, p)),
         pl.BlockSpec((None, tp, PAIR), lambda p, i: (1, 0, p)),
         pl.BlockSpec((None, tp, PAIR), lambda p, i: (2, 0, p)), es, ts],
        (ts, pl.BlockSpec((2, t, PAIR), lambda p, i: (0, 0, p)), es),
        (jax.ShapeDtypeStruct((t, d), F32), jax.ShapeDtypeStruct((2, t, d), F32), jax.ShapeDtypeStruct((hp, 2, CA_EXT), F32)),
        [pltpu.VMEM((2, CA_TQ, CA_EXT), F32), pltpu.VMEM((2, CA_TQ, CA_EXT), F32),
         pltpu.VMEM((tp, PAIR), F32), pltpu.VMEM((tp, PAIR), F32)],
        (qkvn, qkvn, qkvn, ext, do), side, ("parallel", "arbitrary"))


def _loss_head(y, target):
    t, d = y.shape
    tm = _tile(t, 512)

    def body(y_ref, t_ref, dy_ref, loss_ref):
        diff = y_ref[...] - t_ref[...]
        dy_ref[...] = diff * (1.0 / d)
        part = 0.5 * jnp.sum(jnp.mean(diff * diff, axis=-1, keepdims=True), axis=0, keepdims=True)

        @pl.when(pl.program_id(0) == 0)
        def _():
            loss_ref[...] = jnp.zeros_like(loss_ref)

        loss_ref[...] += jnp.broadcast_to(part, loss_ref.shape)

    row = pl.BlockSpec((tm, d), lambda m: (m, 0))
    return pl.pallas_call(
        body, name="loss_head", grid=(t // tm,), in_specs=[row, row],
        out_specs=(row, pl.BlockSpec((8, 128), lambda m: (0, 0))),
        out_shape=(jax.ShapeDtypeStruct((t, d), F32), jax.ShapeDtypeStruct((8, 128), F32)),
        compiler_params=_cp(("arbitrary",)),
    )(y, target)


def _adamw_math(w, g, m, v):
    m = ADAM_B1 * m + (1.0 - ADAM_B1) * g
    v = ADAM_B2 * v + (1.0 - ADAM_B2) * (g * g)
    m_hat = m / (1.0 - ADAM_B1 ** ADAM_STEP)
    v_hat = v / (1.0 - ADAM_B2 ** ADAM_STEP)
    delta = -ADAM_LR * (m_hat / (jnp.sqrt(v_hat) + ADAM_EPS) + ADAM_WD * w)
    return delta, m, v


def _adamw(name, w, g, m, v):
    shape = w.shape
    cols = shape[-1]
    rows = int(np.prod(shape[:-1]))
    tr = _tile(rows, 512)
    flat = [a.reshape(rows, cols) for a in (w, g, m, v)]

    def body(w_ref, g_ref, m_ref, v_ref, d_ref, nm_ref, nv_ref):
        d_ref[...], nm_ref[...], nv_ref[...] = _adamw_math(w_ref[...], g_ref[...], m_ref[...], v_ref[...])

    blk = pl.BlockSpec((tr, cols), lambda r: (r, 0))
    outs = pl.pallas_call(
        body, name=name, grid=(rows // tr,), in_specs=[blk] * 4, out_specs=(blk,) * 3,
        out_shape=(jax.ShapeDtypeStruct((rows, cols), F32),) * 3, compiler_params=_cp(("parallel",)),
    )(*flat)
    return [o.reshape(shape) for o in outs]


def _place():
    x, y, c = lax.axis_index("x"), lax.axis_index("y"), lax.axis_index("c")
    chips = [(1 - x, y), (x, 1 - y), (1 - x, 1 - y)]
    return x, y, c, chips


def _shard_slab(ref, kind, layer0, n_layers, shard, width):
    lay = pl.ds(layer0, n_layers)
    if kind == "cols":
        return ref.at[lay, :, pl.ds(shard * width, width)]
    return ref.at[lay, pl.ds(shard * width, width), :]


def _cast_into_place(w, kind, shard):
    n_layers, rows, cols = w.shape
    tr = _tile(rows, 512)
    nr = rows // tr

    def body(s_ref, w_ref, o_ref):
        o_ref[...] = w_ref[...].astype(BF16)

    if kind == "cols":
        full, out = (n_layers, rows, cols * N_CHIPS), pl.BlockSpec((None, tr, cols), lambda l, r, s_ref: (l, r, s_ref[0]))
    else:
        full, out = (n_layers, rows * N_CHIPS, cols), pl.BlockSpec((None, tr, cols), lambda l, r, s_ref: (l, s_ref[0] * nr + r, 0))
    grid_spec = pltpu.PrefetchScalarGridSpec(
        num_scalar_prefetch=1, grid=(n_layers, nr),
        in_specs=[pl.BlockSpec((None, tr, cols), lambda l, r, s_ref: (l, r, 0))], out_specs=out)
    return pl.pallas_call(body, name="cast_into_place", grid_spec=grid_spec, out_shape=jax.ShapeDtypeStruct(full, BF16),
                          compiler_params=_cp(("parallel", "parallel")))(shard, w)


def _gather_side(items, fulls, kinds, widths):
    n = len(items)

    def each(w, send, recv, owner_does, other_does=None):
        x, y, c, chips = _place()
        for e, (a, layer) in enumerate(items):
            def copy(k, shard, to, a=a, layer=layer):
                slab = _shard_slab(w[a], kinds[a], layer, 1, shard, widths[a])
                return pltpu.make_async_remote_copy(src_ref=slab, dst_ref=slab, send_sem=send.at[k], recv_sem=recv.at[k],
                                                    device_id=to, device_id_type=MESH)

            @pl.when(c == layer % 2)
            def _(e=e, copy=copy):
                for j, (cx, cy) in enumerate(chips):
                    owner_does(copy, j * n + e, 3 * n + j * n + e, x, y, c, cx, cy)

            if other_does is not None:
                @pl.when(c != layer % 2)
                def _(e=e, copy=copy):
                    for j, (cx, cy) in enumerate(chips):
                        other_does(copy, 3 * n + j * n + e, x, y, c, cx, cy)

    def start(_, w, send, recv):
        each(w, send, recv, lambda copy, k, kf, x, y, c, cx, cy: copy(k, 2 * x + y, (cx, cy, c)).start())

    def mid(_, w, send, recv):
        def forward(copy, k, kf, x, y, c, cx, cy):
            copy(k, 2 * cx + cy, (x, y, c)).wait_recv()
            copy(kf, 2 * cx + cy, (x, y, 1 - c)).start()
        each(w, send, recv, forward)

    def finish(_, w, send, recv):
        def sent(copy, k, kf, x, y, c, cx, cy):
            copy(k, 2 * x + y, (cx, cy, c)).wait_send()
            copy(kf, 2 * cx + cy, (x, y, 1 - c)).wait_send()
        each(w, send, recv, sent, lambda copy, kf, x, y, c, cx, cy: copy(kf, 2 * cx + cy, (x, y, c)).wait_recv())

    return _Side(fulls, [True] * len(fulls), [], 6 * n, 6 * n, start, mid, finish)


def _shard_piece(ref, kind, shard, width):
    return ref.at[:, shard * width:(shard + 1) * width] if kind == "cols" else ref.at[shard * width:(shard + 1) * width, :]


def _reduce_side(layer, grads, kinds, widths):
    na = len(grads)
    owner = layer % 2
    new_outs = []
    for g, k, w in zip(grads, kinds, widths):
        new_outs.append(jax.ShapeDtypeStruct((N_DEV,) + ((g.shape[0], w) if k == "cols" else (w, g.shape[1])), g.dtype))

    def sends(g, got, send, recv, act):
        x, y, c, _ = _place()
        me = 4 * x + 2 * y + c
        for b in range(N_CHIPS):
            bx, by = b >> 1, b & 1
            to_me = jnp.logical_and(jnp.logical_and(x == bx, y == by), c == owner)

            @pl.when(jnp.logical_not(to_me))
            def _(b=b, bx=bx, by=by):
                for a in range(na):
                    cp = pltpu.make_async_remote_copy(
                        src_ref=_shard_piece(g[a], kinds[a], b, widths[a]), dst_ref=got[a].at[me], send_sem=send.at[b * na + a],
                        recv_sem=recv.at[me * na + a], device_id=(bx, by, owner), device_id_type=MESH)
                    cp.start() if act == "start" else cp.wait_send()

    def start(g, got, send, recv):
        sends(g, got, send, recv, "start")

    def mid(g, got, send, recv):
        pass

    def finish(g, got, send, recv):
        sends(g, got, send, recv, "wait")
        x, y, c, _ = _place()
        me = 4 * x + 2 * y + c

        @pl.when(c == owner)
        def _():
            for s in range(N_DEV):
                @pl.when(me != s)
                def _(s=s):
                    for a in range(na):
                        pltpu.make_async_remote_copy(
                            src_ref=_shard_piece(g[a], kinds[a], 0, widths[a]), dst_ref=got[a].at[s], send_sem=send.at[0],
                            recv_sem=recv.at[s * na + a], device_id=(x, y, c), device_id_type=MESH).wait_recv()

    return _Side(grads, [False] * na, new_outs, N_CHIPS * na, N_DEV * na, start, mid, finish)


def _reduce_sum(layer, n_layers, g, got, kind, width, prev, flags):
    _, rows, cols = got.shape
    tr = _tile(rows, 256)
    nr = rows // tr

    def body(f_ref, g_ref, got_ref, *rest):
        o_ref = rest[-1]

        @pl.when(f_ref[2] == 1)
        def _():
            acc = None
            for s in range(N_DEV):
                term = jnp.where(f_ref[1] == s, g_ref[...], got_ref[s]).astype(F32)
                acc = term if acc is None else acc + term
            o_ref[...] = acc

    if kind == "cols":
        own = pl.BlockSpec((tr, width), lambda r, f_ref: (r, f_ref[0]))
    else:
        own = pl.BlockSpec((tr, cols), lambda r, f_ref: (f_ref[0] * nr + r, 0))
    grid_spec = pltpu.PrefetchScalarGridSpec(
        num_scalar_prefetch=1, grid=(nr,),
        in_specs=[own, pl.BlockSpec((N_DEV, tr, cols), lambda r, f_ref: (0, r, 0))] + ([] if prev is None else [ANY]),
        out_specs=pl.BlockSpec((None, tr, cols), lambda r, f_ref: (layer, r, 0)))
    return pl.pallas_call(
        body, name="reduce_sum", grid_spec=grid_spec, out_shape=jax.ShapeDtypeStruct((n_layers, rows, cols), F32),
        input_output_aliases={} if prev is None else {3: 0}, compiler_params=_cp(("arbitrary",)),
    )(flags, g, got, *([] if prev is None else [prev]))


def _pair_share(sums):
    n_layers = sums[0].shape[0]
    na = len(sums)

    def body(*refs):
        dst = refs[na:2 * na]
        send_sems, recv_sems = refs[2 * na:]
        x, y, c, _ = _place()

        def swap(l, a):
            return pltpu.make_async_remote_copy(
                src_ref=dst[a].at[l], dst_ref=dst[a].at[l], send_sem=send_sems.at[l * na + a],
                recv_sem=recv_sems.at[l * na + a], device_id=(x, y, 1 - c), device_id_type=MESH)

        for l in range(n_layers):
            @pl.when(c == l % 2)
            def _(l=l):
                for a in range(na):
                    swap(l, a).start()
        for l in range(n_layers):
            @pl.when(c == l % 2)
            def _(l=l):
                for a in range(na):
                    swap(l, a).wait_send()

            @pl.when(c != l % 2)
            def _(l=l):
                for a in range(na):
                    swap(l, a).wait_recv()

    return pl.pallas_call(
        body, name="pair_share", in_specs=[ANY] * na, out_specs=[ANY] * na,
        out_shape=[jax.ShapeDtypeStruct(s.shape, s.dtype) for s in sums], input_output_aliases={a: a for a in range(na)},
        scratch_shapes=[pltpu.SemaphoreType.DMA((n_layers * na,)), pltpu.SemaphoreType.DMA((n_layers * na,))],
    )(*sums)


def _small_step(g_part, w, m, v):
    r = g_part.shape[0]

    def body(g_ref, w_ref, m_ref, v_ref, go_ref, d_ref, nm_ref, nv_ref, all_ref, send_sems, recv_sems):
        x, y, c, _ = _place()
        me = 4 * x + 2 * y + c
        all_ref[me] = g_ref[...]
        cps = []
        for k in range(1, N_DEV):
            px, py, pc = (x + (k >> 2)) % 2, (y + ((k >> 1) & 1)) % 2, (c + (k & 1)) % 2
            cps.append(pltpu.make_async_remote_copy(src_ref=g_ref, dst_ref=all_ref.at[me], send_sem=send_sems.at[k - 1],
                                                    recv_sem=recv_sems.at[k - 1], device_id=(px, py, pc), device_id_type=MESH))
        for cp in cps:
            cp.start()
        for cp in cps:
            cp.wait()
        g = all_ref[0]
        for k in range(1, N_DEV):
            g = g + all_ref[k]
        go_ref[...] = g
        d_ref[...], nm_ref[...], nv_ref[...] = _adamw_math(w_ref[...], g, m_ref[...], v_ref[...])

    vm = pl.BlockSpec(memory_space=pltpu.VMEM)
    return pl.pallas_call(
        body, name="small_step", in_specs=[vm] * 4, out_specs=[vm] * 4, out_shape=[jax.ShapeDtypeStruct((r, 128), F32)] * 4,
        scratch_shapes=[pltpu.VMEM((N_DEV, r, 128), F32), pltpu.SemaphoreType.DMA((N_DEV - 1,)), pltpu.SemaphoreType.DMA((N_DEV - 1,))],
    )(g_part, w, m, v)


def _onehot_mm(name, a, onehot):
    def body(a_ref, oh_ref, o_ref):
        v = a_ref[...]
        oh = oh_ref[...]
        hi, lo = _split_bf16(v)
        lo2 = (v - hi.astype(F32) - lo.astype(F32)).astype(BF16)
        o_ref[...] = _dot(hi, oh) + _dot(lo, oh) + _dot(lo2, oh)

    return pl.pallas_call(body, name=name, out_shape=jax.ShapeDtypeStruct((a.shape[0], onehot.shape[1]), F32))(a, onehot)


def _pack_small(parts):
    flat = jnp.concatenate([p.reshape(-1) for p in parts])
    n = flat.shape[0]
    rows = -(-n // 128)
    rows = -(-rows // 8) * 8
    return jnp.pad(flat, (0, rows * 128 - n)).reshape(rows, 128)


def _unpack_small(packed, like):
    flat = packed.reshape(-1)
    out, off = [], 0
    for p in like:
        out.append(flat[off:off + p.size].reshape(p.shape))
        off += p.size
    return out


def kernel(x, mix_norm, w_qkv, w_o, q_norm, k_norm, rel_bias, ffn_norm, w_up, w_down, loss_target, m_mix_norm, m_w_qkv, m_w_o, m_q_norm, m_k_norm, m_rel_bias, m_ffn_norm, m_w_up, m_w_down, v_mix_norm, v_w_qkv, v_w_o, v_q_norm, v_k_norm, v_rel_bias, v_ffn_norm, v_w_up, v_w_down):
    n_layers, d = mix_norm.shape
    t = x.shape[1]
    ff = w_down.shape[1] * N_CHIPS
    heads = d // HEAD_DIM
    cx, cy, cc = lax.axis_index("x"), lax.axis_index("y"), lax.axis_index("c")
    shard = (2 * cx + cy).astype(jnp.int32).reshape(1)
    core = cc.astype(jnp.int32).reshape(1)

    big = [w_qkv, w_o, w_up, w_down]
    kinds = ["cols", "rows", "cols", "rows"]
    widths = [w_qkv.shape[2], w_o.shape[1], w_up.shape[2], w_down.shape[1]]
    fulls = _blocking("gather_first", _gather_side([(0, 0)], [_cast_into_place(w, k, shard) for w, k in zip(big, kinds)],
                                                   kinds, widths))

    rel_pad = -(-N_REL // PAIR) * PAIR
    ext_hot = _ca_ext_index()[:, None] == np.arange(rel_pad)[None, :]
    fold_hot, spread_hot = jnp.asarray(ext_hot, BF16), jnp.asarray(ext_hot.T, BF16)
    rel_tab = jnp.pad(rel_bias, ((0, 0), (0, 0), (0, rel_pad - N_REL)))
    exts = [_onehot_mm("relbias_spread", rel_tab[i], spread_hot).reshape(heads // 2, 2, CA_EXT) for i in range(n_layers // 2)]

    xs, h1s, qkvs, os_, o32s, x2s, h2s, ss, us, q32s = [], [], [], [], [], [], [], [], [], []
    xc = x[0]
    h = _rms_first(xc, mix_norm[0:1])
    for layer in range(n_layers):
        xs.append(xc)
        h1s.append(h)
        side = _gather_side([(1, layer), (2, layer), (3, layer)] + ([(0, layer + 1)] if layer + 1 < n_layers else []),
                            fulls, kinds, widths)
        if layer % 2 == 0:
            qkv = _qkv_proj(h, fulls[0], layer, BF16)
            (o, o32), gathered = _sb_fwd(qkv, side)
            q32s.append(None)
        else:
            idx = layer // 2
            q32 = _qkv_proj(h, fulls[0], layer, F32)
            gq = jnp.tile(q_norm[idx], 2).reshape(1, PAIR)
            gk = jnp.tile(k_norm[idx], 2).reshape(1, PAIR)
            qkv = _ca_prep(q32, gq, gk)
            (o,), gathered = _ca_fwd(qkv, exts[idx], side)
            o32 = None
            q32s.append(q32)
        fulls = gathered
        wq, wo, wu, wd = fulls
        qkvs.append(qkv)
        os_.append(o)
        o32s.append(o32)
        x2, h2 = _out_proj("attn_out", o, wo, layer, xc, ffn_norm[layer:layer + 1])
        s, u = _up_proj(h2, wu, layer)
        nxt = mix_norm[layer + 1:layer + 2] if layer + 1 < n_layers else mix_norm[0:1]
        xc, h = _out_proj("mlp_out", u, wd, layer, x2, nxt)
        x2s.append(x2)
        h2s.append(h2)
        ss.append(s)
        us.append(u)

    dx, loss_part = _loss_head(xc, loss_target[0])
    loss = lax.psum(loss_part[0, 0], ("x", "y", "c"))

    d_mix, d_ffn = [None] * n_layers, [None] * n_layers
    d_qn, d_kn, d_rb = [], [], []
    device = (4 * cx + 2 * cy + cc).astype(jnp.int32).reshape(1)
    sums = [None] * len(big)

    def reduce_sums(of_layer, grads, got):
        flags = jnp.concatenate([shard, device, (core == of_layer % 2).astype(jnp.int32)])
        return [_reduce_sum(of_layer, n_layers, g, r_, k, w_, s_, flags)
                for g, r_, k, w_, s_ in zip(grads, got, kinds, widths, sums)]

    for layer in reversed(range(n_layers)):
        du = _down_bwd(dx, wd, layer, ss[layer])
        g_down = _wgrad("wgrad_down", us[layer], _mat_spec, dx, _mat_spec_b, t, ff, d)
        g_up = _wgrad("wgrad_up", h2s[layer], _mat_spec, du, _mat_spec_b, t, d, ff)
        dx2, d_ffn[layer] = _norm_bwd_proj(
            "up_bwd", du, _rows2(ff), [(None, slice(None), slice(None))], wu, layer, x2s[layer], ffn_norm[layer:layer + 1], dx)
        do = _plain_nt("attn_out_bwd", dx2, wo, layer, BF16)
        g_o = _wgrad("wgrad_o", os_[layer], _mat_spec, dx2, _mat_spec_b, t, d, d)
        side = _reduce_side(layer, [g_o, g_up, g_down], kinds[1:], widths[1:])
        if layer % 2 == 0:
            (dqkv,), got = _sb_bwd(qkvs[layer], o32s[layer], do, side)
        else:
            idx = layer // 2
            gq = jnp.tile(q_norm[idx], 2).reshape(1, PAIR)
            gk = jnp.tile(k_norm[idx], 2).reshape(1, PAIR)
            (dqn, dkv, dext), got = _ca_bwd(qkvs[layer], exts[idx], do, side)
            dqkv, dgq, dgk = _ca_unprep(dqn, dkv, q32s[layer], gq, gk)
            d_qn.insert(0, dgq[0, :HEAD_DIM])
            d_kn.insert(0, dgk[0, :HEAD_DIM])
            d_rb.insert(0, _onehot_mm("relbias_fold", dext.reshape(heads, CA_EXT), fold_hot)[:, :N_REL])
        g_qkv = _wgrad("wgrad_qkv", h1s[layer], _mat_spec, dqkv, _plane_spec_b(d), t, d, 3 * d, tno=_tile(d, 1024))
        (dx, d_mix[layer]), got_qkv = _norm_bwd_proj(
            "qkv_bwd", dqkv, lambda tm: pl.BlockSpec((3, tm, d), lambda m: (0, m, 0)),
            [(p, slice(None), slice(p * d, (p + 1) * d)) for p in range(3)], wq, layer, xs[layer], mix_norm[layer:layer + 1], dx2,
            side=_reduce_side(layer, [g_qkv], kinds[:1], widths[:1]))
        sums = reduce_sums(layer, [g_qkv, g_o, g_up, g_down], got_qkv + got)
    grad_x = dx.reshape(x.shape)
    g_big = _pair_share(sums)

    upd = [_adamw("adamw", w_, g_, m_, v_) for w_, g_, m_, v_ in
           zip(big, g_big, [m_w_qkv, m_w_o, m_w_up, m_w_down], [v_w_qkv, v_w_o, v_w_up, v_w_down])]

    small_w = [mix_norm, q_norm, k_norm, rel_bias, ffn_norm]
    small_g = [jnp.concatenate(d_mix, 0), jnp.stack(d_qn), jnp.stack(d_kn), jnp.stack(d_rb), jnp.concatenate(d_ffn, 0)]
    packed = _small_step(_pack_small(small_g), _pack_small(small_w),
                         _pack_small([m_mix_norm, m_q_norm, m_k_norm, m_rel_bias, m_ffn_norm]),
                         _pack_small([v_mix_norm, v_q_norm, v_k_norm, v_rel_bias, v_ffn_norm]))
    sg, sd, sm, sv = [_unpack_small(p, small_w) for p in packed]

    def order(small, bigs):
        return [small[0], bigs[0], bigs[1], small[1], small[2], small[3], small[4], bigs[2], bigs[3]]

    return (loss, grad_x, *order(sg, g_big), *order(sd, [u_[0] for u_ in upd]),
            *order(sm, [u_[1] for u_ in upd]), *order(sv, [u_[2] for u_ in upd]))
```

```python
import functools

import numpy as np
import jax
import jax.numpy as jnp
from jax import lax
from jax.experimental import pallas as pl
from jax.experimental.pallas import tpu as pltpu

F32, BF16 = jnp.float32, jnp.bfloat16
MESH = pl.DeviceIdType.MESH

HEAD_DIM = 64
PAIR = 128
CHUNK = 64
LEFT_CHUNKS = 8
MAX_REL = 256
N_REL = 2 * MAX_REL + 1
CA_TQ = 128
CA_SUB = 4
CA_PAD = LEFT_CHUNKS * CHUNK
assert CA_PAD % (CA_SUB * CA_TQ) == 0
CA_WIN = CA_PAD + CA_TQ
CA_EXT = CA_WIN + CA_TQ
SB_T = 128
SB_WIN = 3
SB_SUB = 8
RMS_EPS = 1e-6
SB_EXIT = -104.0
NEG_BIG = -1e30
ADAM_LR, ADAM_B1, ADAM_B2, ADAM_EPS, ADAM_WD, ADAM_STEP = 0.001, 0.9, 0.999, 1e-08, 0.01, 10
VMEM_LIMIT = 56 << 20
N_CHIPS = 4
N_DEV = 8
ANY = pl.BlockSpec(memory_space=pl.ANY)


def _cp(sem=None, **kw):
    return pltpu.CompilerParams(dimension_semantics=sem, vmem_limit_bytes=VMEM_LIMIT, **kw)


def _tile(n, want):
    t = min(n, want)
    assert n % t == 0, (n, t)
    return t


class _Side:
    def __init__(self, ins, aliased, new_outs, n_send, n_recv, start, mid, finish):
        self.ins, self.aliased, self.new_outs = list(ins), list(aliased), list(new_outs)
        self.n_send, self.n_recv, self.start, self.mid, self.finish = n_send, n_recv, start, mid, finish

    def out_shapes(self):
        return [jax.ShapeDtypeStruct(a.shape, a.dtype) for a, al in zip(self.ins, self.aliased) if al] + self.new_outs


def _side_call(name, body, grid, in_specs, out_specs, out_shape, scratch_shapes, args, side, sem):
    out_shape, out_specs = list(out_shape), list(out_specs)
    if side is None:
        outs = pl.pallas_call(body, name=name, grid=grid, in_specs=in_specs, out_specs=out_specs, out_shape=out_shape,
                              scratch_shapes=scratch_shapes, compiler_params=_cp(sem))(*args)
        return list(outs), []
    n_in, n_out, n_sc, n_sin = len(in_specs), len(out_shape), len(scratch_shapes), len(side.ins)
    s_outs = side.out_shapes()

    def wrapped(*refs):
        ins, s_in = refs[:n_in], refs[n_in:n_in + n_sin]
        outs = refs[n_in + n_sin:n_in + n_sin + n_out]
        s_out = refs[n_in + n_sin + n_out:n_in + n_sin + n_out + len(s_outs)]
        scr = refs[n_in + n_sin + n_out + len(s_outs):]
        send, recv = scr[n_sc], scr[n_sc + 1]
        ids = [pl.program_id(k) for k in range(len(grid))]

        def at(first_axis, rest):
            ok = ids[0] == first_axis
            for k in range(1, len(grid)):
                ok = jnp.logical_and(ok, ids[k] == (grid[k] - 1 if rest == "last" else 0))
            return ok

        @pl.when(at(0, "first"))
        def _():
            side.start(s_in, s_out, send, recv)

        body(*ins, *outs, *scr[:n_sc])

        @pl.when(at(grid[0] // 2, "first"))
        def _():
            side.mid(s_in, s_out, send, recv)

        @pl.when(at(grid[0] - 1, "last"))
        def _():
            side.finish(s_in, s_out, send, recv)

    aliases, pos = {}, n_out
    for k, al in enumerate(side.aliased):
        if al:
            aliases[n_in + k] = pos
            pos += 1
    outs = pl.pallas_call(
        wrapped, name=name, grid=grid, in_specs=list(in_specs) + [ANY] * n_sin, out_specs=out_specs + [ANY] * len(s_outs),
        out_shape=out_shape + s_outs, input_output_aliases=aliases,
        scratch_shapes=list(scratch_shapes) + [pltpu.SemaphoreType.DMA((side.n_send,)), pltpu.SemaphoreType.DMA((side.n_recv,))],
        compiler_params=_cp(("arbitrary",) * len(grid)),
    )(*args, *side.ins)
    return list(outs[:n_out]), list(outs[n_out:])


def _blocking(name, side):
    s_outs = side.out_shapes()
    n_sin = len(side.ins)

    def body(*refs):
        s_in, s_out = refs[:n_sin], refs[n_sin:n_sin + len(s_outs)]
        send, recv = refs[n_sin + len(s_outs):]
        side.start(s_in, s_out, send, recv)
        side.mid(s_in, s_out, send, recv)
        side.finish(s_in, s_out, send, recv)

    aliases, pos = {}, 0
    for k, al in enumerate(side.aliased):
        if al:
            aliases[k] = pos
            pos += 1
    return list(pl.pallas_call(
        body, name=name, in_specs=[ANY] * n_sin, out_specs=[ANY] * len(s_outs), out_shape=s_outs, input_output_aliases=aliases,
        scratch_shapes=[pltpu.SemaphoreType.DMA((side.n_send,)), pltpu.SemaphoreType.DMA((side.n_recv,))],
    )(*side.ins))


def _split_bf16(v):
    hi = v.astype(BF16)
    lo = (v - hi.astype(F32)).astype(BF16)
    return hi, lo


def _dot(a, b):
    return jnp.dot(a, b, preferred_element_type=F32)


def _dot_nt(a, b):
    return lax.dot_general(a, b, (((1,), (1,)), ((), ())), preferred_element_type=F32)


def _dot_tn(a, b):
    return lax.dot_general(a, b, (((0,), (0,)), ((), ())), preferred_element_type=F32)


TM = 512
TN = 1024


def _row_matmul(name, a, a_spec, w, layer, nt, chunks, extra, extra_specs, out_shape, out_specs, epilogue, t,
                sem=("parallel",), side=None):
    tm = _tile(t, TM)
    n_extra = len(extra)

    def body(a_ref, w_ref, *rest):
        ins, outs = rest[:n_extra], rest[n_extra:]
        for n, terms in enumerate(chunks):
            acc = None
            for plane, rows, cols in terms:
                av = (a_ref[...] if plane is None else a_ref[plane]).astype(BF16)
                part = _dot_nt(av, w_ref[rows, cols]) if nt else _dot(av, w_ref[rows, cols])
                acc = part if acc is None else acc + part
            epilogue(n, acc, ins, outs)

    w_spec = pl.BlockSpec((None,) + w.shape[1:], lambda m: (layer, 0, 0))
    outs, side_outs = _side_call(name, body, (t // tm,), [a_spec(tm), w_spec, *extra_specs], out_specs, out_shape, [],
                                 (a, w, *extra), side, sem)
    return outs if side is None else (outs, side_outs)


def _rows2(width):
    return lambda tm: pl.BlockSpec((tm, width), lambda m: (m, 0))


def _col_chunks(n, width=None):
    tn = _tile(n, TN)
    return [[(None, slice(None), slice(c * tn, (c + 1) * tn))] for c in range(n // tn)], tn


def _rms_rows(x, gain):
    r = lax.rsqrt(jnp.mean(x * x, axis=-1, keepdims=True) + RMS_EPS)
    return x * r * gain


def _rms_bwd_rows(dh, x, gain, dres):
    r = lax.rsqrt(jnp.mean(x * x, axis=-1, keepdims=True) + RMS_EPS)
    xhat = x * r
    dxn = dh * gain
    dx = r * (dxn - xhat * jnp.mean(dxn * xhat, axis=-1, keepdims=True))
    return dx + dres, jnp.sum(dh * xhat, axis=0, keepdims=True)


def _rms_first(x, gain):
    t, d = x.shape
    tm = _tile(t, 512)

    def body(x_ref, g_ref, h_ref):
        h_ref[...] = _rms_rows(x_ref[...], g_ref[...]).astype(BF16)

    return pl.pallas_call(
        body, name="rms_first", grid=(t // tm,),
        in_specs=[pl.BlockSpec((tm, d), lambda m: (m, 0)), pl.BlockSpec((1, d), lambda m: (0, 0))],
        out_specs=pl.BlockSpec((tm, d), lambda m: (m, 0)), out_shape=jax.ShapeDtypeStruct((t, d), BF16),
        compiler_params=_cp(("parallel",)),
    )(x, gain)


def _qkv_proj(h, w, layer, out_dtype):
    t, d = h.shape
    chunks = [[(None, slice(None), slice(p * d, (p + 1) * d))] for p in range(3)]

    def epi(n, acc, ins, outs):
        outs[0][n] = acc.astype(out_dtype)

    return _row_matmul(
        "qkv_proj", h, _rows2(d), w, layer, False, chunks, (), (), (jax.ShapeDtypeStruct((3, t, d), out_dtype),),
        (pl.BlockSpec((3, _tile(t, TM), d), lambda m: (0, m, 0)),), epi, t)[0]


def _out_proj(name, a, w, layer, res, gain):
    t, kk = a.shape
    d = res.shape[1]

    def epi(n, acc, ins, outs):
        xn = ins[0][...] + acc
        outs[0][...] = xn
        outs[1][...] = _rms_rows(xn, ins[1][...]).astype(BF16)

    row = _rows2(d)(_tile(t, TM))
    return _row_matmul(
        name, a, _rows2(kk), w, layer, False, [[(None, slice(None), slice(None))]], (res, gain),
        (row, pl.BlockSpec((1, d), lambda m: (0, 0))),
        (jax.ShapeDtypeStruct((t, d), F32), jax.ShapeDtypeStruct((t, d), BF16)), (row, row), epi, t)


def _up_proj(h, w, layer):
    t, d = h.shape
    ff = w.shape[2]
    chunks, tn = _col_chunks(ff)

    def epi(n, acc, ins, outs):
        s = jnp.maximum(acc, 0.0)
        outs[0][:, n * tn:(n + 1) * tn] = s.astype(BF16)
        outs[1][:, n * tn:(n + 1) * tn] = (s * s).astype(BF16)

    o = _rows2(ff)(_tile(t, TM))
    return _row_matmul("up_proj", h, _rows2(d), w, layer, False, chunks, (), (), (jax.ShapeDtypeStruct((t, ff), BF16),) * 2, (o, o), epi, t)


def _down_bwd(dx, w, layer, s):
    t, d = dx.shape
    ff = w.shape[1]
    tn = _tile(ff, TN)
    chunks = [[(None, slice(c * tn, (c + 1) * tn), slice(None))] for c in range(ff // tn)]

    def epi(n, acc, ins, outs):
        cols = slice(n * tn, (n + 1) * tn)
        outs[0][:, cols] = (acc * (2.0 * ins[0][:, cols].astype(F32))).astype(BF16)

    o = _rows2(ff)(_tile(t, TM))
    return _row_matmul("down_bwd", dx, _rows2(d), w, layer, True, chunks, (s,), (o,), (jax.ShapeDtypeStruct((t, ff), BF16),), (o,), epi, t)[0]


def _norm_bwd_proj(name, a, a_spec, terms, w, layer, x, gain, dres, side=None):
    t, d = x.shape

    def epi(n, acc, ins, outs):
        dx, dg = _rms_bwd_rows(acc, ins[0][...], ins[1][...], ins[2][...])
        outs[0][...] = dx

        @pl.when(pl.program_id(0) == 0)
        def _():
            outs[1][...] = dg

        @pl.when(pl.program_id(0) > 0)
        def _():
            outs[1][...] += dg

    row = _rows2(d)(_tile(t, TM))
    vec = pl.BlockSpec((1, d), lambda m: (0, 0))
    return _row_matmul(
        name, a, a_spec, w, layer, True, [terms], (x, gain, dres), (row, vec, row),
        (jax.ShapeDtypeStruct((t, d), F32), jax.ShapeDtypeStruct((1, d), F32)), (row, vec), epi, t, sem=("arbitrary",), side=side)


def _plain_nt(name, a, w, layer, out_dtype):
    t, n = a.shape
    m_out = w.shape[1]

    def epi(n_, acc, ins, outs):
        outs[0][...] = acc.astype(out_dtype)

    return _row_matmul(
        name, a, _rows2(n), w, layer, True, [[(None, slice(None), slice(None))]], (), (),
        (jax.ShapeDtypeStruct((t, m_out), out_dtype),), (_rows2(m_out)(_tile(t, TM)),), epi, t)[0]


def _wgrad(name, a, a_spec_fn, b, b_spec_fn, t, mo, no, tno=1024):
    tmo, tno, tk = _tile(mo, 1024), _tile(no, tno), _tile(t, 2048)
    nk = t // tk

    def body(a_ref, b_ref, o_ref, acc_ref):
        part = _dot_tn(a_ref[...].astype(BF16), b_ref[...].astype(BF16))
        k = pl.program_id(2)

        @pl.when(k == 0)
        def _():
            acc_ref[...] = part

        @pl.when(k > 0)
        def _():
            acc_ref[...] += part

        @pl.when(k == nk - 1)
        def _():
            o_ref[...] = acc_ref[...].astype(BF16)

    return pl.pallas_call(
        body, name=name, grid=(mo // tmo, no // tno, nk), in_specs=[a_spec_fn(tk, tmo), b_spec_fn(tk, tno)],
        out_specs=pl.BlockSpec((tmo, tno), lambda m, n, k: (m, n)), out_shape=jax.ShapeDtypeStruct((mo, no), BF16),
        scratch_shapes=[pltpu.VMEM((tmo, tno), F32)], compiler_params=_cp(("parallel", "parallel", "arbitrary")),
    )(a, b)


def _mat_spec(tk, tw):
    return pl.BlockSpec((tk, tw), lambda m, n, k: (k, m))


def _mat_spec_b(tk, tw):
    return pl.BlockSpec((tk, tw), lambda m, n, k: (k, n))


def _plane_spec_b(d):
    def fn(tk, tw):
        npp = d // tw
        return pl.BlockSpec((None, tk, tw), lambda m, n, k: (n // npp, k, n % npp))
    return fn


def _sb_masks(t):
    row = lax.broadcasted_iota(jnp.int32, (t, t), 0)
    col = lax.broadcasted_iota(jnp.int32, (t, t), 1)
    lane = lax.broadcasted_iota(jnp.int32, (1, PAIR), 1)
    return row, col, [(lane // HEAD_DIM) == h for h in (0, 1)]


def _suffix_sums(v, tri, tail, tt):
    hi, lo = _split_bf16(v)
    parts = []
    for b in reversed(range(v.shape[1] // tt)):
        cols = slice(b * tt, (b + 1) * tt)
        parts.insert(0, _dot(hi[:, cols], tri) + _dot(lo[:, cols], tri) + tail)
        tail = tail + jnp.sum(v[:, cols], axis=1, keepdims=True)
    return (parts[0] if len(parts) == 1 else jnp.concatenate(parts, axis=1)), tail


def _sb_tile(qh, kb, valid, after, carry, tt):
    z = _dot_nt(qh, kb)
    lb = -(jnp.maximum(z, 0.0) + jnp.log(1.0 + jnp.exp(-jnp.abs(z))))
    if valid is not None:
        lb = jnp.where(valid, lb, 0.0)
    between, carry = _suffix_sums(lb, after, carry, tt)
    a = jnp.exp(z + lb + between)
    if valid is not None:
        a = jnp.where(valid, a, 0.0)
    return lb, a, carry


def _sb_window_valid(i, first, tt):
    t_pos = i * tt + lax.broadcasted_iota(jnp.int32, (tt, SB_WIN * tt), 0)
    s_pos = first * tt + lax.broadcasted_iota(jnp.int32, (tt, SB_WIN * tt), 1)
    return s_pos < t_pos


def _sb_walk_on(st):
    return jnp.logical_and(st[0] >= 0, jnp.max(st[1]) > SB_EXIT)


def _sb_fwd(qkv, side=None):
    _, t, d = qkv.shape
    hp, tt = d // PAIR, _tile(t, SB_T)

    def body(q_ref, k_ref, v_ref, o_ref, o32_ref):
        row, col, head = _sb_masks(tt)
        after = (row > col).astype(BF16)

        def tile(kb, vb, valid, carry, acc, q2):
            _, a, carry = _sb_tile(q2, kb, valid, after, carry, tt)
            ab = a.astype(BF16)
            for h in (0, 1):
                acc = acc + _dot(ab[h * tt:(h + 1) * tt], jnp.where(head[h], vb, jnp.zeros_like(vb)))
            return carry, acc

        states = []
        for sub in range(SB_SUB):
            i = pl.program_id(1) * SB_SUB + sub
            q = q_ref[sub * tt:(sub + 1) * tt, :]
            q2 = jnp.concatenate([jnp.where(head[h], q, jnp.zeros_like(q)) for h in (0, 1)], axis=0) * jnp.asarray(HEAD_DIM ** -0.5, BF16)
            first = jnp.maximum(i - (SB_WIN - 1), 0)
            rows_w = pl.ds(pl.multiple_of(first * tt, tt), SB_WIN * tt)
            valid = _sb_window_valid(i, first, tt)
            carry, acc = tile(k_ref[rows_w, :], v_ref[rows_w, :], jnp.concatenate([valid, valid], axis=0),
                              jnp.zeros((2 * tt, 1), F32), jnp.zeros((tt, PAIR), F32), q2)
            states.append((first, carry, acc, q2))

        for sub, (first, carry, acc, q2) in enumerate(states):
            def step(st, q2=q2):
                rows = pl.ds(pl.multiple_of(st[0] * tt, tt), tt)
                carry, acc = tile(k_ref[rows, :], v_ref[rows, :], None, st[1], st[2], q2)
                return st[0] - 1, carry, acc

            o = lax.while_loop(_sb_walk_on, step, (first - 1, carry, acc))[2]
            o_ref[sub * tt:(sub + 1) * tt, :] = o.astype(BF16)
            o32_ref[sub * tt:(sub + 1) * tt, :] = o

    tq = SB_SUB * tt
    qs = pl.BlockSpec((None, tq, PAIR), lambda p, i: (0, i, p))
    ks = pl.BlockSpec((None, t, PAIR), lambda p, i: (1, 0, p))
    vs = pl.BlockSpec((None, t, PAIR), lambda p, i: (2, 0, p))
    os_ = pl.BlockSpec((tq, PAIR), lambda p, i: (i, p))
    return _side_call(
        "sb_fwd", body, (hp, t // tq), [qs, ks, vs], (os_, os_),
        (jax.ShapeDtypeStruct((t, d), BF16), jax.ShapeDtypeStruct((t, d), F32)), [], (qkv, qkv, qkv), side,
        ("parallel", "arbitrary"))


def _sb_bwd(qkv, o32, do, side=None):
    _, t, d = qkv.shape
    hp, tt = d // PAIR, _tile(t, SB_T)
    nq = t // (SB_SUB * tt)
    scale = HEAD_DIM ** -0.5

    def body(q_ref, k_ref, v_ref, o32_ref, do_ref, dqkv_ref, dk_acc, dv_acc):
        i = pl.program_id(1)
        row, col, head = _sb_masks(tt)
        after = (row > col).astype(BF16)
        from_s = (row >= col).astype(BF16)

        @pl.when(i == 0)
        def _():
            dk_acc[...] = jnp.zeros_like(dk_acc)
            dv_acc[...] = jnp.zeros_like(dv_acc)

        def tile(rows, valid, carry, seen, dq, q2, do2, tot):
            kb, vb = k_ref[rows, :], v_ref[rows, :]
            lb, a, carry = _sb_tile(q2, kb, valid, after, carry, tt)
            ab = a.astype(BF16)
            g = ab.astype(F32) * _dot_nt(do2, vb)
            g_from, seen = _suffix_sums(g, from_s, seen, tt)
            e = jnp.exp(lb)
            dz = g * e - (1.0 - e) * (tot - g_from)
            if valid is not None:
                dz = jnp.where(valid, dz, 0.0)
            dz = dz.astype(BF16)
            for h in (0, 1):
                dq = dq + _dot(dz[h * tt:(h + 1) * tt], jnp.where(head[h], kb, jnp.zeros_like(kb)))
            dk_acc[rows, :] += _dot_tn(dz, q2)
            dv_acc[rows, :] += _dot_tn(ab, do2)
            return carry, seen, dq

        states = []
        for sub in range(SB_SUB):
            ii = i * SB_SUB + sub
            q = q_ref[sub * tt:(sub + 1) * tt, :]
            dob = do_ref[sub * tt:(sub + 1) * tt, :]
            prod = dob.astype(F32) * o32_ref[sub * tt:(sub + 1) * tt, :]
            zb = jnp.zeros_like(q)
            q2 = jnp.concatenate([jnp.where(head[h], q, zb) for h in (0, 1)], axis=0) * jnp.asarray(scale, BF16)
            do2 = jnp.concatenate([jnp.where(head[h], dob, zb) for h in (0, 1)], axis=0)
            tot = jnp.concatenate([jnp.sum(jnp.where(head[h], prod, 0.0), axis=1, keepdims=True) for h in (0, 1)], axis=0)
            first = jnp.maximum(ii - (SB_WIN - 1), 0)
            valid = _sb_window_valid(ii, first, tt)
            zc = jnp.zeros((2 * tt, 1), F32)
            carry, seen, dq = tile(pl.ds(pl.multiple_of(first * tt, tt), SB_WIN * tt), jnp.concatenate([valid, valid], axis=0),
                                   zc, zc, jnp.zeros((tt, PAIR), F32), q2, do2, tot)
            states.append((first, carry, seen, dq, q2, do2, tot))

        for sub, (first, carry, seen, dq, q2, do2, tot) in enumerate(states):
            def step(st, q2=q2, do2=do2, tot=tot):
                carry, seen, dq = tile(pl.ds(pl.multiple_of(st[0] * tt, tt), tt), None, st[1], st[2], st[3], q2, do2, tot)
                return st[0] - 1, carry, seen, dq

            dq = lax.while_loop(_sb_walk_on, step, (first - 1, carry, seen, dq))[3]
            dqkv_ref[0, pl.ds(pl.multiple_of((i * SB_SUB + sub) * tt, tt), tt), :] = (dq * scale).astype(BF16)

        @pl.when(i == nq - 1)
        def _():
            dqkv_ref[1, :, :] = dk_acc[...].astype(BF16)
            dqkv_ref[2, :, :] = dv_acc[...].astype(BF16)

    qs = pl.BlockSpec((None, SB_SUB * tt, PAIR), lambda p, i: (0, i, p))
    ks = pl.BlockSpec((None, t, PAIR), lambda p, i: (1, 0, p))
    vs = pl.BlockSpec((None, t, PAIR), lambda p, i: (2, 0, p))
    ts = pl.BlockSpec((SB_SUB * tt, PAIR), lambda p, i: (i, p))
    return _side_call(
        "sb_bwd", body, (hp, nq), [qs, ks, vs, ts, ts], [pl.BlockSpec((3, t, PAIR), lambda p, i: (0, 0, p))],
        [jax.ShapeDtypeStruct((3, t, d), BF16)], [pltpu.VMEM((t, PAIR), F32), pltpu.VMEM((t, PAIR), F32)],
        (qkv, qkv, qkv, o32, do), side, ("parallel", "arbitrary"))


def _pair_sum_matrix():
    r = lax.broadcasted_iota(jnp.int32, (PAIR, PAIR), 0) // HEAD_DIM
    c = lax.broadcasted_iota(jnp.int32, (PAIR, PAIR), 1) // HEAD_DIM
    return (r == c).astype(BF16)


def _head_mean(v, ones):
    hi, lo = _split_bf16(v)
    return (_dot(hi, ones) + _dot(lo, ones)) * (1.0 / HEAD_DIM)


def _ca_prep(qkv32, gq, gk):
    _, t, d = qkv32.shape
    tr = _tile(t, CA_PAD)
    assert CA_PAD % tr == 0
    npad = CA_PAD // tr

    def body(x_ref, gq_ref, gk_ref, o_ref):
        p, r = pl.program_id(0), pl.program_id(1)

        @pl.when(r < npad)
        def _():
            o_ref[...] = jnp.zeros_like(o_ref)

        @pl.when(jnp.logical_and(r >= npad, p == 2))
        def _():
            o_ref[...] = x_ref[...].astype(BF16)

        @pl.when(jnp.logical_and(r >= npad, p < 2))
        def _():
            ones = _pair_sum_matrix()
            g = jnp.where(p == 0, gq_ref[...], gk_ref[...])
            for c in range(d // PAIR):
                x = x_ref[:, c * PAIR:(c + 1) * PAIR]
                rs = lax.rsqrt(_head_mean(x * x, ones) + RMS_EPS)
                o_ref[:, c * PAIR:(c + 1) * PAIR] = (x * rs * g).astype(BF16)

    vec = pl.BlockSpec((1, PAIR), lambda p, r: (0, 0))
    return pl.pallas_call(
        body, name="ca_prep", grid=(3, npad + t // tr),
        in_specs=[pl.BlockSpec((None, tr, d), lambda p, r: (p, jnp.maximum(r - npad, 0), 0)), vec, vec],
        out_specs=pl.BlockSpec((None, tr, d), lambda p, r: (p, r, 0)),
        out_shape=jax.ShapeDtypeStruct((3, CA_PAD + t, d), BF16), compiler_params=_cp(("parallel", "parallel")),
    )(qkv32, gq, gk)


def _ca_unprep(dq, dkv, qkv32, gq, gk):
    _, t, d = qkv32.shape
    tr = _tile(t, 512)
    nr = t // tr

    def body(dq_ref, dkv_ref, x_ref, gq_ref, gk_ref, o_ref, dgq_ref, dgk_ref):
        p, r = pl.program_id(0), pl.program_id(1)

        @pl.when(jnp.logical_and(p == 0, r == 0))
        def _():
            dgq_ref[...] = jnp.zeros_like(dgq_ref)
            dgk_ref[...] = jnp.zeros_like(dgk_ref)

        @pl.when(p == 2)
        def _():
            o_ref[...] = dkv_ref[...].astype(BF16)

        @pl.when(p < 2)
        def _():
            ones = _pair_sum_matrix()
            g = jnp.where(p == 0, gq_ref[...], gk_ref[...])
            dg = jnp.zeros((1, PAIR), F32)
            for c in range(d // PAIR):
                cols = slice(c * PAIR, (c + 1) * PAIR)
                x = x_ref[:, cols]
                dy = jnp.where(p == 0, dq_ref[:, cols], dkv_ref[:, cols])
                rs = lax.rsqrt(_head_mean(x * x, ones) + RMS_EPS)
                xhat = x * rs
                dxn = dy * g
                o_ref[:, cols] = (rs * (dxn - xhat * _head_mean(dxn * xhat, ones))).astype(BF16)
                dg = dg + jnp.sum(dy * xhat, axis=0, keepdims=True)
            dg = dg + pltpu.roll(dg, HEAD_DIM, 1)

            @pl.when(p == 0)
            def _():
                dgq_ref[...] += dg

            @pl.when(p == 1)
            def _():
                dgk_ref[...] += dg

    vec = pl.BlockSpec((1, PAIR), lambda p, r: (0, 0))
    return pl.pallas_call(
        body, name="ca_unprep", grid=(3, nr),
        in_specs=[pl.BlockSpec((tr, d), lambda p, r: (r, 0)),
                  pl.BlockSpec((None, tr, d), lambda p, r: (jnp.maximum(p - 1, 0), r, 0)),
                  pl.BlockSpec((None, tr, d), lambda p, r: (p, r, 0)), vec, vec],
        out_specs=(pl.BlockSpec((None, tr, d), lambda p, r: (p, r, 0)), vec, vec),
        out_shape=(jax.ShapeDtypeStruct((3, t, d), BF16), jax.ShapeDtypeStruct((1, PAIR), F32), jax.ShapeDtypeStruct((1, PAIR), F32)),
        compiler_params=_cp(("arbitrary", "arbitrary")),
    )(dq, dkv, qkv32, gq, gk)


def _ca_ext_index():
    m = np.arange(CA_EXT)
    return np.where(m <= CA_WIN, np.clip(CA_PAD - m, -MAX_REL, MAX_REL) + MAX_REL, 2 * MAX_REL).astype(np.int32)


def _ca_pad_penalty(i):
    b = lax.broadcasted_iota(jnp.int32, (1, CA_WIN), 1)
    return jnp.where(b >= CA_PAD - CA_TQ * i, 0.0, NEG_BIG)


def _skew(x, sign):
    row = lax.broadcasted_iota(jnp.int32, (CA_TQ, 1), 0)
    for bit in range(CA_TQ.bit_length() - 1):
        amount = (1 << bit) if sign > 0 else CA_EXT - (1 << bit)
        x = jnp.where(((row >> bit) & 1) == 1, pltpu.roll(x, amount, 1), x)
    return x


def _ca_bias_tiles(ext_ref, bias_ref):
    a = lax.broadcasted_iota(jnp.int32, (CA_TQ, CA_EXT), 0) // CHUNK
    b = lax.broadcasted_iota(jnp.int32, (CA_TQ, CA_EXT), 1) // CHUNK
    seen = jnp.logical_and(b >= a, b <= a + LEFT_CHUNKS)
    for h in (0, 1):
        bias_ref[h] = jnp.where(seen, _skew(jnp.broadcast_to(ext_ref[pl.ds(h, 1), :], (CA_TQ, CA_EXT)), 1), NEG_BIG)


def _ca_probs(qh, kw, bias, penalty):
    z = _dot_nt(qh, kw) + bias + penalty
    p = jnp.exp(z - jnp.max(z, axis=1, keepdims=True))
    return p * (1.0 / jnp.sum(p, axis=1, keepdims=True))


def _ca_fwd(qkvn, ext, side=None):
    _, tp, d = qkvn.shape
    t = tp - CA_PAD
    hp, tq = d // PAIR, CA_SUB * CA_TQ
    npad = CA_PAD // tq

    def body(q_ref, k_ref, v_ref, ext_ref, o_ref, bias_ref):
        i = pl.program_id(1)

        @pl.when(i == 0)
        def _():
            _ca_bias_tiles(ext_ref, bias_ref)

        lane = lax.broadcasted_iota(jnp.int32, (1, PAIR), 1)
        for sub in range(CA_SUB):
            ii = i * CA_SUB + sub
            penalty = _ca_pad_penalty(ii)
            win = pl.ds(pl.multiple_of(ii * CA_TQ, CA_TQ), CA_WIN)
            kw, vw, q = k_ref[win, :], v_ref[win, :], q_ref[sub * CA_TQ:(sub + 1) * CA_TQ, :]
            head = [(lane // HEAD_DIM) == h for h in (0, 1)]
            q2 = jnp.concatenate([jnp.where(hm, q, jnp.zeros_like(q)) for hm in head], axis=0) * jnp.asarray(HEAD_DIM ** -0.5, BF16)
            bias2 = jnp.concatenate([bias_ref[h, :, :CA_WIN] for h in (0, 1)], axis=0)
            pb = _ca_probs(q2, kw, bias2, penalty).astype(BF16)
            o = jnp.zeros((CA_TQ, PAIR), F32)
            for h in (0, 1):
                o = o + _dot(pb[h * CA_TQ:(h + 1) * CA_TQ], jnp.where(head[h], vw, jnp.zeros_like(vw)))
            o_ref[sub * CA_TQ:(sub + 1) * CA_TQ, :] = o.astype(BF16)

    return _side_call(
        "ca_fwd", body, (hp, t // tq),
        [pl.BlockSpec((None, tq, PAIR), lambda p, i: (0, i + npad, p)),
         pl.BlockSpec((None, tp, PAIR), lambda p, i: (1, 0, p)),
         pl.BlockSpec((None, tp, PAIR), lambda p, i: (2, 0, p)),
         pl.BlockSpec((None, 2, CA_EXT), lambda p, i: (p, 0, 0))],
        [pl.BlockSpec((tq, PAIR), lambda p, i: (i, p))], [jax.ShapeDtypeStruct((t, d), BF16)],
        [pltpu.VMEM((2, CA_TQ, CA_EXT), F32)], (qkvn, qkvn, qkvn, ext), side, ("parallel", "arbitrary"))


def _ca_bwd(qkvn, ext, do, side=None):
    _, tp, d = qkvn.shape
    t = tp - CA_PAD
    hp, tq = d // PAIR, CA_SUB * CA_TQ
    npad, nq = CA_PAD // tq, t // tq
    scale = HEAD_DIM ** -0.5

    def body(q_ref, k_ref, v_ref, ext_ref, do_ref, dq_ref, dkv_ref, dext_ref, bias_ref, dbias_ref, dk_acc, dv_acc):
        i = pl.program_id(1)

        @pl.when(i == 0)
        def _():
            _ca_bias_tiles(ext_ref, bias_ref)
            dbias_ref[...] = jnp.zeros_like(dbias_ref)
            dk_acc[...] = jnp.zeros_like(dk_acc)
            dv_acc[...] = jnp.zeros_like(dv_acc)

        lane = lax.broadcasted_iota(jnp.int32, (1, PAIR), 1)
        dbias = [None, None]
        dk_u = dv_u = None

        def spread(v, sub):
            parts = [jnp.zeros((sub * CA_TQ, PAIR), F32)] * (sub > 0) + [v] + \
                    [jnp.zeros(((CA_SUB - 1 - sub) * CA_TQ, PAIR), F32)] * (sub < CA_SUB - 1)
            return jnp.concatenate(parts, axis=0) if len(parts) > 1 else v

        for sub in range(CA_SUB):
            ii = i * CA_SUB + sub
            penalty = _ca_pad_penalty(ii)
            win = pl.ds(pl.multiple_of(ii * CA_TQ, CA_TQ), CA_WIN)
            rows = slice(sub * CA_TQ, (sub + 1) * CA_TQ)
            kw, vw, q, dob = k_ref[win, :], v_ref[win, :], q_ref[rows, :], do_ref[rows, :]
            head = [(lane // HEAD_DIM) == h for h in (0, 1)]
            zb = jnp.zeros_like(q)
            q2 = jnp.concatenate([jnp.where(hm, q, zb) for hm in head], axis=0) * jnp.asarray(scale, BF16)
            do2 = jnp.concatenate([jnp.where(hm, dob, zb) for hm in head], axis=0)
            p = _ca_probs(q2, kw, jnp.concatenate([bias_ref[h, :, :CA_WIN] for h in (0, 1)], axis=0), penalty)
            dp = _dot_nt(do2, vw)
            ds = p * (dp - jnp.sum(p * dp, axis=1, keepdims=True))
            dsb = ds.astype(BF16)
            dq = jnp.zeros((CA_TQ, PAIR), F32)
            for h in (0, 1):
                ds_h = ds[h * CA_TQ:(h + 1) * CA_TQ]
                dbias[h] = ds_h if dbias[h] is None else dbias[h] + ds_h
                dq = dq + _dot(dsb[h * CA_TQ:(h + 1) * CA_TQ], jnp.where(head[h], kw, jnp.zeros_like(kw)))
            dk_t = _dot_tn(dsb, q2)
            dv_t = _dot_tn(p.astype(BF16), do2)
            dq_ref[rows, :] = dq * scale
            dk_u = spread(dk_t, sub) if dk_u is None else dk_u + spread(dk_t, sub)
            dv_u = spread(dv_t, sub) if dv_u is None else dv_u + spread(dv_t, sub)
        union = pl.ds(pl.multiple_of(i * tq, CA_TQ), CA_WIN + (CA_SUB - 1) * CA_TQ)
        dk_acc[union, :] += dk_u
        dv_acc[union, :] += dv_u
        for h in (0, 1):
            dbias_ref[h, :, :CA_WIN] += dbias[h]

        @pl.when(i == nq - 1)
        def _():
            dkv_ref[0, :, :] = dk_acc[CA_PAD:, :]
            dkv_ref[1, :, :] = dv_acc[CA_PAD:, :]
            for h in (0, 1):
                dext_ref[pl.ds(h, 1), :] = jnp.sum(_skew(dbias_ref[h], -1), axis=0, keepdims=True)

    es = pl.BlockSpec((None, 2, CA_EXT), lambda p, i: (p, 0, 0))
    ts = pl.BlockSpec((tq, PAIR), lambda p, i: (i, p))
    return _side_call(
        "ca_bwd", body, (hp, nq),
        [pl.BlockSpec((None, tq, PAIR), lambda p, i: (0, i + npad, p)),
         pl.BlockSpec((None, tp, PAIR), lambda p, i: (1, 0, p)),
         pl.BlockSpec((None, tp, PAIR), lambda p, i: (2, 0, p)), es, ts],
        (ts, pl.BlockSpec((2, t, PAIR), lambda p, i: (0, 0, p)), es),
        (jax.ShapeDtypeStruct((t, d), F32), jax.ShapeDtypeStruct((2, t, d), F32), jax.ShapeDtypeStruct((hp, 2, CA_EXT), F32)),
        [pltpu.VMEM((2, CA_TQ, CA_EXT), F32), pltpu.VMEM((2, CA_TQ, CA_EXT), F32),
         pltpu.VMEM((tp, PAIR), F32), pltpu.VMEM((tp, PAIR), F32)],
        (qkvn, qkvn, qkvn, ext, do), side, ("parallel", "arbitrary"))


def _loss_head(y, target):
    t, d = y.shape
    tm = _tile(t, 512)

    def body(y_ref, t_ref, dy_ref, loss_ref):
        diff = y_ref[...] - t_ref[...]
        dy_ref[...] = diff * (1.0 / d)
        part = 0.5 * jnp.sum(jnp.mean(diff * diff, axis=-1, keepdims=True), axis=0, keepdims=True)

        @pl.when(pl.program_id(0) == 0)
        def _():
            loss_ref[...] = jnp.zeros_like(loss_ref)

        loss_ref[...] += jnp.broadcast_to(part, loss_ref.shape)

    row = pl.BlockSpec((tm, d), lambda m: (m, 0))
    return pl.pallas_call(
        body, name="loss_head", grid=(t // tm,), in_specs=[row, row],
        out_specs=(row, pl.BlockSpec((8, 128), lambda m: (0, 0))),
        out_shape=(jax.ShapeDtypeStruct((t, d), F32), jax.ShapeDtypeStruct((8, 128), F32)),
        compiler_params=_cp(("arbitrary",)),
    )(y, target)


def _adamw_math(w, g, m, v):
    m = ADAM_B1 * m + (1.0 - ADAM_B1) * g
    v = ADAM_B2 * v + (1.0 - ADAM_B2) * (g * g)
    m_hat = m / (1.0 - ADAM_B1 ** ADAM_STEP)
    v_hat = v / (1.0 - ADAM_B2 ** ADAM_STEP)
    delta = -ADAM_LR * (m_hat / (jnp.sqrt(v_hat) + ADAM_EPS) + ADAM_WD * w)
    return delta, m, v


def _adamw(name, w, g, m, v):
    shape = w.shape
    cols = shape[-1]
    rows = int(np.prod(shape[:-1]))
    tr = _tile(rows, 512)
    flat = [a.reshape(rows, cols) for a in (w, g, m, v)]

    def body(w_ref, g_ref, m_ref, v_ref, d_ref, nm_ref, nv_ref):
        d_ref[...], nm_ref[...], nv_ref[...] = _adamw_math(w_ref[...], g_ref[...], m_ref[...], v_ref[...])

    blk = pl.BlockSpec((tr, cols), lambda r: (r, 0))
    outs = pl.pallas_call(
        body, name=name, grid=(rows // tr,), in_specs=[blk] * 4, out_specs=(blk,) * 3,
        out_shape=(jax.ShapeDtypeStruct((rows, cols), F32),) * 3, compiler_params=_cp(("parallel",)),
    )(*flat)
    return [o.reshape(shape) for o in outs]


def _place():
    x, y, c = lax.axis_index("x"), lax.axis_index("y"), lax.axis_index("c")
    chips = [(1 - x, y), (x, 1 - y), (1 - x, 1 - y)]
    return x, y, c, chips


def _shard_slab(ref, kind, layer0, n_layers, shard, width):
    lay = pl.ds(layer0, n_layers)
    if kind == "cols":
        return ref.at[lay, :, pl.ds(shard * width, width)]
    return ref.at[lay, pl.ds(shard * width, width), :]


def _cast_into_place(w, kind, shard):
    n_layers, rows, cols = w.shape
    tr = _tile(rows, 512)
    nr = rows // tr

    def body(s_ref, w_ref, o_ref):
        o_ref[...] = w_ref[...].astype(BF16)

    if kind == "cols":
        full, out = (n_layers, rows, cols * N_CHIPS), pl.BlockSpec((None, tr, cols), lambda l, r, s_ref: (l, r, s_ref[0]))
    else:
        full, out = (n_layers, rows * N_CHIPS, cols), pl.BlockSpec((None, tr, cols), lambda l, r, s_ref: (l, s_ref[0] * nr + r, 0))
    grid_spec = pltpu.PrefetchScalarGridSpec(
        num_scalar_prefetch=1, grid=(n_layers, nr),
        in_specs=[pl.BlockSpec((None, tr, cols), lambda l, r, s_ref: (l, r, 0))], out_specs=out)
    return pl.pallas_call(body, name="cast_into_place", grid_spec=grid_spec, out_shape=jax.ShapeDtypeStruct(full, BF16),
                          compiler_params=_cp(("parallel", "parallel")))(shard, w)


def _gather_side(items, fulls, kinds, widths):
    n = len(items)

    def each(w, send, recv, owner_does, other_does=None):
        x, y, c, chips = _place()
        for e, (a, layer) in enumerate(items):
            def copy(k, shard, to, a=a, layer=layer):
                slab = _shard_slab(w[a], kinds[a], layer, 1, shard, widths[a])
                return pltpu.make_async_remote_copy(src_ref=slab, dst_ref=slab, send_sem=send.at[k], recv_sem=recv.at[k],
                                                    device_id=to, device_id_type=MESH)

            @pl.when(c == layer % 2)
            def _(e=e, copy=copy):
                for j, (cx, cy) in enumerate(chips):
                    owner_does(copy, j * n + e, 3 * n + j * n + e, x, y, c, cx, cy)

            if other_does is not None:
                @pl.when(c != layer % 2)
                def _(e=e, copy=copy):
                    for j, (cx, cy) in enumerate(chips):
                        other_does(copy, 3 * n + j * n + e, x, y, c, cx, cy)

    def start(_, w, send, recv):
        each(w, send, recv, lambda copy, k, kf, x, y, c, cx, cy: copy(k, 2 * x + y, (cx, cy, c)).start())

    def mid(_, w, send, recv):
        def forward(copy, k, kf, x, y, c, cx, cy):
            copy(k, 2 * cx + cy, (x, y, c)).wait_recv()
            copy(kf, 2 * cx + cy, (x, y, 1 - c)).start()
        each(w, send, recv, forward)

    def finish(_, w, send, recv):
        def sent(copy, k, kf, x, y, c, cx, cy):
            copy(k, 2 * x + y, (cx, cy, c)).wait_send()
            copy(kf, 2 * cx + cy, (x, y, 1 - c)).wait_send()
        each(w, send, recv, sent, lambda copy, kf, x, y, c, cx, cy: copy(kf, 2 * cx + cy, (x, y, c)).wait_recv())

    return _Side(fulls, [True] * len(fulls), [], 6 * n, 6 * n, start, mid, finish)


def _shard_piece(ref, kind, shard, width):
    return ref.at[:, shard * width:(shard + 1) * width] if kind == "cols" else ref.at[shard * width:(shard + 1) * width, :]


def _reduce_side(layer, grads, kinds, widths):
    na = len(grads)
    owner = layer % 2
    new_outs = []
    for g, k, w in zip(grads, kinds, widths):
        new_outs.append(jax.ShapeDtypeStruct((N_DEV,) + ((g.shape[0], w) if k == "cols" else (w, g.shape[1])), g.dtype))

    def sends(g, got, send, recv, act):
        x, y, c, _ = _place()
        me = 4 * x + 2 * y + c
        for b in range(N_CHIPS):
            bx, by = b >> 1, b & 1
            to_me = jnp.logical_and(jnp.logical_and(x == bx, y == by), c == owner)

            @pl.when(jnp.logical_not(to_me))
            def _(b=b, bx=bx, by=by):
                for a in range(na):
                    cp = pltpu.make_async_remote_copy(
                        src_ref=_shard_piece(g[a], kinds[a], b, widths[a]), dst_ref=got[a].at[me], send_sem=send.at[b * na + a],
                        recv_sem=recv.at[me * na + a], device_id=(bx, by, owner), device_id_type=MESH)
                    cp.start() if act == "start" else cp.wait_send()

    def start(g, got, send, recv):
        sends(g, got, send, recv, "start")

    def mid(g, got, send, recv):
        pass

    def finish(g, got, send, recv):
        sends(g, got, send, recv, "wait")
        x, y, c, _ = _place()
        me = 4 * x + 2 * y + c

        @pl.when(c == owner)
        def _():
            for s in range(N_DEV):
                @pl.when(me != s)
                def _(s=s):
                    for a in range(na):
                        pltpu.make_async_remote_copy(
                            src_ref=_shard_piece(g[a], kinds[a], 0, widths[a]), dst_ref=got[a].at[s], send_sem=send.at[0],
                            recv_sem=recv.at[s * na + a], device_id=(x, y, c), device_id_type=MESH).wait_recv()

    return _Side(grads, [False] * na, new_outs, N_CHIPS * na, N_DEV * na, start, mid, finish)


def _reduce_sum(layer, n_layers, g, got, kind, width, prev, flags):
    _, rows, cols = got.shape
    tr = _tile(rows, 256)
    nr = rows // tr

    def body(f_ref, g_ref, got_ref, *rest):
        o_ref = rest[-1]

        @pl.when(f_ref[2] == 1)
        def _():
            acc = None
            for s in range(N_DEV):
                term = jnp.where(f_ref[1] == s, g_ref[...], got_ref[s]).astype(F32)
                acc = term if acc is None else acc + term
            o_ref[...] = acc

    if kind == "cols":
        own = pl.BlockSpec((tr, width), lambda r, f_ref: (r, f_ref[0]))
    else:
        own = pl.BlockSpec((tr, cols), lambda r, f_ref: (f_ref[0] * nr + r, 0))
    grid_spec = pltpu.PrefetchScalarGridSpec(
        num_scalar_prefetch=1, grid=(nr,),
        in_specs=[own, pl.BlockSpec((N_DEV, tr, cols), lambda r, f_ref: (0, r, 0))] + ([] if prev is None else [ANY]),
        out_specs=pl.BlockSpec((None, tr, cols), lambda r, f_ref: (layer, r, 0)))
    return pl.pallas_call(
        body, name="reduce_sum", grid_spec=grid_spec, out_shape=jax.ShapeDtypeStruct((n_layers, rows, cols), F32),
        input_output_aliases={} if prev is None else {3: 0}, compiler_params=_cp(("arbitrary",)),
    )(flags, g, got, *([] if prev is None else [prev]))


def _pair_share(sums):
    n_layers = sums[0].shape[0]
    na = len(sums)

    def body(*refs):
        dst = refs[na:2 * na]
        send_sems, recv_sems = refs[2 * na:]
        x, y, c, _ = _place()

        def swap(l, a):
            return pltpu.make_async_remote_copy(
                src_ref=dst[a].at[l], dst_ref=dst[a].at[l], send_sem=send_sems.at[l * na + a],
                recv_sem=recv_sems.at[l * na + a], device_id=(x, y, 1 - c), device_id_type=MESH)

        for l in range(n_layers):
            @pl.when(c == l % 2)
            def _(l=l):
                for a in range(na):
                    swap(l, a).start()
        for l in range(n_layers):
            @pl.when(c == l % 2)
            def _(l=l):
                for a in range(na):
                    swap(l, a).wait_send()

            @pl.when(c != l % 2)
            def _(l=l):
                for a in range(na):
                    swap(l, a).wait_recv()

    return pl.pallas_call(
        body, name="pair_share", in_specs=[ANY] * na, out_specs=[ANY] * na,
        out_shape=[jax.ShapeDtypeStruct(s.shape, s.dtype) for s in sums], input_output_aliases={a: a for a in range(na)},
        scratch_shapes=[pltpu.SemaphoreType.DMA((n_layers * na,)), pltpu.SemaphoreType.DMA((n_layers * na,))],
    )(*sums)


def _small_step(g_part, w, m, v):
    r = g_part.shape[0]

    def body(g_ref, w_ref, m_ref, v_ref, go_ref, d_ref, nm_ref, nv_ref, all_ref, send_sems, recv_sems):
        x, y, c, _ = _place()
        me = 4 * x + 2 * y + c
        all_ref[me] = g_ref[...]
        cps = []
        for k in range(1, N_DEV):
            px, py, pc = (x + (k >> 2)) % 2, (y + ((k >> 1) & 1)) % 2, (c + (k & 1)) % 2
            cps.append(pltpu.make_async_remote_copy(src_ref=g_ref, dst_ref=all_ref.at[me], send_sem=send_sems.at[k - 1],
                                                    recv_sem=recv_sems.at[k - 1], device_id=(px, py, pc), device_id_type=MESH))
        for cp in cps:
            cp.start()
        for cp in cps:
            cp.wait()
        g = all_ref[0]
        for k in range(1, N_DEV):
            g = g + all_ref[k]
        go_ref[...] = g
        d_ref[...], nm_ref[...], nv_ref[...] = _adamw_math(w_ref[...], g, m_ref[...], v_ref[...])

    vm = pl.BlockSpec(memory_space=pltpu.VMEM)
    return pl.pallas_call(
        body, name="small_step", in_specs=[vm] * 4, out_specs=[vm] * 4, out_shape=[jax.ShapeDtypeStruct((r, 128), F32)] * 4,
        scratch_shapes=[pltpu.VMEM((N_DEV, r, 128), F32), pltpu.SemaphoreType.DMA((N_DEV - 1,)), pltpu.SemaphoreType.DMA((N_DEV - 1,))],
    )(g_part, w, m, v)


def _onehot_mm(name, a, onehot):
    def body(a_ref, oh_ref, o_ref):
        v = a_ref[...]
        oh = oh_ref[...]
        hi, lo = _split_bf16(v)
        lo2 = (v - hi.astype(F32) - lo.astype(F32)).astype(BF16)
        o_ref[...] = _dot(hi, oh) + _dot(lo, oh) + _dot(lo2, oh)

    return pl.pallas_call(body, name=name, out_shape=jax.ShapeDtypeStruct((a.shape[0], onehot.shape[1]), F32))(a, onehot)


def _pack_small(parts):
    flat = jnp.concatenate([p.reshape(-1) for p in parts])
    n = flat.shape[0]
    rows = -(-n // 128)
    rows = -(-rows // 8) * 8
    return jnp.pad(flat, (0, rows * 128 - n)).reshape(rows, 128)


def _unpack_small(packed, like):
    flat = packed.reshape(-1)
    out, off = [], 0
    for p in like:
        out.append(flat[off:off + p.size].reshape(p.shape))
        off += p.size
    return out


def kernel(x, mix_norm, w_qkv, w_o, q_norm, k_norm, rel_bias, ffn_norm, w_up, w_down, loss_target, m_mix_norm, m_w_qkv, m_w_o, m_q_norm, m_k_norm, m_rel_bias, m_ffn_norm, m_w_up, m_w_down, v_mix_norm, v_w_qkv, v_w_o, v_q_norm, v_k_norm, v_rel_bias, v_ffn_norm, v_w_up, v_w_down):
    n_layers, d = mix_norm.shape
    t = x.shape[1]
    ff = w_down.shape[1] * N_CHIPS
    heads = d // HEAD_DIM
    cx, cy, cc = lax.axis_index("x"), lax.axis_index("y"), lax.axis_index("c")
    shard = (2 * cx + cy).astype(jnp.int32).reshape(1)
    core = cc.astype(jnp.int32).reshape(1)

    big = [w_qkv, w_o, w_up, w_down]
    kinds = ["cols", "rows", "cols", "rows"]
    widths = [w_qkv.shape[2], w_o.shape[1], w_up.shape[2], w_down.shape[1]]
    fulls = _blocking("gather_first", _gather_side([(0, 0)], [_cast_into_place(w, k, shard) for w, k in zip(big, kinds)],
                                                   kinds, widths))

    rel_pad = -(-N_REL // PAIR) * PAIR
    ext_hot = _ca_ext_index()[:, None] == np.arange(rel_pad)[None, :]
    fold_hot, spread_hot = jnp.asarray(ext_hot, BF16), jnp.asarray(ext_hot.T, BF16)
    rel_tab = jnp.pad(rel_bias, ((0, 0), (0, 0), (0, rel_pad - N_REL)))
    exts = [_onehot_mm("relbias_spread", rel_tab[i], spread_hot).reshape(heads // 2, 2, CA_EXT) for i in range(n_layers // 2)]

    xs, h1s, qkvs, os_, o32s, x2s, h2s, ss, us, q32s = [], [], [], [], [], [], [], [], [], []
    xc = x[0]
    h = _rms_first(xc, mix_norm[0:1])
    for layer in range(n_layers):
        xs.append(xc)
        h1s.append(h)
        side = _gather_side([(1, layer), (2, layer), (3, layer)] + ([(0, layer + 1)] if layer + 1 < n_layers else []),
                            fulls, kinds, widths)
        if layer % 2 == 0:
            qkv = _qkv_proj(h, fulls[0], layer, BF16)
            (o, o32), gathered = _sb_fwd(qkv, side)
            q32s.append(None)
        else:
            idx = layer // 2
            q32 = _qkv_proj(h, fulls[0], layer, F32)
            gq = jnp.tile(q_norm[idx], 2).reshape(1, PAIR)
            gk = jnp.tile(k_norm[idx], 2).reshape(1, PAIR)
            qkv = _ca_prep(q32, gq, gk)
            (o,), gathered = _ca_fwd(qkv, exts[idx], side)
            o32 = None
            q32s.append(q32)
        fulls = gathered
        wq, wo, wu, wd = fulls
        qkvs.append(qkv)
        os_.append(o)
        o32s.append(o32)
        x2, h2 = _out_proj("attn_out", o, wo, layer, xc, ffn_norm[layer:layer + 1])
        s, u = _up_proj(h2, wu, layer)
        nxt = mix_norm[layer + 1:layer + 2] if layer + 1 < n_layers else mix_norm[0:1]
        xc, h = _out_proj("mlp_out", u, wd, layer, x2, nxt)
        x2s.append(x2)
        h2s.append(h2)
        ss.append(s)
        us.append(u)

    dx, loss_part = _loss_head(xc, loss_target[0])
    loss = lax.psum(loss_part[0, 0], ("x", "y", "c"))

    d_mix, d_ffn = [None] * n_layers, [None] * n_layers
    d_qn, d_kn, d_rb = [], [], []
    device = (4 * cx + 2 * cy + cc).astype(jnp.int32).reshape(1)
    sums = [None] * len(big)

    def reduce_sums(of_layer, grads, got):
        flags = jnp.concatenate([shard, device, (core == of_layer % 2).astype(jnp.int32)])
        return [_reduce_sum(of_layer, n_layers, g, r_, k, w_, s_, flags)
                for g, r_, k, w_, s_ in zip(grads, got, kinds, widths, sums)]

    for layer in reversed(range(n_layers)):
        du = _down_bwd(dx, wd, layer, ss[layer])
        g_down = _wgrad("wgrad_down", us[layer], _mat_spec, dx, _mat_spec_b, t, ff, d)
        g_up = _wgrad("wgrad_up", h2s[layer], _mat_spec, du, _mat_spec_b, t, d, ff)
        dx2, d_ffn[layer] = _norm_bwd_proj(
            "up_bwd", du, _rows2(ff), [(None, slice(None), slice(None))], wu, layer, x2s[layer], ffn_norm[layer:layer + 1], dx)
        do = _plain_nt("attn_out_bwd", dx2, wo, layer, BF16)
        g_o = _wgrad("wgrad_o", os_[layer], _mat_spec, dx2, _mat_spec_b, t, d, d)
        side = _reduce_side(layer, [g_o, g_up, g_down], kinds[1:], widths[1:])
        if layer % 2 == 0:
            (dqkv,), got = _sb_bwd(qkvs[layer], o32s[layer], do, side)
        else:
            idx = layer // 2
            gq = jnp.tile(q_norm[idx], 2).reshape(1, PAIR)
            gk = jnp.tile(k_norm[idx], 2).reshape(1, PAIR)
            (dqn, dkv, dext), got = _ca_bwd(qkvs[layer], exts[idx], do, side)
            dqkv, dgq, dgk = _ca_unprep(dqn, dkv, q32s[layer], gq, gk)
            d_qn.insert(0, dgq[0, :HEAD_DIM])
            d_kn.insert(0, dgk[0, :HEAD_DIM])
            d_rb.insert(0, _onehot_mm("relbias_fold", dext.reshape(heads, CA_EXT), fold_hot)[:, :N_REL])
        g_qkv = _wgrad("wgrad_qkv", h1s[layer], _mat_spec, dqkv, _plane_spec_b(d), t, d, 3 * d, tno=_tile(d, 1024))
        (dx, d_mix[layer]), got_qkv = _norm_bwd_proj(
            "qkv_bwd", dqkv, lambda tm: pl.BlockSpec((3, tm, d), lambda m: (0, m, 0)),
            [(p, slice(None), slice(p * d, (p + 1) * d)) for p in range(3)], wq, layer, xs[layer], mix_norm[layer:layer + 1], dx2,
            side=_reduce_side(layer, [g_qkv], kinds[:1], widths[:1]))
        sums = reduce_sums(layer, [g_qkv, g_o, g_up, g_down], got_qkv + got)
    grad_x = dx.reshape(x.shape)
    g_big = _pair_share(sums)

    upd = [_adamw("adamw", w_, g_, m_, v_) for w_, g_, m_, v_ in
           zip(big, g_big, [m_w_qkv, m_w_o, m_w_up, m_w_down], [v_w_qkv, v_w_o, v_w_up, v_w_down])]

    small_w = [mix_norm, q_norm, k_norm, rel_bias, ffn_norm]
    small_g = [jnp.concatenate(d_mix, 0), jnp.stack(d_qn), jnp.stack(d_kn), jnp.stack(d_rb), jnp.concatenate(d_ffn, 0)]
    packed = _small_step(_pack_small(small_g), _pack_small(small_w),
                         _pack_small([m_mix_norm, m_q_norm, m_k_norm, m_rel_bias, m_ffn_norm]),
                         _pack_small([v_mix_norm, v_q_norm, v_k_norm, v_rel_bias, v_ffn_norm]))
    sg, sd, sm, sv = [_unpack_small(p, small_w) for p in packed]

    def order(small, bigs):
        return [small[0], bigs[0], bigs[1], small[1], small[2], small[3], small[4], bigs[2], bigs[3]]

    return (loss, grad_x, *order(sg, g_big), *order(sd, [u_[0] for u_ in upd]),
            *order(sm, [u_[1] for u_ in upd]), *order(sv, [u_[2] for u_ in upd]))
```

```python
import functools

import numpy as np
import jax
import jax.numpy as jnp
from jax import lax
from jax.experimental import pallas as pl
from jax.experimental.pallas import tpu as pltpu

F32, BF16 = jnp.float32, jnp.bfloat16
MESH = pl.DeviceIdType.MESH

HEAD_DIM = 64
PAIR = 128
CHUNK = 64
LEFT_CHUNKS = 8
MAX_REL = 256
N_REL = 2 * MAX_REL + 1
CA_TQ = 128
CA_SUB = 4
CA_PAD = LEFT_CHUNKS * CHUNK
assert CA_PAD % (CA_SUB * CA_TQ) == 0
CA_WIN = CA_PAD + CA_TQ
CA_EXT = CA_WIN + CA_TQ
SB_T = 128
SB_WIN = 3
SB_SUB = 8
RMS_EPS = 1e-6
SB_EXIT = -104.0
NEG_BIG = -1e30
ADAM_LR, ADAM_B1, ADAM_B2, ADAM_EPS, ADAM_WD, ADAM_STEP = 0.001, 0.9, 0.999, 1e-08, 0.01, 10
VMEM_LIMIT = 56 << 20
N_CHIPS = 4
N_DEV = 8
ANY = pl.BlockSpec(memory_space=pl.ANY)


def _cp(sem=None, **kw):
    return pltpu.CompilerParams(dimension_semantics=sem, vmem_limit_bytes=VMEM_LIMIT, **kw)


def _tile(n, want):
    t = min(n, want)
    assert n % t == 0, (n, t)
    return t


class _Side:
    def __init__(self, ins, aliased, new_outs, n_send, n_recv, start, mid, finish):
        self.ins, self.aliased, self.new_outs = list(ins), list(aliased), list(new_outs)
        self.n_send, self.n_recv, self.start, self.mid, self.finish = n_send, n_recv, start, mid, finish

    def out_shapes(self):
        return [jax.ShapeDtypeStruct(a.shape, a.dtype) for a, al in zip(self.ins, self.aliased) if al] + self.new_outs


def _side_call(name, body, grid, in_specs, out_specs, out_shape, scratch_shapes, args, side, sem):
    out_shape, out_specs = list(out_shape), list(out_specs)
    if side is None:
        outs = pl.pallas_call(body, name=name, grid=grid, in_specs=in_specs, out_specs=out_specs, out_shape=out_shape,
                              scratch_shapes=scratch_shapes, compiler_params=_cp(sem))(*args)
        return list(outs), []
    n_in, n_out, n_sc, n_sin = len(in_specs), len(out_shape), len(scratch_shapes), len(side.ins)
    s_outs = side.out_shapes()

    def wrapped(*refs):
        ins, s_in = refs[:n_in], refs[n_in:n_in + n_sin]
        outs = refs[n_in + n_sin:n_in + n_sin + n_out]
        s_out = refs[n_in + n_sin + n_out:n_in + n_sin + n_out + len(s_outs)]
        scr = refs[n_in + n_sin + n_out + len(s_outs):]
        send, recv = scr[n_sc], scr[n_sc + 1]
        ids = [pl.program_id(k) for k in range(len(grid))]

        def at(first_axis, rest):
            ok = ids[0] == first_axis
            for k in range(1, len(grid)):
                ok = jnp.logical_and(ok, ids[k] == (grid[k] - 1 if rest == "last" else 0))
            return ok

        @pl.when(at(0, "first"))
        def _():
            side.start(s_in, s_out, send, recv)

        body(*ins, *outs, *scr[:n_sc])

        @pl.when(at(grid[0] // 2, "first"))
        def _():
            side.mid(s_in, s_out, send, recv)

        @pl.when(at(grid[0] - 1, "last"))
        def _():
            side.finish(s_in, s_out, send, recv)

    aliases, pos = {}, n_out
    for k, al in enumerate(side.aliased):
        if al:
            aliases[n_in + k] = pos
            pos += 1
    outs = pl.pallas_call(
        wrapped, name=name, grid=grid, in_specs=list(in_specs) + [ANY] * n_sin, out_specs=out_specs + [ANY] * len(s_outs),
        out_shape=out_shape + s_outs, input_output_aliases=aliases,
        scratch_shapes=list(scratch_shapes) + [pltpu.SemaphoreType.DMA((side.n_send,)), pltpu.SemaphoreType.DMA((side.n_recv,))],
        compiler_params=_cp(("arbitrary",) * len(grid)),
    )(*args, *side.ins)
    return list(outs[:n_out]), list(outs[n_out:])


def _blocking(name, side):
    s_outs = side.out_shapes()
    n_sin = len(side.ins)

    def body(*refs):
        s_in, s_out = refs[:n_sin], refs[n_sin:n_sin + len(s_outs)]
        send, recv = refs[n_sin + len(s_outs):]
        side.start(s_in, s_out, send, recv)
        side.mid(s_in, s_out, send, recv)
        side.finish(s_in, s_out, send, recv)

    aliases, pos = {}, 0
    for k, al in enumerate(side.aliased):
        if al:
            aliases[k] = pos
            pos += 1
    return list(pl.pallas_call(
        body, name=name, in_specs=[ANY] * n_sin, out_specs=[ANY] * len(s_outs), out_shape=s_outs, input_output_aliases=aliases,
        scratch_shapes=[pltpu.SemaphoreType.DMA((side.n_send,)), pltpu.SemaphoreType.DMA((side.n_recv,))],
    )(*side.ins))


def _split_bf16(v):
    hi = v.astype(BF16)
    lo = (v - hi.astype(F32)).astype(BF16)
    return hi, lo


def _dot(a, b):
    return jnp.dot(a, b, preferred_element_type=F32)


def _dot_nt(a, b):
    return lax.dot_general(a, b, (((1,), (1,)), ((), ())), preferred_element_type=F32)


def _dot_tn(a, b):
    return lax.dot_general(a, b, (((0,), (0,)), ((), ())), preferred_element_type=F32)


TM = 512
TN = 1024


def _row_matmul(name, a, a_spec, w, layer, nt, chunks, extra, extra_specs, out_shape, out_specs, epilogue, t,
                sem=("parallel",), side=None):
    tm = _tile(t, TM)
    n_extra = len(extra)

    def body(a_ref, w_ref, *rest):
        ins, outs = rest[:n_extra], rest[n_extra:]
        for n, terms in enumerate(chunks):
            acc = None
            for plane, rows, cols in terms:
                av = (a_ref[...] if plane is None else a_ref[plane]).astype(BF16)
                part = _dot_nt(av, w_ref[rows, cols]) if nt else _dot(av, w_ref[rows, cols])
                acc = part if acc is None else acc + part
            epilogue(n, acc, ins, outs)

    w_spec = pl.BlockSpec((None,) + w.shape[1:], lambda m: (layer, 0, 0))
    outs, side_outs = _side_call(name, body, (t // tm,), [a_spec(tm), w_spec, *extra_specs], out_specs, out_shape, [],
                                 (a, w, *extra), side, sem)
    return outs if side is None else (outs, side_outs)


def _rows2(width):
    return lambda tm: pl.BlockSpec((tm, width), lambda m: (m, 0))


def _col_chunks(n, width=None):
    tn = _tile(n, TN)
    return [[(None, slice(None), slice(c * tn, (c + 1) * tn))] for c in range(n // tn)], tn


def _rms_rows(x, gain):
    r = lax.rsqrt(jnp.mean(x * x, axis=-1, keepdims=True) + RMS_EPS)
    return x * r * gain


def _rms_bwd_rows(dh, x, gain, dres):
    r = lax.rsqrt(jnp.mean(x * x, axis=-1, keepdims=True) + RMS_EPS)
    xhat = x * r
    dxn = dh * gain
    dx = r * (dxn - xhat * jnp.mean(dxn * xhat, axis=-1, keepdims=True))
    return dx + dres, jnp.sum(dh * xhat, axis=0, keepdims=True)


def _rms_first(x, gain):
    t, d = x.shape
    tm = _tile(t, 512)

    def body(x_ref, g_ref, h_ref):
        h_ref[...] = _rms_rows(x_ref[...], g_ref[...]).astype(BF16)

    return pl.pallas_call(
        body, name="rms_first", grid=(t // tm,),
        in_specs=[pl.BlockSpec((tm, d), lambda m: (m, 0)), pl.BlockSpec((1, d), lambda m: (0, 0))],
        out_specs=pl.BlockSpec((tm, d), lambda m: (m, 0)), out_shape=jax.ShapeDtypeStruct((t, d), BF16),
        compiler_params=_cp(("parallel",)),
    )(x, gain)


def _qkv_proj(h, w, layer, out_dtype):
    t, d = h.shape
    chunks = [[(None, slice(None), slice(p * d, (p + 1) * d))] for p in range(3)]

    def epi(n, acc, ins, outs):
        outs[0][n] = acc.astype(out_dtype)

    return _row_matmul(
        "qkv_proj", h, _rows2(d), w, layer, False, chunks, (), (), (jax.ShapeDtypeStruct((3, t, d), out_dtype),),
        (pl.BlockSpec((3, _tile(t, TM), d), lambda m: (0, m, 0)),), epi, t)[0]


def _out_proj(name, a, w, layer, res, gain):
    t, kk = a.shape
    d = res.shape[1]

    def epi(n, acc, ins, outs):
        xn = ins[0][...] + acc
        outs[0][...] = xn
        outs[1][...] = _rms_rows(xn, ins[1][...]).astype(BF16)

    row = _rows2(d)(_tile(t, TM))
    return _row_matmul(
        name, a, _rows2(kk), w, layer, False, [[(None, slice(None), slice(None))]], (res, gain),
        (row, pl.BlockSpec((1, d), lambda m: (0, 0))),
        (jax.ShapeDtypeStruct((t, d), F32), jax.ShapeDtypeStruct((t, d), BF16)), (row, row), epi, t)


def _up_proj(h, w, layer):
    t, d = h.shape
    ff = w.shape[2]
    chunks, tn = _col_chunks(ff)

    def epi(n, acc, ins, outs):
        s = jnp.maximum(acc, 0.0)
        outs[0][:, n * tn:(n + 1) * tn] = s.astype(BF16)
        outs[1][:, n * tn:(n + 1) * tn] = (s * s).astype(BF16)

    o = _rows2(ff)(_tile(t, TM))
    return _row_matmul("up_proj", h, _rows2(d), w, layer, False, chunks, (), (), (jax.ShapeDtypeStruct((t, ff), BF16),) * 2, (o, o), epi, t)


def _down_bwd(dx, w, layer, s):
    t, d = dx.shape
    ff = w.shape[1]
    tn = _tile(ff, TN)
    chunks = [[(None, slice(c * tn, (c + 1) * tn), slice(None))] for c in range(ff // tn)]

    def epi(n, acc, ins, outs):
        cols = slice(n * tn, (n + 1) * tn)
        outs[0][:, cols] = (acc * (2.0 * ins[0][:, cols].astype(F32))).astype(BF16)

    o = _rows2(ff)(_tile(t, TM))
    return _row_matmul("down_bwd", dx, _rows2(d), w, layer, True, chunks, (s,), (o,), (jax.ShapeDtypeStruct((t, ff), BF16),), (o,), epi, t)[0]


def _norm_bwd_proj(name, a, a_spec, terms, w, layer, x, gain, dres, side=None):
    t, d = x.shape

    def epi(n, acc, ins, outs):
        dx, dg = _rms_bwd_rows(acc, ins[0][...], ins[1][...], ins[2][...])
        outs[0][...] = dx

        @pl.when(pl.program_id(0) == 0)
        def _():
            outs[1][...] = dg

        @pl.when(pl.program_id(0) > 0)
        def _():
            outs[1][...] += dg

    row = _rows2(d)(_tile(t, TM))
    vec = pl.BlockSpec((1, d), lambda m: (0, 0))
    return _row_matmul(
        name, a, a_spec, w, layer, True, [terms], (x, gain, dres), (row, vec, row),
        (jax.ShapeDtypeStruct((t, d), F32), jax.ShapeDtypeStruct((1, d), F32)), (row, vec), epi, t, sem=("arbitrary",), side=side)


def _plain_nt(name, a, w, layer, out_dtype):
    t, n = a.shape
    m_out = w.shape[1]

    def epi(n_, acc, ins, outs):
        outs[0][...] = acc.astype(out_dtype)

    return _row_matmul(
        name, a, _rows2(n), w, layer, True, [[(None, slice(None), slice(None))]], (), (),
        (jax.ShapeDtypeStruct((t, m_out), out_dtype),), (_rows2(m_out)(_tile(t, TM)),), epi, t)[0]


def _wgrad(name, a, a_spec_fn, b, b_spec_fn, t, mo, no, tno=1024):
    tmo, tno, tk = _tile(mo, 1024), _tile(no, tno), _tile(t, 2048)
    nk = t // tk

    def body(a_ref, b_ref, o_ref, acc_ref):
        part = _dot_tn(a_ref[...].astype(BF16), b_ref[...].astype(BF16))
        k = pl.program_id(2)

        @pl.when(k == 0)
        def _():
            acc_ref[...] = part

        @pl.when(k > 0)
        def _():
            acc_ref[...] += part

        @pl.when(k == nk - 1)
        def _():
            o_ref[...] = acc_ref[...].astype(BF16)

    return pl.pallas_call(
        body, name=name, grid=(mo // tmo, no // tno, nk), in_specs=[a_spec_fn(tk, tmo), b_spec_fn(tk, tno)],
        out_specs=pl.BlockSpec((tmo, tno), lambda m, n, k: (m, n)), out_shape=jax.ShapeDtypeStruct((mo, no), BF16),
        scratch_shapes=[pltpu.VMEM((tmo, tno), F32)], compiler_params=_cp(("parallel", "parallel", "arbitrary")),
    )(a, b)


def _mat_spec(tk, tw):
    return pl.BlockSpec((tk, tw), lambda m, n, k: (k, m))


def _mat_spec_b(tk, tw):
    return pl.BlockSpec((tk, tw), lambda m, n, k: (k, n))


def _plane_spec_b(d):
    def fn(tk, tw):
        npp = d // tw
        return pl.BlockSpec((None, tk, tw), lambda m, n, k: (n // npp, k, n % npp))
    return fn


def _sb_masks(t):
    row = lax.broadcasted_iota(jnp.int32, (t, t), 0)
    col = lax.broadcasted_iota(jnp.int32, (t, t), 1)
    lane = lax.broadcasted_iota(jnp.int32, (1, PAIR), 1)
    return row, col, [(lane // HEAD_DIM) == h for h in (0, 1)]


def _suffix_sums(v, tri, tail, tt):
    hi, lo = _split_bf16(v)
    parts = []
    for b in reversed(range(v.shape[1] // tt)):
        cols = slice(b * tt, (b + 1) * tt)
        parts.insert(0, _dot(hi[:, cols], tri) + _dot(lo[:, cols], tri) + tail)
        tail = tail + jnp.sum(v[:, cols], axis=1, keepdims=True)
    return (parts[0] if len(parts) == 1 else jnp.concatenate(parts, axis=1)), tail


def _sb_tile(qh, kb, valid, after, carry, tt):
    z = _dot_nt(qh, kb)
    lb = -(jnp.maximum(z, 0.0) + jnp.log(1.0 + jnp.exp(-jnp.abs(z))))
    if valid is not None:
        lb = jnp.where(valid, lb, 0.0)
    between, carry = _suffix_sums(lb, after, carry, tt)
    a = jnp.exp(z + lb + between)
    if valid is not None:
        a = jnp.where(valid, a, 0.0)
    return lb, a, carry


def _sb_window_valid(i, first, tt):
    t_pos = i * tt + lax.broadcasted_iota(jnp.int32, (tt, SB_WIN * tt), 0)
    s_pos = first * tt + lax.broadcasted_iota(jnp.int32, (tt, SB_WIN * tt), 1)
    return s_pos < t_pos


def _sb_walk_on(st):
    return jnp.logical_and(st[0] >= 0, jnp.max(st[1]) > SB_EXIT)


def _sb_fwd(qkv, side=None):
    _, t, d = qkv.shape
    hp, tt = d // PAIR, _tile(t, SB_T)

    def body(q_ref, k_ref, v_ref, o_ref, o32_ref):
        row, col, head = _sb_masks(tt)
        after = (row > col).astype(BF16)

        def tile(kb, vb, valid, carry, acc, q2):
            _, a, carry = _sb_tile(q2, kb, valid, after, carry, tt)
            ab = a.astype(BF16)
            for h in (0, 1):
                acc = acc + _dot(ab[h * tt:(h + 1) * tt], jnp.where(head[h], vb, jnp.zeros_like(vb)))
            return carry, acc

        states = []
        for sub in range(SB_SUB):
            i = pl.program_id(1) * SB_SUB + sub
            q = q_ref[sub * tt:(sub + 1) * tt, :]
            q2 = jnp.concatenate([jnp.where(head[h], q, jnp.zeros_like(q)) for h in (0, 1)], axis=0) * jnp.asarray(HEAD_DIM ** -0.5, BF16)
            first = jnp.maximum(i - (SB_WIN - 1), 0)
            rows_w = pl.ds(pl.multiple_of(first * tt, tt), SB_WIN * tt)
            valid = _sb_window_valid(i, first, tt)
            carry, acc = tile(k_ref[rows_w, :], v_ref[rows_w, :], jnp.concatenate([valid, valid], axis=0),
                              jnp.zeros((2 * tt, 1), F32), jnp.zeros((tt, PAIR), F32), q2)
            states.append((first, carry, acc, q2))

        for sub, (first, carry, acc, q2) in enumerate(states):
            def step(st, q2=q2):
                rows = pl.ds(pl.multiple_of(st[0] * tt, tt), tt)
                carry, acc = tile(k_ref[rows, :], v_ref[rows, :], None, st[1], st[2], q2)
                return st[0] - 1, carry, acc

            o = lax.while_loop(_sb_walk_on, step, (first - 1, carry, acc))[2]
            o_ref[sub * tt:(sub + 1) * tt, :] = o.astype(BF16)
            o32_ref[sub * tt:(sub + 1) * tt, :] = o

    tq = SB_SUB * tt
    qs = pl.BlockSpec((None, tq, PAIR), lambda p, i: (0, i, p))
    ks = pl.BlockSpec((None, t, PAIR), lambda p, i: (1, 0, p))
    vs = pl.BlockSpec((None, t, PAIR), lambda p, i: (2, 0, p))
    os_ = pl.BlockSpec((tq, PAIR), lambda p, i: (i, p))
    return _side_call(
        "sb_fwd", body, (hp, t // tq), [qs, ks, vs], (os_, os_),
        (jax.ShapeDtypeStruct((t, d), BF16), jax.ShapeDtypeStruct((t, d), F32)), [], (qkv, qkv, qkv), side,
        ("parallel", "arbitrary"))


def _sb_bwd(qkv, o32, do, side=None):
    _, t, d = qkv.shape
    hp, tt = d // PAIR, _tile(t, SB_T)
    nq = t // (SB_SUB * tt)
    scale = HEAD_DIM ** -0.5

    def body(q_ref, k_ref, v_ref, o32_ref, do_ref, dqkv_ref, dk_acc, dv_acc):
        i = pl.program_id(1)
        row, col, head = _sb_masks(tt)
        after = (row > col).astype(BF16)
        from_s = (row >= col).astype(BF16)

        @pl.when(i == 0)
        def _():
            dk_acc[...] = jnp.zeros_like(dk_acc)
            dv_acc[...] = jnp.zeros_like(dv_acc)

        def tile(rows, valid, carry, seen, dq, q2, do2, tot):
            kb, vb = k_ref[rows, :], v_ref[rows, :]
            lb, a, carry = _sb_tile(q2, kb, valid, after, carry, tt)
            ab = a.astype(BF16)
            g = ab.astype(F32) * _dot_nt(do2, vb)
            g_from, seen = _suffix_sums(g, from_s, seen, tt)
            e = jnp.exp(lb)
            dz = g * e - (1.0 - e) * (tot - g_from)
            if valid is not None:
                dz = jnp.where(valid, dz, 0.0)
            dz = dz.astype(BF16)
            for h in (0, 1):
                dq = dq + _dot(dz[h * tt:(h + 1) * tt], jnp.where(head[h], kb, jnp.zeros_like(kb)))
            dk_acc[rows, :] += _dot_tn(dz, q2)
            dv_acc[rows, :] += _dot_tn(ab, do2)
            return carry, seen, dq

        states = []
        for sub in range(SB_SUB):
            ii = i * SB_SUB + sub
            q = q_ref[sub * tt:(sub + 1) * tt, :]
            dob = do_ref[sub * tt:(sub + 1) * tt, :]
            prod = dob.astype(F32) * o32_ref[sub * tt:(sub + 1) * tt, :]
            zb = jnp.zeros_like(q)
            q2 = jnp.concatenate([jnp.where(head[h], q, zb) for h in (0, 1)], axis=0) * jnp.asarray(scale, BF16)
            do2 = jnp.concatenate([jnp.where(head[h], dob, zb) for h in (0, 1)], axis=0)
            tot = jnp.concatenate([jnp.sum(jnp.where(head[h], prod, 0.0), axis=1, keepdims=True) for h in (0, 1)], axis=0)
            first = jnp.maximum(ii - (SB_WIN - 1), 0)
            valid = _sb_window_valid(ii, first, tt)
            zc = jnp.zeros((2 * tt, 1), F32)
            carry, seen, dq = tile(pl.ds(pl.multiple_of(first * tt, tt), SB_WIN * tt), jnp.concatenate([valid, valid], axis=0),
                                   zc, zc, jnp.zeros((tt, PAIR), F32), q2, do2, tot)
            states.append((first, carry, seen, dq, q2, do2, tot))

        for sub, (first, carry, seen, dq, q2, do2, tot) in enumerate(states):
            def step(st, q2=q2, do2=do2, tot=tot):
                carry, seen, dq = tile(pl.ds(pl.multiple_of(st[0] * tt, tt), tt), None, st[1], st[2], st[3], q2, do2, tot)
                return st[0] - 1, carry, seen, dq

            dq = lax.while_loop(_sb_walk_on, step, (first - 1, carry, seen, dq))[3]
            dqkv_ref[0, pl.ds(pl.multiple_of((i * SB_SUB + sub) * tt, tt), tt), :] = (dq * scale).astype(BF16)

        @pl.when(i == nq - 1)
        def _():
            dqkv_ref[1, :, :] = dk_acc[...].astype(BF16)
            dqkv_ref[2, :, :] = dv_acc[...].astype(BF16)

    qs = pl.BlockSpec((None, SB_SUB * tt, PAIR), lambda p, i: (0, i, p))
    ks = pl.BlockSpec((None, t, PAIR), lambda p, i: (1, 0, p))
    vs = pl.BlockSpec((None, t, PAIR), lambda p, i: (2, 0, p))
    ts = pl.BlockSpec((SB_SUB * tt, PAIR), lambda p, i: (i, p))
    return _side_call(
        "sb_bwd", body, (hp, nq), [qs, ks, vs, ts, ts], [pl.BlockSpec((3, t, PAIR), lambda p, i: (0, 0, p))],
        [jax.ShapeDtypeStruct((3, t, d), BF16)], [pltpu.VMEM((t, PAIR), F32), pltpu.VMEM((t, PAIR), F32)],
        (qkv, qkv, qkv, o32, do), side, ("parallel", "arbitrary"))


def _pair_sum_matrix():
    r = lax.broadcasted_iota(jnp.int32, (PAIR, PAIR), 0) // HEAD_DIM
    c = lax.broadcasted_iota(jnp.int32, (PAIR, PAIR), 1) // HEAD_DIM
    return (r == c).astype(BF16)


def _head_mean(v, ones):
    hi, lo = _split_bf16(v)
    return (_dot(hi, ones) + _dot(lo, ones)) * (1.0 / HEAD_DIM)


def _qkv_proj_ca(h, w, layer, gq, gk):
    t, d = h.shape
    tm = CA_PAD
    assert t % tm == 0

    def body(h_ref, w_ref, gq_ref, gk_ref, raw_ref, o_ref):
        @pl.when(pl.program_id(0) == 0)
        def _():
            o_ref[...] = jnp.zeros_like(o_ref)

        @pl.when(pl.program_id(0) > 0)
        def _():
            hv = h_ref[...]
            ones = _pair_sum_matrix()
            for p in range(3):
                acc = _dot(hv, w_ref[:, p * d:(p + 1) * d])
                if p == 2:
                    o_ref[p] = acc.astype(BF16)
                    continue
                raw_ref[p] = acc
                g = (gq_ref, gk_ref)[p][...]
                for c in range(d // PAIR):
                    x = acc[:, c * PAIR:(c + 1) * PAIR]
                    rs = lax.rsqrt(_head_mean(x * x, ones) + RMS_EPS)
                    o_ref[p, :, c * PAIR:(c + 1) * PAIR] = (x * rs * g).astype(BF16)

    vec = pl.BlockSpec((1, PAIR), lambda m: (0, 0))
    return pl.pallas_call(
        body, name="qkv_proj_ca", grid=(t // tm + 1,),
        in_specs=[pl.BlockSpec((tm, d), lambda m: (jnp.maximum(m - 1, 0), 0)),
                  pl.BlockSpec((None, d, 3 * d), lambda m: (layer, 0, 0)), vec, vec],
        out_specs=(pl.BlockSpec((2, tm, d), lambda m: (0, jnp.maximum(m - 1, 0), 0)), pl.BlockSpec((3, tm, d), lambda m: (0, m, 0))),
        out_shape=(jax.ShapeDtypeStruct((2, t, d), F32), jax.ShapeDtypeStruct((3, CA_PAD + t, d), BF16)),
        compiler_params=_cp(("arbitrary",)),
    )(h, w, gq, gk)


def _ca_unprep(dq, dkv, qkv32, gq, gk):
    _, t, d = qkv32.shape
    tr = _tile(t, 512)
    nr = t // tr

    def body(dq_ref, dkv_ref, x_ref, gq_ref, gk_ref, o_ref, dgq_ref, dgk_ref):
        p, r = pl.program_id(0), pl.program_id(1)

        @pl.when(jnp.logical_and(p == 0, r == 0))
        def _():
            dgq_ref[...] = jnp.zeros_like(dgq_ref)
            dgk_ref[...] = jnp.zeros_like(dgk_ref)

        @pl.when(p == 2)
        def _():
            o_ref[...] = dkv_ref[...].astype(BF16)

        @pl.when(p < 2)
        def _():
            ones = _pair_sum_matrix()
            g = jnp.where(p == 0, gq_ref[...], gk_ref[...])
            dg = jnp.zeros((1, PAIR), F32)
            for c in range(d // PAIR):
                cols = slice(c * PAIR, (c + 1) * PAIR)
                x = x_ref[:, cols]
                dy = jnp.where(p == 0, dq_ref[:, cols], dkv_ref[:, cols])
                rs = lax.rsqrt(_head_mean(x * x, ones) + RMS_EPS)
                xhat = x * rs
                dxn = dy * g
                o_ref[:, cols] = (rs * (dxn - xhat * _head_mean(dxn * xhat, ones))).astype(BF16)
                dg = dg + jnp.sum(dy * xhat, axis=0, keepdims=True)
            dg = dg + pltpu.roll(dg, HEAD_DIM, 1)

            @pl.when(p == 0)
            def _():
                dgq_ref[...] += dg

            @pl.when(p == 1)
            def _():
                dgk_ref[...] += dg

    vec = pl.BlockSpec((1, PAIR), lambda p, r: (0, 0))
    return pl.pallas_call(
        body, name="ca_unprep", grid=(3, nr),
        in_specs=[pl.BlockSpec((tr, d), lambda p, r: (jnp.where(p == 0, r, nr - 1), 0)),
                  pl.BlockSpec((None, tr, d), lambda p, r: (jnp.maximum(p - 1, 0), jnp.where(p == 0, 0, r), 0)),
                  pl.BlockSpec((None, tr, d), lambda p, r: (jnp.minimum(p, 1), jnp.where(p == 2, nr - 1, r), 0)), vec, vec],
        out_specs=(pl.BlockSpec((None, tr, d), lambda p, r: (p, r, 0)), vec, vec),
        out_shape=(jax.ShapeDtypeStruct((3, t, d), BF16), jax.ShapeDtypeStruct((1, PAIR), F32), jax.ShapeDtypeStruct((1, PAIR), F32)),
        compiler_params=_cp(("arbitrary", "arbitrary")),
    )(dq, dkv, qkv32, gq, gk)


def _ca_ext_index():
    m = np.arange(CA_EXT)
    return np.where(m <= CA_WIN, np.clip(CA_PAD - m, -MAX_REL, MAX_REL) + MAX_REL, 2 * MAX_REL).astype(np.int32)


def _ca_pad_penalty(i):
    b = lax.broadcasted_iota(jnp.int32, (1, CA_WIN), 1)
    return jnp.where(b >= CA_PAD - CA_TQ * i, 0.0, NEG_BIG)


def _skew(x, sign):
    row = lax.broadcasted_iota(jnp.int32, (CA_TQ, 1), 0)
    for bit in range(CA_TQ.bit_length() - 1):
        amount = (1 << bit) if sign > 0 else CA_EXT - (1 << bit)
        x = jnp.where(((row >> bit) & 1) == 1, pltpu.roll(x, amount, 1), x)
    return x


def _ca_bias_tiles(ext_ref, bias_ref):
    a = lax.broadcasted_iota(jnp.int32, (CA_TQ, CA_EXT), 0) // CHUNK
    b = lax.broadcasted_iota(jnp.int32, (CA_TQ, CA_EXT), 1) // CHUNK
    seen = jnp.logical_and(b >= a, b <= a + LEFT_CHUNKS)
    for h in (0, 1):
        bias_ref[h] = jnp.where(seen, _skew(jnp.broadcast_to(ext_ref[pl.ds(h, 1), :], (CA_TQ, CA_EXT)), 1), NEG_BIG)


def _ca_probs(qh, kw, bias, penalty):
    z = _dot_nt(qh, kw) + bias + penalty
    p = jnp.exp(z - jnp.max(z, axis=1, keepdims=True))
    return p * (1.0 / jnp.sum(p, axis=1, keepdims=True))


def _ca_fwd(qkvn, ext, side=None):
    _, tp, d = qkvn.shape
    t = tp - CA_PAD
    hp, tq = d // PAIR, CA_SUB * CA_TQ
    npad = CA_PAD // tq

    def body(q_ref, k_ref, v_ref, ext_ref, o_ref, bias_ref):
        i = pl.program_id(1)

        @pl.when(i == 0)
        def _():
            _ca_bias_tiles(ext_ref, bias_ref)

        lane = lax.broadcasted_iota(jnp.int32, (1, PAIR), 1)
        for sub in range(CA_SUB):
            ii = i * CA_SUB + sub
            penalty = _ca_pad_penalty(ii)
            win = pl.ds(pl.multiple_of(ii * CA_TQ, CA_TQ), CA_WIN)
            kw, vw, q = k_ref[win, :], v_ref[win, :], q_ref[sub * CA_TQ:(sub + 1) * CA_TQ, :]
            head = [(lane // HEAD_DIM) == h for h in (0, 1)]
            q2 = jnp.concatenate([jnp.where(hm, q, jnp.zeros_like(q)) for hm in head], axis=0) * jnp.asarray(HEAD_DIM ** -0.5, BF16)
            bias2 = jnp.concatenate([bias_ref[h, :, :CA_WIN] for h in (0, 1)], axis=0)
            pb = _ca_probs(q2, kw, bias2, penalty).astype(BF16)
            o = jnp.zeros((CA_TQ, PAIR), F32)
            for h in (0, 1):
                o = o + _dot(pb[h * CA_TQ:(h + 1) * CA_TQ], jnp.where(head[h], vw, jnp.zeros_like(vw)))
            o_ref[sub * CA_TQ:(sub + 1) * CA_TQ, :] = o.astype(BF16)

    return _side_call(
        "ca_fwd", body, (hp, t // tq),
        [pl.BlockSpec((None, tq, PAIR), lambda p, i: (0, i + npad, p)),
         pl.BlockSpec((None, tp, PAIR), lambda p, i: (1, 0, p)),
         pl.BlockSpec((None, tp, PAIR), lambda p, i: (2, 0, p)),
         pl.BlockSpec((None, 2, CA_EXT), lambda p, i: (p, 0, 0))],
        [pl.BlockSpec((tq, PAIR), lambda p, i: (i, p))], [jax.ShapeDtypeStruct((t, d), BF16)],
        [pltpu.VMEM((2, CA_TQ, CA_EXT), F32)], (qkvn, qkvn, qkvn, ext), side, ("parallel", "arbitrary"))


def _ca_bwd(qkvn, ext, do, side=None):
    _, tp, d = qkvn.shape
    t = tp - CA_PAD
    hp, tq = d // PAIR, CA_SUB * CA_TQ
    npad, nq = CA_PAD // tq, t // tq
    scale = HEAD_DIM ** -0.5

    def body(q_ref, k_ref, v_ref, ext_ref, do_ref, dq_ref, dkv_ref, dext_ref, bias_ref, dbias_ref, dk_acc, dv_acc):
        i = pl.program_id(1)

        @pl.when(i == 0)
        def _():
            _ca_bias_tiles(ext_ref, bias_ref)
            dbias_ref[...] = jnp.zeros_like(dbias_ref)
            dk_acc[...] = jnp.zeros_like(dk_acc)
            dv_acc[...] = jnp.zeros_like(dv_acc)

        lane = lax.broadcasted_iota(jnp.int32, (1, PAIR), 1)
        dbias = [None, None]
        dk_u = dv_u = None

        def spread(v, sub):
            parts = [jnp.zeros((sub * CA_TQ, PAIR), F32)] * (sub > 0) + [v] + \
                    [jnp.zeros(((CA_SUB - 1 - sub) * CA_TQ, PAIR), F32)] * (sub < CA_SUB - 1)
            return jnp.concatenate(parts, axis=0) if len(parts) > 1 else v

        for sub in range(CA_SUB):
            ii = i * CA_SUB + sub
            penalty = _ca_pad_penalty(ii)
            win = pl.ds(pl.multiple_of(ii * CA_TQ, CA_TQ), CA_WIN)
            rows = slice(sub * CA_TQ, (sub + 1) * CA_TQ)
            kw, vw, q, dob = k_ref[win, :], v_ref[win, :], q_ref[rows, :], do_ref[rows, :]
            head = [(lane // HEAD_DIM) == h for h in (0, 1)]
            zb = jnp.zeros_like(q)
            q2 = jnp.concatenate([jnp.where(hm, q, zb) for hm in head], axis=0) * jnp.asarray(scale, BF16)
            do2 = jnp.concatenate([jnp.where(hm, dob, zb) for hm in head], axis=0)
            p = _ca_probs(q2, kw, jnp.concatenate([bias_ref[h, :, :CA_WIN] for h in (0, 1)], axis=0), penalty)
            dp = _dot_nt(do2, vw)
            ds = p * (dp - jnp.sum(p * dp, axis=1, keepdims=True))
            dsb = ds.astype(BF16)
            dq = jnp.zeros((CA_TQ, PAIR), F32)
            for h in (0, 1):
                ds_h = ds[h * CA_TQ:(h + 1) * CA_TQ]
                dbias[h] = ds_h if dbias[h] is None else dbias[h] + ds_h
                dq = dq + _dot(dsb[h * CA_TQ:(h + 1) * CA_TQ], jnp.where(head[h], kw, jnp.zeros_like(kw)))
            dk_t = _dot_tn(dsb, q2)
            dv_t = _dot_tn(p.astype(BF16), do2)
            dq_ref[rows, :] = dq * scale
            dk_u = spread(dk_t, sub) if dk_u is None else dk_u + spread(dk_t, sub)
            dv_u = spread(dv_t, sub) if dv_u is None else dv_u + spread(dv_t, sub)
        union = pl.ds(pl.multiple_of(i * tq, CA_TQ), CA_WIN + (CA_SUB - 1) * CA_TQ)
        dk_acc[union, :] += dk_u
        dv_acc[union, :] += dv_u
        for h in (0, 1):
            dbias_ref[h, :, :CA_WIN] += dbias[h]

        @pl.when(i == nq - 1)
        def _():
            dkv_ref[0, :, :] = dk_acc[CA_PAD:, :]
            dkv_ref[1, :, :] = dv_acc[CA_PAD:, :]
            for h in (0, 1):
                dext_ref[pl.ds(h, 1), :] = jnp.sum(_skew(dbias_ref[h], -1), axis=0, keepdims=True)

    es = pl.BlockSpec((None, 2, CA_EXT), lambda p, i: (p, 0, 0))
    ts = pl.BlockSpec((tq, PAIR), lambda p, i: (i, p))
    return _side_call(
        "ca_bwd", body, (hp, nq),
        [pl.BlockSpec((None, tq, PAIR), lambda p, i: (0, i + npad, p)),
         pl.BlockSpec((None, tp, PAIR), lambda p, i: (1, 0, p)),
         pl.BlockSpec((None, tp, PAIR), lambda p, i: (2, 0, p)), es, ts],
        (ts, pl.BlockSpec((2, t, PAIR), lambda p, i: (0, 0, p)), es),
        (jax.ShapeDtypeStruct((t, d), F32), jax.ShapeDtypeStruct((2, t, d), F32), jax.ShapeDtypeStruct((hp, 2, CA_EXT), F32)),
        [pltpu.VMEM((2, CA_TQ, CA_EXT), F32), pltpu.VMEM((2, CA_TQ, CA_EXT), F32),
         pltpu.VMEM((tp, PAIR), F32), pltpu.VMEM((tp, PAIR), F32)],
        (qkvn, qkvn, qkvn, ext, do), side, ("parallel", "arbitrary"))


def _loss_head(y, target):
    t, d = y.shape
    tm = _tile(t, 512)

    def body(y_ref, t_ref, dy_ref, loss_ref):
        diff = y_ref[...] - t_ref[...]
        dy_ref[...] = diff * (1.0 / d)
        part = 0.5 * jnp.sum(jnp.mean(diff * diff, axis=-1, keepdims=True), axis=0, keepdims=True)

        @pl.when(pl.program_id(0) == 0)
        def _():
            loss_ref[...] = jnp.zeros_like(loss_ref)

        loss_ref[...] += jnp.broadcast_to(part, loss_ref.shape)

    row = pl.BlockSpec((tm, d), lambda m: (m, 0))
    return pl.pallas_call(
        body, name="loss_head", grid=(t // tm,), in_specs=[row, row],
        out_specs=(row, pl.BlockSpec((8, 128), lambda m: (0, 0))),
        out_shape=(jax.ShapeDtypeStruct((t, d), F32), jax.ShapeDtypeStruct((8, 128), F32)),
        compiler_params=_cp(("arbitrary",)),
    )(y, target)


def _adamw_math(w, g, m, v):
    m = ADAM_B1 * m + (1.0 - ADAM_B1) * g
    v = ADAM_B2 * v + (1.0 - ADAM_B2) * (g * g)
    m_hat = m / (1.0 - ADAM_B1 ** ADAM_STEP)
    v_hat = v / (1.0 - ADAM_B2 ** ADAM_STEP)
    delta = -ADAM_LR * (m_hat / (jnp.sqrt(v_hat) + ADAM_EPS) + ADAM_WD * w)
    return delta, m, v


def _adamw(name, w, g, m, v):
    shape = w.shape
    cols = shape[-1]
    rows = int(np.prod(shape[:-1]))
    tr = _tile(rows, 512)
    flat = [a.reshape(rows, cols) for a in (w, g, m, v)]

    def body(w_ref, g_ref, m_ref, v_ref, d_ref, nm_ref, nv_ref):
        d_ref[...], nm_ref[...], nv_ref[...] = _adamw_math(w_ref[...], g_ref[...], m_ref[...], v_ref[...])

    blk = pl.BlockSpec((tr, cols), lambda r: (r, 0))
    outs = pl.pallas_call(
        body, name=name, grid=(rows // tr,), in_specs=[blk] * 4, out_specs=(blk,) * 3,
        out_shape=(jax.ShapeDtypeStruct((rows, cols), F32),) * 3, compiler_params=_cp(("parallel",)),
    )(*flat)
    return [o.reshape(shape) for o in outs]


def _place():
    x, y, c = lax.axis_index("x"), lax.axis_index("y"), lax.axis_index("c")
    chips = [(1 - x, y), (x, 1 - y), (1 - x, 1 - y)]
    return x, y, c, chips


def _shard_slab(ref, kind, layer0, n_layers, shard, width):
    lay = pl.ds(layer0, n_layers)
    if kind == "cols":
        return ref.at[lay, :, pl.ds(shard * width, width)]
    return ref.at[lay, pl.ds(shard * width, width), :]


def _cast_into_place(w, kind, shard):
    n_layers, rows, cols = w.shape
    tr = _tile(rows, 512)
    nr = rows // tr

    def body(s_ref, w_ref, o_ref):
        o_ref[...] = w_ref[...].astype(BF16)

    if kind == "cols":
        full, out = (n_layers, rows, cols * N_CHIPS), pl.BlockSpec((None, tr, cols), lambda l, r, s_ref: (l, r, s_ref[0]))
    else:
        full, out = (n_layers, rows * N_CHIPS, cols), pl.BlockSpec((None, tr, cols), lambda l, r, s_ref: (l, s_ref[0] * nr + r, 0))
    grid_spec = pltpu.PrefetchScalarGridSpec(
        num_scalar_prefetch=1, grid=(n_layers, nr),
        in_specs=[pl.BlockSpec((None, tr, cols), lambda l, r, s_ref: (l, r, 0))], out_specs=out)
    return pl.pallas_call(body, name="cast_into_place", grid_spec=grid_spec, out_shape=jax.ShapeDtypeStruct(full, BF16),
                          compiler_params=_cp(("parallel", "parallel")))(shard, w)


def _gather_side(items, fulls, kinds, widths):
    n = len(items)

    def each(w, send, recv, owner_does, other_does=None):
        x, y, c, chips = _place()
        for e, (a, layer) in enumerate(items):
            def copy(k, shard, to, a=a, layer=layer):
                slab = _shard_slab(w[a], kinds[a], layer, 1, shard, widths[a])
                return pltpu.make_async_remote_copy(src_ref=slab, dst_ref=slab, send_sem=send.at[k], recv_sem=recv.at[k],
                                                    device_id=to, device_id_type=MESH)

            @pl.when(c == layer % 2)
            def _(e=e, copy=copy):
                for j, (cx, cy) in enumerate(chips):
                    owner_does(copy, j * n + e, 3 * n + j * n + e, x, y, c, cx, cy)

            if other_does is not None:
                @pl.when(c != layer % 2)
                def _(e=e, copy=copy):
                    for j, (cx, cy) in enumerate(chips):
                        other_does(copy, 3 * n + j * n + e, x, y, c, cx, cy)

    def start(_, w, send, recv):
        each(w, send, recv, lambda copy, k, kf, x, y, c, cx, cy: copy(k, 2 * x + y, (cx, cy, c)).start())

    def mid(_, w, send, recv):
        def forward(copy, k, kf, x, y, c, cx, cy):
            copy(k, 2 * cx + cy, (x, y, c)).wait_recv()
            copy(kf, 2 * cx + cy, (x, y, 1 - c)).start()
        each(w, send, recv, forward)

    def finish(_, w, send, recv):
        def sent(copy, k, kf, x, y, c, cx, cy):
            copy(k, 2 * x + y, (cx, cy, c)).wait_send()
            copy(kf, 2 * cx + cy, (x, y, 1 - c)).wait_send()
        each(w, send, recv, sent, lambda copy, kf, x, y, c, cx, cy: copy(kf, 2 * cx + cy, (x, y, c)).wait_recv())

    return _Side(fulls, [True] * len(fulls), [], 6 * n, 6 * n, start, mid, finish)


def _shard_piece(ref, kind, shard, width):
    return ref.at[:, shard * width:(shard + 1) * width] if kind == "cols" else ref.at[shard * width:(shard + 1) * width, :]


def _reduce_side(layer, grads, kinds, widths):
    na = len(grads)
    owner = layer % 2
    new_outs = []
    for g, k, w in zip(grads, kinds, widths):
        new_outs.append(jax.ShapeDtypeStruct((N_DEV,) + ((g.shape[0], w) if k == "cols" else (w, g.shape[1])), g.dtype))

    def sends(g, got, send, recv, act):
        x, y, c, _ = _place()
        me = 4 * x + 2 * y + c
        for b in range(N_CHIPS):
            bx, by = b >> 1, b & 1
            to_me = jnp.logical_and(jnp.logical_and(x == bx, y == by), c == owner)

            @pl.when(jnp.logical_not(to_me))
            def _(b=b, bx=bx, by=by):
                for a in range(na):
                    cp = pltpu.make_async_remote_copy(
                        src_ref=_shard_piece(g[a], kinds[a], b, widths[a]), dst_ref=got[a].at[me], send_sem=send.at[b * na + a],
                        recv_sem=recv.at[me * na + a], device_id=(bx, by, owner), device_id_type=MESH)
                    cp.start() if act == "start" else cp.wait_send()

    def start(g, got, send, recv):
        sends(g, got, send, recv, "start")

    def mid(g, got, send, recv):
        pass

    def finish(g, got, send, recv):
        sends(g, got, send, recv, "wait")
        x, y, c, _ = _place()
        me = 4 * x + 2 * y + c

        @pl.when(c == owner)
        def _():
            for s in range(N_DEV):
                @pl.when(me != s)
                def _(s=s):
                    for a in range(na):
                        pltpu.make_async_remote_copy(
                            src_ref=_shard_piece(g[a], kinds[a], 0, widths[a]), dst_ref=got[a].at[s], send_sem=send.at[0],
                            recv_sem=recv.at[s * na + a], device_id=(x, y, c), device_id_type=MESH).wait_recv()

    return _Side(grads, [False] * na, new_outs, N_CHIPS * na, N_DEV * na, start, mid, finish)


def _reduce_sum(layer, n_layers, g, got, kind, width, prev, flags):
    _, rows, cols = got.shape
    tr = _tile(rows, 256)
    nr = rows // tr

    def body(f_ref, g_ref, got_ref, *rest):
        o_ref = rest[-1]

        @pl.when(f_ref[2] == 1)
        def _():
            acc = None
            for s in range(N_DEV):
                term = jnp.where(f_ref[1] == s, g_ref[...], got_ref[s]).astype(F32)
                acc = term if acc is None else acc + term
            o_ref[...] = acc

    if kind == "cols":
        own = pl.BlockSpec((tr, width), lambda r, f_ref: (r, f_ref[0]))
    else:
        own = pl.BlockSpec((tr, cols), lambda r, f_ref: (f_ref[0] * nr + r, 0))
    grid_spec = pltpu.PrefetchScalarGridSpec(
        num_scalar_prefetch=1, grid=(nr,),
        in_specs=[own, pl.BlockSpec((N_DEV, tr, cols), lambda r, f_ref: (0, r, 0))] + ([] if prev is None else [ANY]),
        out_specs=pl.BlockSpec((None, tr, cols), lambda r, f_ref: (layer, r, 0)))
    return pl.pallas_call(
        body, name="reduce_sum", grid_spec=grid_spec, out_shape=jax.ShapeDtypeStruct((n_layers, rows, cols), F32),
        input_output_aliases={} if prev is None else {3: 0}, compiler_params=_cp(("arbitrary",)),
    )(flags, g, got, *([] if prev is None else [prev]))


def _pair_share(sums):
    n_layers = sums[0].shape[0]
    na = len(sums)

    def body(*refs):
        dst = refs[na:2 * na]
        send_sems, recv_sems = refs[2 * na:]
        x, y, c, _ = _place()

        def swap(l, a):
            return pltpu.make_async_remote_copy(
                src_ref=dst[a].at[l], dst_ref=dst[a].at[l], send_sem=send_sems.at[l * na + a],
                recv_sem=recv_sems.at[l * na + a], device_id=(x, y, 1 - c), device_id_type=MESH)

        for l in range(n_layers):
            @pl.when(c == l % 2)
            def _(l=l):
                for a in range(na):
                    swap(l, a).start()
        for l in range(n_layers):
            @pl.when(c == l % 2)
            def _(l=l):
                for a in range(na):
                    swap(l, a).wait_send()

            @pl.when(c != l % 2)
            def _(l=l):
                for a in range(na):
                    swap(l, a).wait_recv()

    return pl.pallas_call(
        body, name="pair_share", in_specs=[ANY] * na, out_specs=[ANY] * na,
        out_shape=[jax.ShapeDtypeStruct(s.shape, s.dtype) for s in sums], input_output_aliases={a: a for a in range(na)},
        scratch_shapes=[pltpu.SemaphoreType.DMA((n_layers * na,)), pltpu.SemaphoreType.DMA((n_layers * na,))],
    )(*sums)


def _small_step(g_part, w, m, v):
    r = g_part.shape[0]

    def body(g_ref, w_ref, m_ref, v_ref, go_ref, d_ref, nm_ref, nv_ref, all_ref, send_sems, recv_sems):
        x, y, c, _ = _place()
        me = 4 * x + 2 * y + c
        all_ref[me] = g_ref[...]
        cps = []
        for k in range(1, N_DEV):
            px, py, pc = (x + (k >> 2)) % 2, (y + ((k >> 1) & 1)) % 2, (c + (k & 1)) % 2
            cps.append(pltpu.make_async_remote_copy(src_ref=g_ref, dst_ref=all_ref.at[me], send_sem=send_sems.at[k - 1],
                                                    recv_sem=recv_sems.at[k - 1], device_id=(px, py, pc), device_id_type=MESH))
        for cp in cps:
            cp.start()
        for cp in cps:
            cp.wait()
        g = all_ref[0]
        for k in range(1, N_DEV):
            g = g + all_ref[k]
        go_ref[...] = g
        d_ref[...], nm_ref[...], nv_ref[...] = _adamw_math(w_ref[...], g, m_ref[...], v_ref[...])

    vm = pl.BlockSpec(memory_space=pltpu.VMEM)
    return pl.pallas_call(
        body, name="small_step", in_specs=[vm] * 4, out_specs=[vm] * 4, out_shape=[jax.ShapeDtypeStruct((r, 128), F32)] * 4,
        scratch_shapes=[pltpu.VMEM((N_DEV, r, 128), F32), pltpu.SemaphoreType.DMA((N_DEV - 1,)), pltpu.SemaphoreType.DMA((N_DEV - 1,))],
    )(g_part, w, m, v)


def _onehot_mm(name, a, onehot):
    def body(a_ref, oh_ref, o_ref):
        v = a_ref[...]
        oh = oh_ref[...]
        hi, lo = _split_bf16(v)
        lo2 = (v - hi.astype(F32) - lo.astype(F32)).astype(BF16)
        o_ref[...] = _dot(hi, oh) + _dot(lo, oh) + _dot(lo2, oh)

    return pl.pallas_call(body, name=name, out_shape=jax.ShapeDtypeStruct((a.shape[0], onehot.shape[1]), F32))(a, onehot)


def _pack_small(parts):
    flat = jnp.concatenate([p.reshape(-1) for p in parts])
    n = flat.shape[0]
    rows = -(-n // 128)
    rows = -(-rows // 8) * 8
    return jnp.pad(flat, (0, rows * 128 - n)).reshape(rows, 128)


def _unpack_small(packed, like):
    flat = packed.reshape(-1)
    out, off = [], 0
    for p in like:
        out.append(flat[off:off + p.size].reshape(p.shape))
        off += p.size
    return out


def kernel(x, mix_norm, w_qkv, w_o, q_norm, k_norm, rel_bias, ffn_norm, w_up, w_down, loss_target, m_mix_norm, m_w_qkv, m_w_o, m_q_norm, m_k_norm, m_rel_bias, m_ffn_norm, m_w_up, m_w_down, v_mix_norm, v_w_qkv, v_w_o, v_q_norm, v_k_norm, v_rel_bias, v_ffn_norm, v_w_up, v_w_down):
    n_layers, d = mix_norm.shape
    t = x.shape[1]
    ff = w_down.shape[1] * N_CHIPS
    heads = d // HEAD_DIM
    cx, cy, cc = lax.axis_index("x"), lax.axis_index("y"), lax.axis_index("c")
    shard = (2 * cx + cy).astype(jnp.int32).reshape(1)
    core = cc.astype(jnp.int32).reshape(1)

    big = [w_qkv, w_o, w_up, w_down]
    kinds = ["cols", "rows", "cols", "rows"]
    widths = [w_qkv.shape[2], w_o.shape[1], w_up.shape[2], w_down.shape[1]]
    fulls = _blocking("gather_first", _gather_side([(0, 0)], [_cast_into_place(w, k, shard) for w, k in zip(big, kinds)],
                                                   kinds, widths))

    rel_pad = -(-N_REL // PAIR) * PAIR
    ext_hot = _ca_ext_index()[:, None] == np.arange(rel_pad)[None, :]
    fold_hot, spread_hot = jnp.asarray(ext_hot, BF16), jnp.asarray(ext_hot.T, BF16)
    rel_tab = jnp.pad(rel_bias, ((0, 0), (0, 0), (0, rel_pad - N_REL)))
    exts = [_onehot_mm("relbias_spread", rel_tab[i], spread_hot).reshape(heads // 2, 2, CA_EXT) for i in range(n_layers // 2)]

    xs, h1s, qkvs, os_, o32s, x2s, h2s, ss, us, q32s = [], [], [], [], [], [], [], [], [], []
    xc = x[0]
    h = _rms_first(xc, mix_norm[0:1])
    for layer in range(n_layers):
        xs.append(xc)
        h1s.append(h)
        side = _gather_side([(1, layer), (2, layer), (3, layer)] + ([(0, layer + 1)] if layer + 1 < n_layers else []),
                            fulls, kinds, widths)
        if layer % 2 == 0:
            qkv = _qkv_proj(h, fulls[0], layer, BF16)
            (o, o32), gathered = _sb_fwd(qkv, side)
            q32s.append(None)
        else:
            idx = layer // 2
            gq = jnp.tile(q_norm[idx], 2).reshape(1, PAIR)
            gk = jnp.tile(k_norm[idx], 2).reshape(1, PAIR)
            q32, qkv = _qkv_proj_ca(h, fulls[0], layer, gq, gk)
            (o,), gathered = _ca_fwd(qkv, exts[idx], side)
            o32 = None
            q32s.append(q32)
        fulls = gathered
        wq, wo, wu, wd = fulls
        qkvs.append(qkv)
        os_.append(o)
        o32s.append(o32)
        x2, h2 = _out_proj("attn_out", o, wo, layer, xc, ffn_norm[layer:layer + 1])
        s, u = _up_proj(h2, wu, layer)
        nxt = mix_norm[layer + 1:layer + 2] if layer + 1 < n_layers else mix_norm[0:1]
        xc, h = _out_proj("mlp_out", u, wd, layer, x2, nxt)
        x2s.append(x2)
        h2s.append(h2)
        ss.append(s)
        us.append(u)

    dx, loss_part = _loss_head(xc, loss_target[0])
    loss = lax.psum(loss_part[0, 0], ("x", "y", "c"))

    d_mix, d_ffn = [None] * n_layers, [None] * n_layers
    d_qn, d_kn, d_rb = [], [], []
    device = (4 * cx + 2 * cy + cc).astype(jnp.int32).reshape(1)
    sums = [None] * len(big)

    def reduce_sums(of_layer, grads, got):
        flags = jnp.concatenate([shard, device, (core == of_layer % 2).astype(jnp.int32)])
        return [_reduce_sum(of_layer, n_layers, g, r_, k, w_, s_, flags)
                for g, r_, k, w_, s_ in zip(grads, got, kinds, widths, sums)]

    for layer in reversed(range(n_layers)):
        du = _down_bwd(dx, wd, layer, ss[layer])
        g_down = _wgrad("wgrad_down", us[layer], _mat_spec, dx, _mat_spec_b, t, ff, d)
        g_up = _wgrad("wgrad_up", h2s[layer], _mat_spec, du, _mat_spec_b, t, d, ff)
        dx2, d_ffn[layer] = _norm_bwd_proj(
            "up_bwd", du, _rows2(ff), [(None, slice(None), slice(None))], wu, layer, x2s[layer], ffn_norm[layer:layer + 1], dx)
        do = _plain_nt("attn_out_bwd", dx2, wo, layer, BF16)
        g_o = _wgrad("wgrad_o", os_[layer], _mat_spec, dx2, _mat_spec_b, t, d, d)
        side = _reduce_side(layer, [g_o, g_up, g_down], kinds[1:], widths[1:])
        if layer % 2 == 0:
            (dqkv,), got = _sb_bwd(qkvs[layer], o32s[layer], do, side)
        else:
            idx = layer // 2
            gq = jnp.tile(q_norm[idx], 2).reshape(1, PAIR)
            gk = jnp.tile(k_norm[idx], 2).reshape(1, PAIR)
            (dqn, dkv, dext), got = _ca_bwd(qkvs[layer], exts[idx], do, side)
            dqkv, dgq, dgk = _ca_unprep(dqn, dkv, q32s[layer], gq, gk)
            d_qn.insert(0, dgq[0, :HEAD_DIM])
            d_kn.insert(0, dgk[0, :HEAD_DIM])
            d_rb.insert(0, _onehot_mm("relbias_fold", dext.reshape(heads, CA_EXT), fold_hot)[:, :N_REL])
        g_qkv = _wgrad("wgrad_qkv", h1s[layer], _mat_spec, dqkv, _plane_spec_b(d), t, d, 3 * d, tno=_tile(d, 1024))
        (dx, d_mix[layer]), got_qkv = _norm_bwd_proj(
            "qkv_bwd", dqkv, lambda tm: pl.BlockSpec((3, tm, d), lambda m: (0, m, 0)),
            [(p, slice(None), slice(p * d, (p + 1) * d)) for p in range(3)], wq, layer, xs[layer], mix_norm[layer:layer + 1], dx2,
            side=_reduce_side(layer, [g_qkv], kinds[:1], widths[:1]))
        sums = reduce_sums(layer, [g_qkv, g_o, g_up, g_down], got_qkv + got)
    grad_x = dx.reshape(x.shape)
    g_big = _pair_share(sums)

    upd = [_adamw("adamw", w_, g_, m_, v_) for w_, g_, m_, v_ in
           zip(big, g_big, [m_w_qkv, m_w_o, m_w_up, m_w_down], [v_w_qkv, v_w_o, v_w_up, v_w_down])]

    small_w = [mix_norm, q_norm, k_norm, rel_bias, ffn_norm]
    small_g = [jnp.concatenate(d_mix, 0), jnp.stack(d_qn), jnp.stack(d_kn), jnp.stack(d_rb), jnp.concatenate(d_ffn, 0)]
    packed = _small_step(_pack_small(small_g), _pack_small(small_w),
                         _pack_small([m_mix_norm, m_q_norm, m_k_norm, m_rel_bias, m_ffn_norm]),
                         _pack_small([v_mix_norm, v_q_norm, v_k_norm, v_rel_bias, v_ffn_norm]))
    sg, sd, sm, sv = [_unpack_small(p, small_w) for p in packed]

    def order(small, bigs):
        return [small[0], bigs[0], bigs[1], small[1], small[2], small[3], small[4], bigs[2], bigs[3]]

    return (loss, grad_x, *order(sg, g_big), *order(sd, [u_[0] for u_ in upd]),
            *order(sm, [u_[1] for u_ in upd]), *order(sv, [u_[2] for u_ in upd]))
```

```python
import functools

import numpy as np
import jax
import jax.numpy as jnp
from jax import lax
from jax.experimental import pallas as pl
from jax.experimental.pallas import tpu as pltpu

F32, BF16 = jnp.float32, jnp.bfloat16
MESH = pl.DeviceIdType.MESH

HEAD_DIM = 64
PAIR = 128
CHUNK = 64
LEFT_CHUNKS = 8
MAX_REL = 256
N_REL = 2 * MAX_REL + 1
CA_TQ = 128
CA_SUB = 4
CA_PAD = LEFT_CHUNKS * CHUNK
assert CA_PAD % (CA_SUB * CA_TQ) == 0
CA_WIN = CA_PAD + CA_TQ
CA_EXT = CA_WIN + CA_TQ
SB_T = 128
SB_WIN = 3
SB_SUB = 8
RMS_EPS = 1e-6
SB_EXIT = -104.0
NEG_BIG = -1e30
ADAM_LR, ADAM_B1, ADAM_B2, ADAM_EPS, ADAM_WD, ADAM_STEP = 0.001, 0.9, 0.999, 1e-08, 0.01, 10
VMEM_LIMIT = 56 << 20
N_CHIPS = 4
N_DEV = 8
ANY = pl.BlockSpec(memory_space=pl.ANY)


def _cp(sem=None, **kw):
    return pltpu.CompilerParams(dimension_semantics=sem, vmem_limit_bytes=VMEM_LIMIT, **kw)


def _tile(n, want):
    t = min(n, want)
    assert n % t == 0, (n, t)
    return t


class _Side:
    def __init__(self, ins, aliased, new_outs, n_send, n_recv, start, mid, finish):
        self.ins, self.aliased, self.new_outs = list(ins), list(aliased), list(new_outs)
        self.n_send, self.n_recv, self.start, self.mid, self.finish = n_send, n_recv, start, mid, finish

    def out_shapes(self):
        return [jax.ShapeDtypeStruct(a.shape, a.dtype) for a, al in zip(self.ins, self.aliased) if al] + self.new_outs


def _side_call(name, body, grid, in_specs, out_specs, out_shape, scratch_shapes, args, side, sem):
    out_shape, out_specs = list(out_shape), list(out_specs)
    if side is None:
        outs = pl.pallas_call(body, name=name, grid=grid, in_specs=in_specs, out_specs=out_specs, out_shape=out_shape,
                              scratch_shapes=scratch_shapes, compiler_params=_cp(sem))(*args)
        return list(outs), []
    n_in, n_out, n_sc, n_sin = len(in_specs), len(out_shape), len(scratch_shapes), len(side.ins)
    s_outs = side.out_shapes()

    def wrapped(*refs):
        ins, s_in = refs[:n_in], refs[n_in:n_in + n_sin]
        outs = refs[n_in + n_sin:n_in + n_sin + n_out]
        s_out = refs[n_in + n_sin + n_out:n_in + n_sin + n_out + len(s_outs)]
        scr = refs[n_in + n_sin + n_out + len(s_outs):]
        send, recv = scr[n_sc], scr[n_sc + 1]
        ids = [pl.program_id(k) for k in range(len(grid))]

        def at(first_axis, rest):
            ok = ids[0] == first_axis
            for k in range(1, len(grid)):
                ok = jnp.logical_and(ok, ids[k] == (grid[k] - 1 if rest == "last" else 0))
            return ok

        @pl.when(at(0, "first"))
        def _():
            side.start(s_in, s_out, send, recv)

        body(*ins, *outs, *scr[:n_sc])

        @pl.when(at(grid[0] // 2, "first"))
        def _():
            side.mid(s_in, s_out, send, recv)

        @pl.when(at(grid[0] - 1, "last"))
        def _():
            side.finish(s_in, s_out, send, recv)

    aliases, pos = {}, n_out
    for k, al in enumerate(side.aliased):
        if al:
            aliases[n_in + k] = pos
            pos += 1
    outs = pl.pallas_call(
        wrapped, name=name, grid=grid, in_specs=list(in_specs) + [ANY] * n_sin, out_specs=out_specs + [ANY] * len(s_outs),
        out_shape=out_shape + s_outs, input_output_aliases=aliases,
        scratch_shapes=list(scratch_shapes) + [pltpu.SemaphoreType.DMA((side.n_send,)), pltpu.SemaphoreType.DMA((side.n_recv,))],
        compiler_params=_cp(("arbitrary",) * len(grid)),
    )(*args, *side.ins)
    return list(outs[:n_out]), list(outs[n_out:])


def _blocking(name, side):
    s_outs = side.out_shapes()
    n_sin = len(side.ins)

    def body(*refs):
        s_in, s_out = refs[:n_sin], refs[n_sin:n_sin + len(s_outs)]
        send, recv = refs[n_sin + len(s_outs):]
        side.start(s_in, s_out, send, recv)
        side.mid(s_in, s_out, send, recv)
        side.finish(s_in, s_out, send, recv)

    aliases, pos = {}, 0
    for k, al in enumerate(side.aliased):
        if al:
            aliases[k] = pos
            pos += 1
    return list(pl.pallas_call(
        body, name=name, in_specs=[ANY] * n_sin, out_specs=[ANY] * len(s_outs), out_shape=s_outs, input_output_aliases=aliases,
        scratch_shapes=[pltpu.SemaphoreType.DMA((side.n_send,)), pltpu.SemaphoreType.DMA((side.n_recv,))],
    )(*side.ins))


def _split_bf16(v):
    hi = v.astype(BF16)
    lo = (v - hi.astype(F32)).astype(BF16)
    return hi, lo


def _dot(a, b):
    return jnp.dot(a, b, preferred_element_type=F32)


def _dot_nt(a, b):
    return lax.dot_general(a, b, (((1,), (1,)), ((), ())), preferred_element_type=F32)


def _dot_tn(a, b):
    return lax.dot_general(a, b, (((0,), (0,)), ((), ())), preferred_element_type=F32)


TM = 512
TN = 1024


def _row_matmul(name, a, a_spec, w, layer, nt, chunks, extra, extra_specs, out_shape, out_specs, epilogue, t,
                sem=("parallel",), side=None):
    tm = _tile(t, TM)
    n_extra = len(extra)

    def body(a_ref, w_ref, *rest):
        ins, outs = rest[:n_extra], rest[n_extra:]
        for n, terms in enumerate(chunks):
            acc = None
            for plane, rows, cols in terms:
                av = (a_ref[...] if plane is None else a_ref[plane]).astype(BF16)
                part = _dot_nt(av, w_ref[rows, cols]) if nt else _dot(av, w_ref[rows, cols])
                acc = part if acc is None else acc + part
            epilogue(n, acc, ins, outs)

    w_spec = pl.BlockSpec((None,) + w.shape[1:], lambda m: (layer, 0, 0))
    outs, side_outs = _side_call(name, body, (t // tm,), [a_spec(tm), w_spec, *extra_specs], out_specs, out_shape, [],
                                 (a, w, *extra), side, sem)
    return outs if side is None else (outs, side_outs)


def _rows2(width):
    return lambda tm: pl.BlockSpec((tm, width), lambda m: (m, 0))


def _col_chunks(n, width=None):
    tn = _tile(n, TN)
    return [[(None, slice(None), slice(c * tn, (c + 1) * tn))] for c in range(n // tn)], tn


def _rms_rows(x, gain):
    r = lax.rsqrt(jnp.mean(x * x, axis=-1, keepdims=True) + RMS_EPS)
    return x * r * gain


def _rms_bwd_rows(dh, x, gain, dres):
    r = lax.rsqrt(jnp.mean(x * x, axis=-1, keepdims=True) + RMS_EPS)
    xhat = x * r
    dxn = dh * gain
    dx = r * (dxn - xhat * jnp.mean(dxn * xhat, axis=-1, keepdims=True))
    return dx + dres, jnp.sum(dh * xhat, axis=0, keepdims=True)


def _rms_first(x, gain):
    t, d = x.shape
    tm = _tile(t, 512)

    def body(x_ref, g_ref, h_ref):
        h_ref[...] = _rms_rows(x_ref[...], g_ref[...]).astype(BF16)

    return pl.pallas_call(
        body, name="rms_first", grid=(t // tm,),
        in_specs=[pl.BlockSpec((tm, d), lambda m: (m, 0)), pl.BlockSpec((1, d), lambda m: (0, 0))],
        out_specs=pl.BlockSpec((tm, d), lambda m: (m, 0)), out_shape=jax.ShapeDtypeStruct((t, d), BF16),
        compiler_params=_cp(("parallel",)),
    )(x, gain)


def _qkv_proj(h, w, layer, out_dtype):
    t, d = h.shape
    chunks = [[(None, slice(None), slice(p * d, (p + 1) * d))] for p in range(3)]

    def epi(n, acc, ins, outs):
        outs[0][n] = acc.astype(out_dtype)

    return _row_matmul(
        "qkv_proj", h, _rows2(d), w, layer, False, chunks, (), (), (jax.ShapeDtypeStruct((3, t, d), out_dtype),),
        (pl.BlockSpec((3, _tile(t, TM), d), lambda m: (0, m, 0)),), epi, t)[0]


def _out_proj(name, a, w, layer, res, gain):
    t, kk = a.shape
    d = res.shape[1]

    def epi(n, acc, ins, outs):
        xn = ins[0][...] + acc
        outs[0][...] = xn
        outs[1][...] = _rms_rows(xn, ins[1][...]).astype(BF16)

    row = _rows2(d)(_tile(t, TM))
    return _row_matmul(
        name, a, _rows2(kk), w, layer, False, [[(None, slice(None), slice(None))]], (res, gain),
        (row, pl.BlockSpec((1, d), lambda m: (0, 0))),
        (jax.ShapeDtypeStruct((t, d), F32), jax.ShapeDtypeStruct((t, d), BF16)), (row, row), epi, t)


def _up_proj(h, w, layer):
    t, d = h.shape
    ff = w.shape[2]
    chunks, tn = _col_chunks(ff)

    def epi(n, acc, ins, outs):
        s = jnp.maximum(acc, 0.0)
        outs[0][:, n * tn:(n + 1) * tn] = s.astype(BF16)
        outs[1][:, n * tn:(n + 1) * tn] = (s * s).astype(BF16)

    o = _rows2(ff)(_tile(t, TM))
    return _row_matmul("up_proj", h, _rows2(d), w, layer, False, chunks, (), (), (jax.ShapeDtypeStruct((t, ff), BF16),) * 2, (o, o), epi, t)


def _down_bwd(dx, w, layer, s):
    t, d = dx.shape
    ff = w.shape[1]
    tn = _tile(ff, TN)
    chunks = [[(None, slice(c * tn, (c + 1) * tn), slice(None))] for c in range(ff // tn)]

    def epi(n, acc, ins, outs):
        cols = slice(n * tn, (n + 1) * tn)
        outs[0][:, cols] = (acc * (2.0 * ins[0][:, cols].astype(F32))).astype(BF16)

    o = _rows2(ff)(_tile(t, TM))
    return _row_matmul("down_bwd", dx, _rows2(d), w, layer, True, chunks, (s,), (o,), (jax.ShapeDtypeStruct((t, ff), BF16),), (o,), epi, t)[0]


def _norm_bwd_proj(name, a, a_spec, terms, w, layer, x, gain, dres, side=None):
    t, d = x.shape

    def epi(n, acc, ins, outs):
        dx, dg = _rms_bwd_rows(acc, ins[0][...], ins[1][...], ins[2][...])
        outs[0][...] = dx

        @pl.when(pl.program_id(0) == 0)
        def _():
            outs[1][...] = dg

        @pl.when(pl.program_id(0) > 0)
        def _():
            outs[1][...] += dg

    row = _rows2(d)(_tile(t, TM))
    vec = pl.BlockSpec((1, d), lambda m: (0, 0))
    return _row_matmul(
        name, a, a_spec, w, layer, True, [terms], (x, gain, dres), (row, vec, row),
        (jax.ShapeDtypeStruct((t, d), F32), jax.ShapeDtypeStruct((1, d), F32)), (row, vec), epi, t, sem=("arbitrary",), side=side)


def _plain_nt(name, a, w, layer, out_dtype):
    t, n = a.shape
    m_out = w.shape[1]

    def epi(n_, acc, ins, outs):
        outs[0][...] = acc.astype(out_dtype)

    return _row_matmul(
        name, a, _rows2(n), w, layer, True, [[(None, slice(None), slice(None))]], (), (),
        (jax.ShapeDtypeStruct((t, m_out), out_dtype),), (_rows2(m_out)(_tile(t, TM)),), epi, t)[0]


def _wgrad(name, a, a_spec_fn, b, b_spec_fn, t, mo, no, tno=1024):
    tmo, tno, tk = _tile(mo, 1024), _tile(no, tno), _tile(t, 2048)
    nk = t // tk

    def body(a_ref, b_ref, o_ref, acc_ref):
        part = _dot_tn(a_ref[...].astype(BF16), b_ref[...].astype(BF16))
        k = pl.program_id(2)

        @pl.when(k == 0)
        def _():
            acc_ref[...] = part

        @pl.when(k > 0)
        def _():
            acc_ref[...] += part

        @pl.when(k == nk - 1)
        def _():
            o_ref[...] = acc_ref[...].astype(BF16)

    return pl.pallas_call(
        body, name=name, grid=(mo // tmo, no // tno, nk), in_specs=[a_spec_fn(tk, tmo), b_spec_fn(tk, tno)],
        out_specs=pl.BlockSpec((tmo, tno), lambda m, n, k: (m, n)), out_shape=jax.ShapeDtypeStruct((mo, no), BF16),
        scratch_shapes=[pltpu.VMEM((tmo, tno), F32)], compiler_params=_cp(("parallel", "parallel", "arbitrary")),
    )(a, b)


def _mat_spec(tk, tw):
    return pl.BlockSpec((tk, tw), lambda m, n, k: (k, m))


def _mat_spec_b(tk, tw):
    return pl.BlockSpec((tk, tw), lambda m, n, k: (k, n))


def _plane_spec_b(d):
    def fn(tk, tw):
        npp = d // tw
        return pl.BlockSpec((None, tk, tw), lambda m, n, k: (n // npp, k, n % npp))
    return fn


def _sb_masks(t):
    row = lax.broadcasted_iota(jnp.int32, (t, t), 0)
    col = lax.broadcasted_iota(jnp.int32, (t, t), 1)
    lane = lax.broadcasted_iota(jnp.int32, (1, PAIR), 1)
    return row, col, [(lane // HEAD_DIM) == h for h in (0, 1)]


def _suffix_sums(v, tri, tail, tt):
    hi, lo = _split_bf16(v)
    parts = []
    for b in reversed(range(v.shape[1] // tt)):
        cols = slice(b * tt, (b + 1) * tt)
        parts.insert(0, _dot(hi[:, cols], tri) + _dot(lo[:, cols], tri) + tail)
        tail = tail + jnp.sum(v[:, cols], axis=1, keepdims=True)
    return (parts[0] if len(parts) == 1 else jnp.concatenate(parts, axis=1)), tail


def _sb_tile(qh, kb, valid, after, carry, tt):
    z = _dot_nt(qh, kb)
    lb = -(jnp.maximum(z, 0.0) + jnp.log(1.0 + jnp.exp(-jnp.abs(z))))
    if valid is not None:
        lb = jnp.where(valid, lb, 0.0)
    between, carry = _suffix_sums(lb, after, carry, tt)
    a = jnp.exp(z + lb + between)
    if valid is not None:
        a = jnp.where(valid, a, 0.0)
    return lb, a, carry


def _sb_window_valid(i, first, tt):
    t_pos = i * tt + lax.broadcasted_iota(jnp.int32, (tt, SB_WIN * tt), 0)
    s_pos = first * tt + lax.broadcasted_iota(jnp.int32, (tt, SB_WIN * tt), 1)
    return s_pos < t_pos


def _sb_walk_on(st):
    return jnp.logical_and(st[0] >= 0, jnp.max(st[1]) > SB_EXIT)


def _sb_fwd(qkv, side=None):
    _, t, d = qkv.shape
    hp, tt = d // PAIR, _tile(t, SB_T)

    def body(q_ref, k_ref, v_ref, o_ref, o32_ref):
        row, col, head = _sb_masks(tt)
        after = (row > col).astype(BF16)

        def tile(kb, vb, valid, carry, acc, q2):
            _, a, carry = _sb_tile(q2, kb, valid, after, carry, tt)
            ab = a.astype(BF16)
            for h in (0, 1):
                acc = acc + _dot(ab[h * tt:(h + 1) * tt], jnp.where(head[h], vb, jnp.zeros_like(vb)))
            return carry, acc

        states = []
        for sub in range(SB_SUB):
            i = pl.program_id(1) * SB_SUB + sub
            q = q_ref[sub * tt:(sub + 1) * tt, :]
            q2 = jnp.concatenate([jnp.where(head[h], q, jnp.zeros_like(q)) for h in (0, 1)], axis=0) * jnp.asarray(HEAD_DIM ** -0.5, BF16)
            first = jnp.maximum(i - (SB_WIN - 1), 0)
            rows_w = pl.ds(pl.multiple_of(first * tt, tt), SB_WIN * tt)
            valid = _sb_window_valid(i, first, tt)
            carry, acc = tile(k_ref[rows_w, :], v_ref[rows_w, :], jnp.concatenate([valid, valid], axis=0),
                              jnp.zeros((2 * tt, 1), F32), jnp.zeros((tt, PAIR), F32), q2)
            states.append((first, carry, acc, q2))

        for sub, (first, carry, acc, q2) in enumerate(states):
            def step(st, q2=q2):
                rows = pl.ds(pl.multiple_of(st[0] * tt, tt), tt)
                carry, acc = tile(k_ref[rows, :], v_ref[rows, :], None, st[1], st[2], q2)
                return st[0] - 1, carry, acc

            o = lax.while_loop(_sb_walk_on, step, (first - 1, carry, acc))[2]
            o_ref[sub * tt:(sub + 1) * tt, :] = o.astype(BF16)
            o32_ref[sub * tt:(sub + 1) * tt, :] = o

    tq = SB_SUB * tt
    qs = pl.BlockSpec((None, tq, PAIR), lambda p, i: (0, i, p))
    ks = pl.BlockSpec((None, t, PAIR), lambda p, i: (1, 0, p))
    vs = pl.BlockSpec((None, t, PAIR), lambda p, i: (2, 0, p))
    os_ = pl.BlockSpec((tq, PAIR), lambda p, i: (i, p))
    return _side_call(
        "sb_fwd", body, (hp, t // tq), [qs, ks, vs], (os_, os_),
        (jax.ShapeDtypeStruct((t, d), BF16), jax.ShapeDtypeStruct((t, d), F32)), [], (qkv, qkv, qkv), side,
        ("parallel", "arbitrary"))


def _sb_bwd(qkv, o32, do, side=None):
    _, t, d = qkv.shape
    hp, tt = d // PAIR, _tile(t, SB_T)
    nq = t // (SB_SUB * tt)
    scale = HEAD_DIM ** -0.5

    def body(q_ref, k_ref, v_ref, o32_ref, do_ref, dqkv_ref, dk_acc, dv_acc):
        i = pl.program_id(1)
        row, col, head = _sb_masks(tt)
        after = (row > col).astype(BF16)
        from_s = (row >= col).astype(BF16)

        @pl.when(i == 0)
        def _():
            dk_acc[...] = jnp.zeros_like(dk_acc)
            dv_acc[...] = jnp.zeros_like(dv_acc)

        def tile(rows, valid, carry, seen, dq, q2, do2, tot):
            kb, vb = k_ref[rows, :], v_ref[rows, :]
            lb, a, carry = _sb_tile(q2, kb, valid, after, carry, tt)
            ab = a.astype(BF16)
            g = ab.astype(F32) * _dot_nt(do2, vb)
            g_from, seen = _suffix_sums(g, from_s, seen, tt)
            e = jnp.exp(lb)
            dz = g * e - (1.0 - e) * (tot - g_from)
            if valid is not None:
                dz = jnp.where(valid, dz, 0.0)
            dz = dz.astype(BF16)
            for h in (0, 1):
                dq = dq + _dot(dz[h * tt:(h + 1) * tt], jnp.where(head[h], kb, jnp.zeros_like(kb)))
            dk_acc[rows, :] += _dot_tn(dz, q2)
            dv_acc[rows, :] += _dot_tn(ab, do2)
            return carry, seen, dq

        states = []
        for sub in range(SB_SUB):
            ii = i * SB_SUB + sub
            q = q_ref[sub * tt:(sub + 1) * tt, :]
            dob = do_ref[sub * tt:(sub + 1) * tt, :]
            prod = dob.astype(F32) * o32_ref[sub * tt:(sub + 1) * tt, :]
            zb = jnp.zeros_like(q)
            q2 = jnp.concatenate([jnp.where(head[h], q, zb) for h in (0, 1)], axis=0) * jnp.asarray(scale, BF16)
            do2 = jnp.concatenate([jnp.where(head[h], dob, zb) for h in (0, 1)], axis=0)
            tot = jnp.concatenate([jnp.sum(jnp.where(head[h], prod, 0.0), axis=1, keepdims=True) for h in (0, 1)], axis=0)
            first = jnp.maximum(ii - (SB_WIN - 1), 0)
            valid = _sb_window_valid(ii, first, tt)
            zc = jnp.zeros((2 * tt, 1), F32)
            carry, seen, dq = tile(pl.ds(pl.multiple_of(first * tt, tt), SB_WIN * tt), jnp.concatenate([valid, valid], axis=0),
                                   zc, zc, jnp.zeros((tt, PAIR), F32), q2, do2, tot)
            states.append((first, carry, seen, dq, q2, do2, tot))

        for sub, (first, carry, seen, dq, q2, do2, tot) in enumerate(states):
            def step(st, q2=q2, do2=do2, tot=tot):
                carry, seen, dq = tile(pl.ds(pl.multiple_of(st[0] * tt, tt), tt), None, st[1], st[2], st[3], q2, do2, tot)
                return st[0] - 1, carry, seen, dq

            dq = lax.while_loop(_sb_walk_on, step, (first - 1, carry, seen, dq))[3]
            dqkv_ref[0, pl.ds(pl.multiple_of((i * SB_SUB + sub) * tt, tt), tt), :] = (dq * scale).astype(BF16)

        @pl.when(i == nq - 1)
        def _():
            dqkv_ref[1, :, :] = dk_acc[...].astype(BF16)
            dqkv_ref[2, :, :] = dv_acc[...].astype(BF16)

    qs = pl.BlockSpec((None, SB_SUB * tt, PAIR), lambda p, i: (0, i, p))
    ks = pl.BlockSpec((None, t, PAIR), lambda p, i: (1, 0, p))
    vs = pl.BlockSpec((None, t, PAIR), lambda p, i: (2, 0, p))
    ts = pl.BlockSpec((SB_SUB * tt, PAIR), lambda p, i: (i, p))
    return _side_call(
        "sb_bwd", body, (hp, nq), [qs, ks, vs, ts, ts], [pl.BlockSpec((3, t, PAIR), lambda p, i: (0, 0, p))],
        [jax.ShapeDtypeStruct((3, t, d), BF16)], [pltpu.VMEM((t, PAIR), F32), pltpu.VMEM((t, PAIR), F32)],
        (qkv, qkv, qkv, o32, do), side, ("parallel", "arbitrary"))


def _pair_sum_matrix():
    r = lax.broadcasted_iota(jnp.int32, (PAIR, PAIR), 0) // HEAD_DIM
    c = lax.broadcasted_iota(jnp.int32, (PAIR, PAIR), 1) // HEAD_DIM
    return (r == c).astype(BF16)


def _head_mean(v, ones):
    hi, lo = _split_bf16(v)
    return (_dot(hi, ones) + _dot(lo, ones)) * (1.0 / HEAD_DIM)


def _qkv_proj_ca(h, w, layer, gq, gk):
    t, d = h.shape
    tm = CA_PAD
    assert t % tm == 0

    def body(h_ref, w_ref, gq_ref, gk_ref, raw_ref, o_ref):
        @pl.when(pl.program_id(0) == 0)
        def _():
            o_ref[...] = jnp.zeros_like(o_ref)

        @pl.when(pl.program_id(0) > 0)
        def _():
            hv = h_ref[...]
            ones = _pair_sum_matrix()
            for p in range(3):
                acc = _dot(hv, w_ref[:, p * d:(p + 1) * d])
                if p == 2:
                    o_ref[p] = acc.astype(BF16)
                    continue
                raw_ref[p] = acc
                g = (gq_ref, gk_ref)[p][...]
                for c in range(d // PAIR):
                    x = acc[:, c * PAIR:(c + 1) * PAIR]
                    rs = lax.rsqrt(_head_mean(x * x, ones) + RMS_EPS)
                    o_ref[p, :, c * PAIR:(c + 1) * PAIR] = (x * rs * g).astype(BF16)

    vec = pl.BlockSpec((1, PAIR), lambda m: (0, 0))
    return pl.pallas_call(
        body, name="qkv_proj_ca", grid=(t // tm + 1,),
        in_specs=[pl.BlockSpec((tm, d), lambda m: (jnp.maximum(m - 1, 0), 0)),
                  pl.BlockSpec((None, d, 3 * d), lambda m: (layer, 0, 0)), vec, vec],
        out_specs=(pl.BlockSpec((2, tm, d), lambda m: (0, jnp.maximum(m - 1, 0), 0)), pl.BlockSpec((3, tm, d), lambda m: (0, m, 0))),
        out_shape=(jax.ShapeDtypeStruct((2, t, d), F32), jax.ShapeDtypeStruct((3, CA_PAD + t, d), BF16)),
        compiler_params=_cp(("arbitrary",)),
    )(h, w, gq, gk)


def _ca_unprep(dq, dkv, qkv32, gq, gk):
    _, t, d = qkv32.shape
    tr = _tile(t, 512)
    nr = t // tr

    def body(dq_ref, dkv_ref, x_ref, gq_ref, gk_ref, o_ref, dgq_ref, dgk_ref):
        p, r = pl.program_id(0), pl.program_id(1)

        @pl.when(jnp.logical_and(p == 0, r == 0))
        def _():
            dgq_ref[...] = jnp.zeros_like(dgq_ref)
            dgk_ref[...] = jnp.zeros_like(dgk_ref)

        @pl.when(p == 2)
        def _():
            o_ref[...] = dkv_ref[...].astype(BF16)

        @pl.when(p < 2)
        def _():
            ones = _pair_sum_matrix()
            g = jnp.where(p == 0, gq_ref[...], gk_ref[...])
            dg = jnp.zeros((1, PAIR), F32)
            for c in range(d // PAIR):
                cols = slice(c * PAIR, (c + 1) * PAIR)
                x = x_ref[:, cols]
                dy = jnp.where(p == 0, dq_ref[:, cols], dkv_ref[:, cols])
                rs = lax.rsqrt(_head_mean(x * x, ones) + RMS_EPS)
                xhat = x * rs
                dxn = dy * g
                o_ref[:, cols] = (rs * (dxn - xhat * _head_mean(dxn * xhat, ones))).astype(BF16)
                dg = dg + jnp.sum(dy * xhat, axis=0, keepdims=True)
            dg = dg + pltpu.roll(dg, HEAD_DIM, 1)

            @pl.when(p == 0)
            def _():
                dgq_ref[...] += dg

            @pl.when(p == 1)
            def _():
                dgk_ref[...] += dg

    vec = pl.BlockSpec((1, PAIR), lambda p, r: (0, 0))
    return pl.pallas_call(
        body, name="ca_unprep", grid=(3, nr),
        in_specs=[pl.BlockSpec((tr, d), lambda p, r: (jnp.where(p == 0, r, nr - 1), 0)),
                  pl.BlockSpec((None, tr, d), lambda p, r: (jnp.maximum(p - 1, 0), jnp.where(p == 0, 0, r), 0)),
                  pl.BlockSpec((None, tr, d), lambda p, r: (jnp.minimum(p, 1), jnp.where(p == 2, nr - 1, r), 0)), vec, vec],
        out_specs=(pl.BlockSpec((None, tr, d), lambda p, r: (p, r, 0)), vec, vec),
        out_shape=(jax.ShapeDtypeStruct((3, t, d), BF16), jax.ShapeDtypeStruct((1, PAIR), F32), jax.ShapeDtypeStruct((1, PAIR), F32)),
        compiler_params=_cp(("arbitrary", "arbitrary")),
    )(dq, dkv, qkv32, gq, gk)


def _ca_ext_index():
    m = np.arange(CA_EXT)
    return np.where(m <= CA_WIN, np.clip(CA_PAD - m, -MAX_REL, MAX_REL) + MAX_REL, 2 * MAX_REL).astype(np.int32)


def _ca_pad_penalty(i):
    b = lax.broadcasted_iota(jnp.int32, (1, CA_WIN), 1)
    return jnp.where(b >= CA_PAD - CA_TQ * i, 0.0, NEG_BIG)


def _skew(x, sign):
    row = lax.broadcasted_iota(jnp.int32, (CA_TQ, 1), 0)
    for bit in range(CA_TQ.bit_length() - 1):
        amount = (1 << bit) if sign > 0 else CA_EXT - (1 << bit)
        x = jnp.where(((row >> bit) & 1) == 1, pltpu.roll(x, amount, 1), x)
    return x


def _ca_bias_tiles(ext_ref, bias_ref):
    a = lax.broadcasted_iota(jnp.int32, (CA_TQ, CA_EXT), 0) // CHUNK
    b = lax.broadcasted_iota(jnp.int32, (CA_TQ, CA_EXT), 1) // CHUNK
    seen = jnp.logical_and(b >= a, b <= a + LEFT_CHUNKS)
    for h in (0, 1):
        bias_ref[h] = jnp.where(seen, _skew(jnp.broadcast_to(ext_ref[pl.ds(h, 1), :], (CA_TQ, CA_EXT)), 1), NEG_BIG)


def _ca_exp(qh, kw, bias, penalty):
    z = _dot_nt(qh, kw) + bias + penalty
    e = jnp.exp(z - jnp.max(z, axis=1, keepdims=True))
    return e, 1.0 / jnp.sum(e, axis=1, keepdims=True)


def _ca_probs(qh, kw, bias, penalty):
    e, inv = _ca_exp(qh, kw, bias, penalty)
    return e * inv


def _ca_fwd(qkvn, ext, side=None):
    _, tp, d = qkvn.shape
    t = tp - CA_PAD
    hp, tq = d // PAIR, CA_SUB * CA_TQ
    npad = CA_PAD // tq

    def body(q_ref, k_ref, v_ref, ext_ref, o_ref, bias_ref):
        i = pl.program_id(1)

        @pl.when(i == 0)
        def _():
            _ca_bias_tiles(ext_ref, bias_ref)

        lane = lax.broadcasted_iota(jnp.int32, (1, PAIR), 1)
        for sub in range(CA_SUB):
            ii = i * CA_SUB + sub
            penalty = _ca_pad_penalty(ii)
            win = pl.ds(pl.multiple_of(ii * CA_TQ, CA_TQ), CA_WIN)
            kw, vw, q = k_ref[win, :], v_ref[win, :], q_ref[sub * CA_TQ:(sub + 1) * CA_TQ, :]
            head = [(lane // HEAD_DIM) == h for h in (0, 1)]
            q2 = jnp.concatenate([jnp.where(hm, q, jnp.zeros_like(q)) for hm in head], axis=0) * jnp.asarray(HEAD_DIM ** -0.5, BF16)
            bias2 = jnp.concatenate([bias_ref[h, :, :CA_WIN] for h in (0, 1)], axis=0)
            e, inv = _ca_exp(q2, kw, bias2, penalty)
            eb = e.astype(BF16)
            o = jnp.zeros((CA_TQ, PAIR), F32)
            for h in (0, 1):
                rows = slice(h * CA_TQ, (h + 1) * CA_TQ)
                o = o + _dot(eb[rows], jnp.where(head[h], vw, jnp.zeros_like(vw))) * inv[rows]
            o_ref[sub * CA_TQ:(sub + 1) * CA_TQ, :] = o.astype(BF16)

    return _side_call(
        "ca_fwd", body, (hp, t // tq),
        [pl.BlockSpec((None, tq, PAIR), lambda p, i: (0, i + npad, p)),
         pl.BlockSpec((None, tp, PAIR), lambda p, i: (1, 0, p)),
         pl.BlockSpec((None, tp, PAIR), lambda p, i: (2, 0, p)),
         pl.BlockSpec((None, 2, CA_EXT), lambda p, i: (p, 0, 0))],
        [pl.BlockSpec((tq, PAIR), lambda p, i: (i, p))], [jax.ShapeDtypeStruct((t, d), BF16)],
        [pltpu.VMEM((2, CA_TQ, CA_EXT), F32)], (qkvn, qkvn, qkvn, ext), side, ("parallel", "arbitrary"))


def _ca_bwd(qkvn, ext, do, side=None):
    _, tp, d = qkvn.shape
    t = tp - CA_PAD
    hp, tq = d // PAIR, CA_SUB * CA_TQ
    npad, nq = CA_PAD // tq, t // tq
    scale = HEAD_DIM ** -0.5

    def body(q_ref, k_ref, v_ref, ext_ref, do_ref, dq_ref, dkv_ref, dext_ref, bias_ref, dbias_ref, dk_acc, dv_acc):
        i = pl.program_id(1)

        @pl.when(i == 0)
        def _():
            _ca_bias_tiles(ext_ref, bias_ref)
            dbias_ref[...] = jnp.zeros_like(dbias_ref)
            dk_acc[...] = jnp.zeros_like(dk_acc)
            dv_acc[...] = jnp.zeros_like(dv_acc)

        lane = lax.broadcasted_iota(jnp.int32, (1, PAIR), 1)
        dbias = [None, None]
        dk_u = dv_u = None

        def spread(v, sub):
            parts = [jnp.zeros((sub * CA_TQ, PAIR), F32)] * (sub > 0) + [v] + \
                    [jnp.zeros(((CA_SUB - 1 - sub) * CA_TQ, PAIR), F32)] * (sub < CA_SUB - 1)
            return jnp.concatenate(parts, axis=0) if len(parts) > 1 else v

        for sub in range(CA_SUB):
            ii = i * CA_SUB + sub
            penalty = _ca_pad_penalty(ii)
            win = pl.ds(pl.multiple_of(ii * CA_TQ, CA_TQ), CA_WIN)
            rows = slice(sub * CA_TQ, (sub + 1) * CA_TQ)
            kw, vw, q, dob = k_ref[win, :], v_ref[win, :], q_ref[rows, :], do_ref[rows, :]
            head = [(lane // HEAD_DIM) == h for h in (0, 1)]
            zb = jnp.zeros_like(q)
            q2 = jnp.concatenate([jnp.where(hm, q, zb) for hm in head], axis=0) * jnp.asarray(scale, BF16)
            do2 = jnp.concatenate([jnp.where(hm, dob, zb) for hm in head], axis=0)
            p = _ca_probs(q2, kw, jnp.concatenate([bias_ref[h, :, :CA_WIN] for h in (0, 1)], axis=0), penalty)
            dp = _dot_nt(do2, vw)
            ds = p * (dp - jnp.sum(p * dp, axis=1, keepdims=True))
            dsb = ds.astype(BF16)
            dq = jnp.zeros((CA_TQ, PAIR), F32)
            for h in (0, 1):
                ds_h = ds[h * CA_TQ:(h + 1) * CA_TQ]
                dbias[h] = ds_h if dbias[h] is None else dbias[h] + ds_h
                dq = dq + _dot(dsb[h * CA_TQ:(h + 1) * CA_TQ], jnp.where(head[h], kw, jnp.zeros_like(kw)))
            dk_t = _dot_tn(dsb, q2)
            dv_t = _dot_tn(p.astype(BF16), do2)
            dq_ref[rows, :] = dq * scale
            dk_u = spread(dk_t, sub) if dk_u is None else dk_u + spread(dk_t, sub)
            dv_u = spread(dv_t, sub) if dv_u is None else dv_u + spread(dv_t, sub)
        union = pl.ds(pl.multiple_of(i * tq, CA_TQ), CA_WIN + (CA_SUB - 1) * CA_TQ)
        dk_acc[union, :] += dk_u
        dv_acc[union, :] += dv_u
        for h in (0, 1):
            dbias_ref[h, :, :CA_WIN] += dbias[h]

        @pl.when(i == nq - 1)
        def _():
            dkv_ref[0, :, :] = dk_acc[CA_PAD:, :]
            dkv_ref[1, :, :] = dv_acc[CA_PAD:, :]
            for h in (0, 1):
                dext_ref[pl.ds(h, 1), :] = jnp.sum(_skew(dbias_ref[h], -1), axis=0, keepdims=True)

    es = pl.BlockSpec((None, 2, CA_EXT), lambda p, i: (p, 0, 0))
    ts = pl.BlockSpec((tq, PAIR), lambda p, i: (i, p))
    return _side_call(
        "ca_bwd", body, (hp, nq),
        [pl.BlockSpec((None, tq, PAIR), lambda p, i: (0, i + npad, p)),
         pl.BlockSpec((None, tp, PAIR), lambda p, i: (1, 0, p)),
         pl.BlockSpec((None, tp, PAIR), lambda p, i: (2, 0, p)), es, ts],
        (ts, pl.BlockSpec((2, t, PAIR), lambda p, i: (0, 0, p)), es),
        (jax.ShapeDtypeStruct((t, d), F32), jax.ShapeDtypeStruct((2, t, d), F32), jax.ShapeDtypeStruct((hp, 2, CA_EXT), F32)),
        [pltpu.VMEM((2, CA_TQ, CA_EXT), F32), pltpu.VMEM((2, CA_TQ, CA_EXT), F32),
         pltpu.VMEM((tp, PAIR), F32), pltpu.VMEM((tp, PAIR), F32)],
        (qkvn, qkvn, qkvn, ext, do), side, ("parallel", "arbitrary"))


def _loss_head(y, target):
    t, d = y.shape
    tm = _tile(t, 512)

    def body(y_ref, t_ref, dy_ref, loss_ref):
        diff = y_ref[...] - t_ref[...]
        dy_ref[...] = diff * (1.0 / d)
        part = 0.5 * jnp.sum(jnp.mean(diff * diff, axis=-1, keepdims=True), axis=0, keepdims=True)

        @pl.when(pl.program_id(0) == 0)
        def _():
            loss_ref[...] = jnp.zeros_like(loss_ref)

        loss_ref[...] += jnp.broadcast_to(part, loss_ref.shape)

    row = pl.BlockSpec((tm, d), lambda m: (m, 0))
    return pl.pallas_call(
        body, name="loss_head", grid=(t // tm,), in_specs=[row, row],
        out_specs=(row, pl.BlockSpec((8, 128), lambda m: (0, 0))),
        out_shape=(jax.ShapeDtypeStruct((t, d), F32), jax.ShapeDtypeStruct((8, 128), F32)),
        compiler_params=_cp(("arbitrary",)),
    )(y, target)


def _adamw_math(w, g, m, v):
    m = ADAM_B1 * m + (1.0 - ADAM_B1) * g
    v = ADAM_B2 * v + (1.0 - ADAM_B2) * (g * g)
    m_hat = m / (1.0 - ADAM_B1 ** ADAM_STEP)
    v_hat = v / (1.0 - ADAM_B2 ** ADAM_STEP)
    delta = -ADAM_LR * (m_hat / (jnp.sqrt(v_hat) + ADAM_EPS) + ADAM_WD * w)
    return delta, m, v


def _adamw(name, w, g, m, v):
    shape = w.shape
    cols = shape[-1]
    rows = int(np.prod(shape[:-1]))
    tr = _tile(rows, 512)
    flat = [a.reshape(rows, cols) for a in (w, g, m, v)]

    def body(w_ref, g_ref, m_ref, v_ref, d_ref, nm_ref, nv_ref):
        d_ref[...], nm_ref[...], nv_ref[...] = _adamw_math(w_ref[...], g_ref[...], m_ref[...], v_ref[...])

    blk = pl.BlockSpec((tr, cols), lambda r: (r, 0))
    outs = pl.pallas_call(
        body, name=name, grid=(rows // tr,), in_specs=[blk] * 4, out_specs=(blk,) * 3,
        out_shape=(jax.ShapeDtypeStruct((rows, cols), F32),) * 3, compiler_params=_cp(("parallel",)),
    )(*flat)
    return [o.reshape(shape) for o in outs]


def _place():
    x, y, c = lax.axis_index("x"), lax.axis_index("y"), lax.axis_index("c")
    chips = [(1 - x, y), (x, 1 - y), (1 - x, 1 - y)]
    return x, y, c, chips


def _shard_slab(ref, kind, layer0, n_layers, shard, width):
    lay = pl.ds(layer0, n_layers)
    if kind == "cols":
        return ref.at[lay, :, pl.ds(shard * width, width)]
    return ref.at[lay, pl.ds(shard * width, width), :]


def _cast_into_place(w, kind, shard):
    n_layers, rows, cols = w.shape
    tr = _tile(rows, 512)
    nr = rows // tr

    def body(s_ref, w_ref, o_ref):
        o_ref[...] = w_ref[...].astype(BF16)

    if kind == "cols":
        full, out = (n_layers, rows, cols * N_CHIPS), pl.BlockSpec((None, tr, cols), lambda l, r, s_ref: (l, r, s_ref[0]))
    else:
        full, out = (n_layers, rows * N_CHIPS, cols), pl.BlockSpec((None, tr, cols), lambda l, r, s_ref: (l, s_ref[0] * nr + r, 0))
    grid_spec = pltpu.PrefetchScalarGridSpec(
        num_scalar_prefetch=1, grid=(n_layers, nr),
        in_specs=[pl.BlockSpec((None, tr, cols), lambda l, r, s_ref: (l, r, 0))], out_specs=out)
    return pl.pallas_call(body, name="cast_into_place", grid_spec=grid_spec, out_shape=jax.ShapeDtypeStruct(full, BF16),
                          compiler_params=_cp(("parallel", "parallel")))(shard, w)


def _gather_side(items, fulls, kinds, widths):
    n = len(items)

    def each(w, send, recv, owner_does, other_does=None):
        x, y, c, chips = _place()
        for e, (a, layer) in enumerate(items):
            def copy(k, shard, to, a=a, layer=layer):
                slab = _shard_slab(w[a], kinds[a], layer, 1, shard, widths[a])
                return pltpu.make_async_remote_copy(src_ref=slab, dst_ref=slab, send_sem=send.at[k], recv_sem=recv.at[k],
                                                    device_id=to, device_id_type=MESH)

            @pl.when(c == layer % 2)
            def _(e=e, copy=copy):
                for j, (cx, cy) in enumerate(chips):
                    owner_does(copy, j * n + e, 3 * n + j * n + e, x, y, c, cx, cy)

            if other_does is not None:
                @pl.when(c != layer % 2)
                def _(e=e, copy=copy):
                    for j, (cx, cy) in enumerate(chips):
                        other_does(copy, 3 * n + j * n + e, x, y, c, cx, cy)

    def start(_, w, send, recv):
        each(w, send, recv, lambda copy, k, kf, x, y, c, cx, cy: copy(k, 2 * x + y, (cx, cy, c)).start())

    def mid(_, w, send, recv):
        def forward(copy, k, kf, x, y, c, cx, cy):
            copy(k, 2 * cx + cy, (x, y, c)).wait_recv()
            copy(kf, 2 * cx + cy, (x, y, 1 - c)).start()
        each(w, send, recv, forward)

    def finish(_, w, send, recv):
        def sent(copy, k, kf, x, y, c, cx, cy):
            copy(k, 2 * x + y, (cx, cy, c)).wait_send()
            copy(kf, 2 * cx + cy, (x, y, 1 - c)).wait_send()
        each(w, send, recv, sent, lambda copy, kf, x, y, c, cx, cy: copy(kf, 2 * cx + cy, (x, y, c)).wait_recv())

    return _Side(fulls, [True] * len(fulls), [], 6 * n, 6 * n, start, mid, finish)


def _shard_piece(ref, kind, shard, width):
    return ref.at[:, shard * width:(shard + 1) * width] if kind == "cols" else ref.at[shard * width:(shard + 1) * width, :]


def _reduce_side(layer, grads, kinds, widths):
    na = len(grads)
    owner = layer % 2
    new_outs = []
    for g, k, w in zip(grads, kinds, widths):
        new_outs.append(jax.ShapeDtypeStruct((N_DEV,) + ((g.shape[0], w) if k == "cols" else (w, g.shape[1])), g.dtype))

    def sends(g, got, send, recv, act):
        x, y, c, _ = _place()
        me = 4 * x + 2 * y + c
        for b in range(N_CHIPS):
            bx, by = b >> 1, b & 1
            to_me = jnp.logical_and(jnp.logical_and(x == bx, y == by), c == owner)

            @pl.when(jnp.logical_not(to_me))
            def _(b=b, bx=bx, by=by):
                for a in range(na):
                    cp = pltpu.make_async_remote_copy(
                        src_ref=_shard_piece(g[a], kinds[a], b, widths[a]), dst_ref=got[a].at[me], send_sem=send.at[b * na + a],
                        recv_sem=recv.at[me * na + a], device_id=(bx, by, owner), device_id_type=MESH)
                    cp.start() if act == "start" else cp.wait_send()

    def start(g, got, send, recv):
        sends(g, got, send, recv, "start")

    def mid(g, got, send, recv):
        pass

    def finish(g, got, send, recv):
        sends(g, got, send, recv, "wait")
        x, y, c, _ = _place()
        me = 4 * x + 2 * y + c

        @pl.when(c == owner)
        def _():
            for s in range(N_DEV):
                @pl.when(me != s)
                def _(s=s):
                    for a in range(na):
                        pltpu.make_async_remote_copy(
                            src_ref=_shard_piece(g[a], kinds[a], 0, widths[a]), dst_ref=got[a].at[s], send_sem=send.at[0],
                            recv_sem=recv.at[s * na + a], device_id=(x, y, c), device_id_type=MESH).wait_recv()

    return _Side(grads, [False] * na, new_outs, N_CHIPS * na, N_DEV * na, start, mid, finish)


def _reduce_sum(layer, n_layers, g, got, kind, width, prev, flags):
    _, rows, cols = got.shape
    tr = _tile(rows, 256)
    nr = rows // tr

    def body(f_ref, g_ref, got_ref, *rest):
        o_ref = rest[-1]

        @pl.when(f_ref[2] == 1)
        def _():
            acc = None
            for s in range(N_DEV):
                term = jnp.where(f_ref[1] == s, g_ref[...], got_ref[s]).astype(F32)
                acc = term if acc is None else acc + term
            o_ref[...] = acc

    if kind == "cols":
        own = pl.BlockSpec((tr, width), lambda r, f_ref: (r, f_ref[0]))
    else:
        own = pl.BlockSpec((tr, cols), lambda r, f_ref: (f_ref[0] * nr + r, 0))
    grid_spec = pltpu.PrefetchScalarGridSpec(
        num_scalar_prefetch=1, grid=(nr,),
        in_specs=[own, pl.BlockSpec((N_DEV, tr, cols), lambda r, f_ref: (0, r, 0))] + ([] if prev is None else [ANY]),
        out_specs=pl.BlockSpec((None, tr, cols), lambda r, f_ref: (layer, r, 0)))
    return pl.pallas_call(
        body, name="reduce_sum", grid_spec=grid_spec, out_shape=jax.ShapeDtypeStruct((n_layers, rows, cols), F32),
        input_output_aliases={} if prev is None else {3: 0}, compiler_params=_cp(("arbitrary",)),
    )(flags, g, got, *([] if prev is None else [prev]))


def _pair_share(sums):
    n_layers = sums[0].shape[0]
    na = len(sums)

    def body(*refs):
        dst = refs[na:2 * na]
        send_sems, recv_sems = refs[2 * na:]
        x, y, c, _ = _place()

        def swap(l, a):
            return pltpu.make_async_remote_copy(
                src_ref=dst[a].at[l], dst_ref=dst[a].at[l], send_sem=send_sems.at[l * na + a],
                recv_sem=recv_sems.at[l * na + a], device_id=(x, y, 1 - c), device_id_type=MESH)

        for l in range(n_layers):
            @pl.when(c == l % 2)
            def _(l=l):
                for a in range(na):
                    swap(l, a).start()
        for l in range(n_layers):
            @pl.when(c == l % 2)
            def _(l=l):
                for a in range(na):
                    swap(l, a).wait_send()

            @pl.when(c != l % 2)
            def _(l=l):
                for a in range(na):
                    swap(l, a).wait_recv()

    return pl.pallas_call(
        body, name="pair_share", in_specs=[ANY] * na, out_specs=[ANY] * na,
        out_shape=[jax.ShapeDtypeStruct(s.shape, s.dtype) for s in sums], input_output_aliases={a: a for a in range(na)},
        scratch_shapes=[pltpu.SemaphoreType.DMA((n_layers * na,)), pltpu.SemaphoreType.DMA((n_layers * na,))],
    )(*sums)


def _small_step(g_part, w, m, v):
    r = g_part.shape[0]

    def body(g_ref, w_ref, m_ref, v_ref, go_ref, d_ref, nm_ref, nv_ref, all_ref, send_sems, recv_sems):
        x, y, c, _ = _place()
        me = 4 * x + 2 * y + c
        all_ref[me] = g_ref[...]
        cps = []
        for k in range(1, N_DEV):
            px, py, pc = (x + (k >> 2)) % 2, (y + ((k >> 1) & 1)) % 2, (c + (k & 1)) % 2
            cps.append(pltpu.make_async_remote_copy(src_ref=g_ref, dst_ref=all_ref.at[me], send_sem=send_sems.at[k - 1],
                                                    recv_sem=recv_sems.at[k - 1], device_id=(px, py, pc), device_id_type=MESH))
        for cp in cps:
            cp.start()
        for cp in cps:
            cp.wait()
        g = all_ref[0]
        for k in range(1, N_DEV):
            g = g + all_ref[k]
        go_ref[...] = g
        d_ref[...], nm_ref[...], nv_ref[...] = _adamw_math(w_ref[...], g, m_ref[...], v_ref[...])

    vm = pl.BlockSpec(memory_space=pltpu.VMEM)
    return pl.pallas_call(
        body, name="small_step", in_specs=[vm] * 4, out_specs=[vm] * 4, out_shape=[jax.ShapeDtypeStruct((r, 128), F32)] * 4,
        scratch_shapes=[pltpu.VMEM((N_DEV, r, 128), F32), pltpu.SemaphoreType.DMA((N_DEV - 1,)), pltpu.SemaphoreType.DMA((N_DEV - 1,))],
    )(g_part, w, m, v)


def _onehot_mm(name, a, onehot):
    def body(a_ref, oh_ref, o_ref):
        v = a_ref[...]
        oh = oh_ref[...]
        hi, lo = _split_bf16(v)
        lo2 = (v - hi.astype(F32) - lo.astype(F32)).astype(BF16)
        o_ref[...] = _dot(hi, oh) + _dot(lo, oh) + _dot(lo2, oh)

    return pl.pallas_call(body, name=name, out_shape=jax.ShapeDtypeStruct((a.shape[0], onehot.shape[1]), F32))(a, onehot)


def _pack_small(parts):
    flat = jnp.concatenate([p.reshape(-1) for p in parts])
    n = flat.shape[0]
    rows = -(-n // 128)
    rows = -(-rows // 8) * 8
    return jnp.pad(flat, (0, rows * 128 - n)).reshape(rows, 128)


def _unpack_small(packed, like):
    flat = packed.reshape(-1)
    out, off = [], 0
    for p in like:
        out.append(flat[off:off + p.size].reshape(p.shape))
        off += p.size
    return out


def kernel(x, mix_norm, w_qkv, w_o, q_norm, k_norm, rel_bias, ffn_norm, w_up, w_down, loss_target, m_mix_norm, m_w_qkv, m_w_o, m_q_norm, m_k_norm, m_rel_bias, m_ffn_norm, m_w_up, m_w_down, v_mix_norm, v_w_qkv, v_w_o, v_q_norm, v_k_norm, v_rel_bias, v_ffn_norm, v_w_up, v_w_down):
    n_layers, d = mix_norm.shape
    t = x.shape[1]
    ff = w_down.shape[1] * N_CHIPS
    heads = d // HEAD_DIM
    cx, cy, cc = lax.axis_index("x"), lax.axis_index("y"), lax.axis_index("c")
    shard = (2 * cx + cy).astype(jnp.int32).reshape(1)
    core = cc.astype(jnp.int32).reshape(1)

    big = [w_qkv, w_o, w_up, w_down]
    kinds = ["cols", "rows", "cols", "rows"]
    widths = [w_qkv.shape[2], w_o.shape[1], w_up.shape[2], w_down.shape[1]]
    fulls = _blocking("gather_first", _gather_side([(0, 0)], [_cast_into_place(w, k, shard) for w, k in zip(big, kinds)],
                                                   kinds, widths))

    rel_pad = -(-N_REL // PAIR) * PAIR
    ext_hot = _ca_ext_index()[:, None] == np.arange(rel_pad)[None, :]
    fold_hot, spread_hot = jnp.asarray(ext_hot, BF16), jnp.asarray(ext_hot.T, BF16)
    rel_tab = jnp.pad(rel_bias, ((0, 0), (0, 0), (0, rel_pad - N_REL)))
    exts = [_onehot_mm("relbias_spread", rel_tab[i], spread_hot).reshape(heads // 2, 2, CA_EXT) for i in range(n_layers // 2)]

    xs, h1s, qkvs, os_, o32s, x2s, h2s, ss, us, q32s = [], [], [], [], [], [], [], [], [], []
    xc = x[0]
    h = _rms_first(xc, mix_norm[0:1])
    for layer in range(n_layers):
        xs.append(xc)
        h1s.append(h)
        side = _gather_side([(1, layer), (2, layer), (3, layer)] + ([(0, layer + 1)] if layer + 1 < n_layers else []),
                            fulls, kinds, widths)
        if layer % 2 == 0:
            qkv = _qkv_proj(h, fulls[0], layer, BF16)
            (o, o32), gathered = _sb_fwd(qkv, side)
            q32s.append(None)
        else:
            idx = layer // 2
            gq = jnp.tile(q_norm[idx], 2).reshape(1, PAIR)
            gk = jnp.tile(k_norm[idx], 2).reshape(1, PAIR)
            q32, qkv = _qkv_proj_ca(h, fulls[0], layer, gq, gk)
            (o,), gathered = _ca_fwd(qkv, exts[idx], side)
            o32 = None
            q32s.append(q32)
        fulls = gathered
        wq, wo, wu, wd = fulls
        qkvs.append(qkv)
        os_.append(o)
        o32s.append(o32)
        x2, h2 = _out_proj("attn_out", o, wo, layer, xc, ffn_norm[layer:layer + 1])
        s, u = _up_proj(h2, wu, layer)
        nxt = mix_norm[layer + 1:layer + 2] if layer + 1 < n_layers else mix_norm[0:1]
        xc, h = _out_proj("mlp_out", u, wd, layer, x2, nxt)
        x2s.append(x2)
        h2s.append(h2)
        ss.append(s)
        us.append(u)

    dx, loss_part = _loss_head(xc, loss_target[0])
    loss = lax.psum(loss_part[0, 0], ("x", "y", "c"))

    d_mix, d_ffn = [None] * n_layers, [None] * n_layers
    d_qn, d_kn, d_rb = [], [], []
    device = (4 * cx + 2 * cy + cc).astype(jnp.int32).reshape(1)
    sums = [None] * len(big)

    def reduce_sums(of_layer, grads, got):
        flags = jnp.concatenate([shard, device, (core == of_layer % 2).astype(jnp.int32)])
        return [_reduce_sum(of_layer, n_layers, g, r_, k, w_, s_, flags)
                for g, r_, k, w_, s_ in zip(grads, got, kinds, widths, sums)]

    for layer in reversed(range(n_layers)):
        du = _down_bwd(dx, wd, layer, ss[layer])
        g_down = _wgrad("wgrad_down", us[layer], _mat_spec, dx, _mat_spec_b, t, ff, d)
        g_up = _wgrad("wgrad_up", h2s[layer], _mat_spec, du, _mat_spec_b, t, d, ff)
        dx2, d_ffn[layer] = _norm_bwd_proj(
            "up_bwd", du, _rows2(ff), [(None, slice(None), slice(None))], wu, layer, x2s[layer], ffn_norm[layer:layer + 1], dx)
        do = _plain_nt("attn_out_bwd", dx2, wo, layer, BF16)
        g_o = _wgrad("wgrad_o", os_[layer], _mat_spec, dx2, _mat_spec_b, t, d, d)
        side = _reduce_side(layer, [g_o, g_up, g_down], kinds[1:], widths[1:])
        if layer % 2 == 0:
            (dqkv,), got = _sb_bwd(qkvs[layer], o32s[layer], do, side)
        else:
            idx = layer // 2
            gq = jnp.tile(q_norm[idx], 2).reshape(1, PAIR)
            gk = jnp.tile(k_norm[idx], 2).reshape(1, PAIR)
            (dqn, dkv, dext), got = _ca_bwd(qkvs[layer], exts[idx], do, side)
            dqkv, dgq, dgk = _ca_unprep(dqn, dkv, q32s[layer], gq, gk)
            d_qn.insert(0, dgq[0, :HEAD_DIM])
            d_kn.insert(0, dgk[0, :HEAD_DIM])
            d_rb.insert(0, _onehot_mm("relbias_fold", dext.reshape(heads, CA_EXT), fold_hot)[:, :N_REL])
        g_qkv = _wgrad("wgrad_qkv", h1s[layer], _mat_spec, dqkv, _plane_spec_b(d), t, d, 3 * d, tno=_tile(d, 1024))
        (dx, d_mix[layer]), got_qkv = _norm_bwd_proj(
            "qkv_bwd", dqkv, lambda tm: pl.BlockSpec((3, tm, d), lambda m: (0, m, 0)),
            [(p, slice(None), slice(p * d, (p + 1) * d)) for p in range(3)], wq, layer, xs[layer], mix_norm[layer:layer + 1], dx2,
            side=_reduce_side(layer, [g_qkv], kinds[:1], widths[:1]))
        sums = reduce_sums(layer, [g_qkv, g_o, g_up, g_down], got_qkv + got)
    grad_x = dx.reshape(x.shape)
    g_big = _pair_share(sums)

    upd = [_adamw("adamw", w_, g_, m_, v_) for w_, g_, m_, v_ in
           zip(big, g_big, [m_w_qkv, m_w_o, m_w_up, m_w_down], [v_w_qkv, v_w_o, v_w_up, v_w_down])]

    small_w = [mix_norm, q_norm, k_norm, rel_bias, ffn_norm]
    small_g = [jnp.concatenate(d_mix, 0), jnp.stack(d_qn), jnp.stack(d_kn), jnp.stack(d_rb), jnp.concatenate(d_ffn, 0)]
    packed = _small_step(_pack_small(small_g), _pack_small(small_w),
                         _pack_small([m_mix_norm, m_q_norm, m_k_norm, m_rel_bias, m_ffn_norm]),
                         _pack_small([v_mix_norm, v_q_norm, v_k_norm, v_rel_bias, v_ffn_norm]))
    sg, sd, sm, sv = [_unpack_small(p, small_w) for p in packed]

    def order(small, bigs):
        return [small[0], bigs[0], bigs[1], small[1], small[2], small[3], small[4], bigs[2], bigs[3]]

    return (loss, grad_x, *order(sg, g_big), *order(sd, [u_[0] for u_ in upd]),
            *order(sm, [u_[1] for u_ in upd]), *order(sv, [u_[2] for u_ in upd]))
```
